```python
import jax, jax.numpy as jnp
from jax import lax
import numpy as np

D_MODEL = 2048
BATCH = 8
SEQ = 2048
DEPTH = 4

GRID_W = 64
CTX_LEN = 256
MIX_W = D_MODEL
GROUP_W = MIX_W // 4
HEAD_DIM = 64
ATTN_SCALE = HEAD_DIM ** -0.5
ROPE_THETA = 10000.0
EPS = 1e-6
GLA_HEADS = 4
GLA_DK = GROUP_W // GLA_HEADS // 2
GLA_DV = GROUP_W // GLA_HEADS
GLA_RANK = 16
GLA_TAU = 16.0
GLA_CHUNK = 64
WIN_HEADS = GROUP_W // HEAD_DIM
WIN_KV = 2
WINDOW = 128
BLOCK = 128
CM_GROUPS = 4
CM_CHUNK = 128
CM_GW = GROUP_W // CM_GROUPS
GA_HEADS = GROUP_W // HEAD_DIM
GA_KV = 2
D_FF = -(-8 * D_MODEL // (3 * 256)) * 256

PROJ_SIZES = (
    ("a_q", GLA_HEADS * GLA_DK), ("a_k", GLA_HEADS * GLA_DK), ("a_v", GLA_HEADS * GLA_DV),
    ("a_g", GROUP_W), ("a_lrf", GLA_RANK), ("a_lrb", GLA_RANK),
    ("b_q", WIN_HEADS * HEAD_DIM), ("b_k", WIN_KV * HEAD_DIM), ("b_v", WIN_KV * HEAD_DIM),
    ("c_u", GROUP_W), ("c_v", GROUP_W),
    ("d_q", GA_HEADS * HEAD_DIM), ("d_k", GA_KV * HEAD_DIM), ("d_v", GA_KV * HEAD_DIM),
)
IN_W = (2 * GLA_HEADS * GLA_DK + GLA_HEADS * GLA_DV + GROUP_W + 2 * GLA_RANK
        + (WIN_HEADS + 2 * WIN_KV) * HEAD_DIM + 2 * GROUP_W + (GA_HEADS + 2 * GA_KV) * HEAD_DIM)

kernel_name = "hybrid_parallel_group_dit_trunk"


def rms_norm(x, g):
    xf = x.astype(jnp.float32)
    y = xf * lax.rsqrt(jnp.mean(xf * xf, axis=-1, keepdims=True) + EPS)
    return (y * g.astype(jnp.float32)).astype(x.dtype)


def layer_norm(x, g, b):
    xf = x.astype(jnp.float32)
    xc = xf - jnp.mean(xf, axis=-1, keepdims=True)
    y = xc * lax.rsqrt(jnp.mean(xc * xc, axis=-1, keepdims=True) + EPS)
    return (y * g.astype(jnp.float32) + b.astype(jnp.float32)).astype(x.dtype)


def modulate(h, shift, scale):
    return h * (1.0 + scale) + shift


def heads(t, h):
    return t.reshape(t.shape[:-1] + (h, t.shape[-1] // h))


def split_proj(p):
    out, off = {}, 0
    for name, w in PROJ_SIZES:
        out[name] = p[..., off:off + w]
        off += w
    return out


def axial_rope_tables(length):
    rows = length // GRID_W
    row = jnp.repeat(jnp.arange(rows), GRID_W).astype(jnp.float32)
    col = jnp.tile(jnp.arange(GRID_W), rows).astype(jnp.float32)
    quarter = HEAD_DIM // 4
    inv = ROPE_THETA ** (-jnp.arange(quarter, dtype=jnp.float32) / quarter)
    ar, ac = row[:, None] * inv, col[:, None] * inv
    ang = jnp.concatenate([ar, ar, ac, ac], axis=-1)
    return jnp.cos(ang), jnp.sin(ang)


def apply_rope(x, cos, sin):
    x1, x2, x3, x4 = jnp.split(x, 4, axis=-1)
    rot = jnp.concatenate([-x2, x1, -x4, x3], axis=-1)
    return x * cos[:, None, :].astype(x.dtype) + rot * sin[:, None, :].astype(x.dtype)


def gla_chunked(q, k, v, log_a, s0):
    b_, length, h_, dk = q.shape
    dv = v.shape[-1]
    nc = length // GLA_CHUNK

    def chunks(t):
        return jnp.moveaxis(t.reshape((b_, nc, GLA_CHUNK) + t.shape[2:]), 1, 0)

    tri = jnp.tril(jnp.ones((GLA_CHUNK, GLA_CHUNK), dtype=bool))

    def step(s, inp):
        qc, kc, vc, ac = inp
        cum = jnp.cumsum(ac, axis=1)
        diff = jnp.where(tri[None, :, :, None, None], cum[:, :, None] - cum[:, None, :], -jnp.inf)
        attn = jnp.einsum('bihk,bjhk,bijhk->bhij', qc, kc, jnp.exp(diff))
        o = (jnp.einsum('bhij,bjhv->bihv', attn, vc)
             + jnp.einsum('bihk,bhkv->bihv', qc * jnp.exp(cum), s))
        last = cum[:, -1]
        s = (jnp.exp(last)[..., None] * s
             + jnp.einsum('bjhk,bjhv->bhkv', kc * jnp.exp(last[:, None] - cum), vc))
        return s, o

    s, o = lax.scan(step, s0, (chunks(q), chunks(k), chunks(v), chunks(log_a)))
    return jnp.moveaxis(o, 0, 1).reshape(b_, length, h_, dv), s


def gla_mixer(pl, pc, gate_up, gate_b, gain, need_ctx):
    def qkv(p):
        q = heads(p['a_q'], GLA_HEADS).astype(jnp.float32) * GLA_DK ** -0.5
        k = heads(p['a_k'], GLA_HEADS).astype(jnp.float32)
        v = heads(p['a_v'], GLA_HEADS).astype(jnp.float32)
        return q, k, v

    def log_decay(lr, d):
        z = jnp.einsum('blr,rk->blk', lr.astype(jnp.float32), gate_up[d].astype(jnp.float32))
        z = z + gate_b[d].astype(jnp.float32)
        return heads(jax.nn.log_sigmoid(z) / GLA_TAU, GLA_HEADS)

    def flip(t):
        return jnp.flip(t, axis=1)

    qc, kc, vc = qkv(pc)
    ql, kl, vl = qkv(pl)
    s0 = jnp.zeros((qc.shape[0], GLA_HEADS, GLA_DK, GLA_DV), jnp.float32)
    oc_f, s_f = gla_chunked(qc, kc, vc, log_decay(pc['a_lrf'], 0), s0)
    ol_f, _ = gla_chunked(ql, kl, vl, log_decay(pl['a_lrf'], 0), s_f)
    oc_b, s_b = gla_chunked(flip(qc), flip(kc), flip(vc), flip(log_decay(pc['a_lrb'], 1)), s0)
    ol_b, _ = gla_chunked(flip(ql), flip(kl), flip(vl), flip(log_decay(pl['a_lrb'], 1)), s_b)

    def finish(o, g):
        o = rms_norm(o, gain.reshape(GLA_HEADS, GLA_DV))
        return (o.reshape(o.shape[:2] + (GROUP_W,)) * jax.nn.silu(g.astype(jnp.float32))).astype(g.dtype)

    out_l = finish(ol_f + flip(ol_b), pl['a_g'])
    out_c = finish(oc_f + flip(oc_b), pc['a_g']) if need_ctx else None
    return out_l, out_c


def window_attn(q, k, v, k_ctx, v_ctx, sink):
    b_, length = q.shape[:2]
    nb = length // BLOCK
    g_ = WIN_HEADS // WIN_KV
    qb = q.reshape(b_, nb, BLOCK, WIN_KV, g_, HEAD_DIM)

    def band(t):
        tp = jnp.pad(t, ((0, 0), (BLOCK, BLOCK), (0, 0), (0, 0)))
        tp = tp.reshape(b_, nb + 2, BLOCK, WIN_KV, HEAD_DIM)
        return jnp.concatenate([tp[:, :-2], tp[:, 1:-1], tp[:, 2:]], axis=2)

    kb, vb = band(k), band(v)
    s_win = jnp.einsum('bnqhgd,bnshd->bnhgqs', qb, kb).astype(jnp.float32) * ATTN_SCALE
    qpos = jnp.arange(nb)[:, None] * BLOCK + jnp.arange(BLOCK)[None, :]
    kpos = (jnp.arange(nb)[:, None] - 1) * BLOCK + jnp.arange(3 * BLOCK)[None, :]
    valid = ((jnp.abs(qpos[:, :, None] - kpos[:, None, :]) <= WINDOW)
             & (kpos[:, None, :] >= 0) & (kpos[:, None, :] < length))
    s_win = jnp.where(valid[None, :, None, None], s_win, -jnp.inf)
    s_ctx = jnp.einsum('bnqhgd,bshd->bnhgqs', qb, k_ctx).astype(jnp.float32) * ATTN_SCALE
    s_sink = jnp.broadcast_to(sink.reshape(WIN_KV, g_)[None, None, :, :, None, None].astype(jnp.float32),
                              s_win.shape[:-1] + (1,))
    p = jax.nn.softmax(jnp.concatenate([s_win, s_ctx, s_sink], axis=-1), axis=-1)
    nw, nctx = 3 * BLOCK, k_ctx.shape[1]
    p_win = p[..., :nw].astype(v.dtype)
    p_ctx = p[..., nw:nw + nctx].astype(v.dtype)
    o = (jnp.einsum('bnhgqs,bnshd->bnqhgd', p_win, vb)
         + jnp.einsum('bnhgqs,bshd->bnqhgd', p_ctx, v_ctx))
    return o.reshape(b_, length, WIN_HEADS * HEAD_DIM)


def global_attn(q, k, v, k_ctx, v_ctx):
    b_, length = q.shape[:2]
    nb = length // BLOCK
    g_ = GA_HEADS // GA_KV
    k_all = jnp.concatenate([k_ctx, k], axis=1)
    v_all = jnp.concatenate([v_ctx, v], axis=1)
    qb = jnp.moveaxis(q.reshape(b_, nb, BLOCK, GA_KV, g_, HEAD_DIM), 1, 0)

    def one_block(qblk):
        s = jnp.einsum('bqhgd,bshd->bhgqs', qblk, k_all).astype(jnp.float32) * ATTN_SCALE
        p = jax.nn.softmax(s, axis=-1).astype(v_all.dtype)
        return jnp.einsum('bhgqs,bshd->bqhgd', p, v_all)

    o = lax.map(one_block, qb)
    return jnp.moveaxis(o, 0, 1).reshape(b_, length, GA_HEADS * HEAD_DIM)


def context_attn(q, k, v, sink):
    b_, clen, hq, hd = q.shape
    kv = k.shape[2]
    g_ = hq // kv
    s = jnp.einsum('bqhgd,bshd->bhgqs', q.reshape(b_, clen, kv, g_, hd), k).astype(jnp.float32) * ATTN_SCALE
    if sink is not None:
        sk = jnp.broadcast_to(sink.reshape(kv, g_, 1, 1).astype(jnp.float32), s.shape[:-1] + (1,))
        p = jax.nn.softmax(jnp.concatenate([s, sk], axis=-1), axis=-1)[..., :-1]
    else:
        p = jax.nn.softmax(s, axis=-1)
    return jnp.einsum('bhgqs,bshd->bqhgd', p.astype(v.dtype), v).reshape(b_, clen, hq * hd)


def chunk_mlp(u, v, ln_g, ln_b, w_s, b_s):
    b_, length = u.shape[:2]
    nc = length // CM_CHUNK
    u = jax.nn.gelu(u)
    v = layer_norm(jax.nn.gelu(v), ln_g, ln_b)
    vb = v.reshape(b_, nc, CM_CHUNK, CM_GROUPS, CM_GW)
    s = jnp.einsum('gpq,bnqgc->bnpgc', w_s, vb) + b_s.T[:, :, None]
    return u * s.reshape(b_, length, GROUP_W)


def swiglu(h, w_in, w_out):
    gate, up = jnp.split(jnp.einsum('bld,df->blf', h, w_in), 2, axis=-1)
    return jnp.einsum('blf,fd->bld', jax.nn.silu(gate) * up, w_out)


def trunk_layer(x, ctx, mod_l, mod_c, cos, sin, norm_g, w_in, gla_gate_up, gla_gate_b, win_sink,
                cm_ln_g, cm_ln_b, cm_ws, cm_bs, qk_g, mix_g, w_out, w_ffn_in, w_ffn_out, need_ctx):
    sh1, sc1, gt1, sh2, sc2, gt2 = jnp.split(mod_l[:, None, :], 6, axis=-1)
    csh1, csc1, cgt1, csh2, csc2, cgt2 = jnp.split(mod_c, 6, axis=-1)
    hl = modulate(rms_norm(x, norm_g[0]), sh1, sc1)
    hc = modulate(rms_norm(ctx, norm_g[0]), csh1, csc1)
    pl = split_proj(jnp.einsum('bld,de->ble', hl, w_in))
    pc = split_proj(jnp.einsum('bld,de->ble', hc, w_in))
    ga, gb, gc, gd = jnp.split(mix_g, 4)

    a_l, a_c = gla_mixer(pl, pc, gla_gate_up, gla_gate_b, ga, need_ctx)
    kb_c, vb_c = heads(pc['b_k'], WIN_KV), heads(pc['b_v'], WIN_KV)
    b_l = window_attn(apply_rope(heads(pl['b_q'], WIN_HEADS), cos, sin),
                      apply_rope(heads(pl['b_k'], WIN_KV), cos, sin),
                      heads(pl['b_v'], WIN_KV), kb_c, vb_c, win_sink)
    c_l = chunk_mlp(pl['c_u'], pl['c_v'], cm_ln_g, cm_ln_b, cm_ws, cm_bs)
    kd_c = rms_norm(heads(pc['d_k'], GA_KV), qk_g[1])
    vd_c = heads(pc['d_v'], GA_KV)
    d_l = global_attn(apply_rope(rms_norm(heads(pl['d_q'], GA_HEADS), qk_g[0]), cos, sin),
                      apply_rope(rms_norm(heads(pl['d_k'], GA_KV), qk_g[1]), cos, sin),
                      heads(pl['d_v'], GA_KV), kd_c, vd_c)

    y_l = jnp.concatenate([a_l, rms_norm(b_l, gb), rms_norm(c_l, gc), rms_norm(d_l, gd)], axis=-1)
    x = x + gt1 * rms_norm(jnp.einsum('ble,ed->bld', y_l, w_out), norm_g[1])
    x = x + gt2 * rms_norm(swiglu(modulate(rms_norm(x, norm_g[2]), sh2, sc2), w_ffn_in, w_ffn_out), norm_g[3])

    if need_ctx:
        b_c = context_attn(heads(pc['b_q'], WIN_HEADS), kb_c, vb_c, win_sink)
        c_c = chunk_mlp(pc['c_u'], pc['c_v'], cm_ln_g, cm_ln_b, cm_ws, cm_bs)
        d_c = context_attn(rms_norm(heads(pc['d_q'], GA_HEADS), qk_g[0]), kd_c, vd_c, None)
        y_c = jnp.concatenate([a_c, rms_norm(b_c, gb), rms_norm(c_c, gc), rms_norm(d_c, gd)], axis=-1)
        ctx = ctx + cgt1 * rms_norm(jnp.einsum('ble,ed->bld', y_c, w_out), norm_g[1])
        ctx = ctx + cgt2 * rms_norm(swiglu(modulate(rms_norm(ctx, norm_g[2]), csh2, csc2), w_ffn_in, w_ffn_out), norm_g[3])
    return x, ctx


def setup_inputs(seed: int = 0) -> dict:
    key = jax.random.key(seed)
    ks = jax.random.split(key, 20)
    f32 = jnp.float32

    def nrm(k, shape, scale):
        return jax.random.normal(k, shape, f32) * scale

    return {
        "x": nrm(ks[0], (BATCH, SEQ, D_MODEL), 1.0),
        "c": nrm(ks[1], (BATCH, D_MODEL), 1.0),
        "ctx": nrm(ks[2], (BATCH, CTX_LEN, D_MODEL), 1.0),
        "c_ctx": nrm(ks[3], (D_MODEL,), 1.0),
        "w_mod": nrm(ks[4], (DEPTH, D_MODEL, 6 * D_MODEL), 0.5 * D_MODEL ** -0.5),
        "b_mod": nrm(ks[5], (DEPTH, 6 * D_MODEL), 0.02),
        "norm_g": 1.0 + nrm(ks[6], (DEPTH, 4, D_MODEL), 0.02),
        "w_in": nrm(ks[7], (DEPTH, D_MODEL, IN_W), D_MODEL ** -0.5),
        "gla_gate_up": nrm(ks[8], (DEPTH, 2, GLA_RANK, GLA_HEADS * GLA_DK), GLA_RANK ** -0.5),
        "gla_gate_b": nrm(ks[9], (DEPTH, 2, GLA_HEADS * GLA_DK), 0.1),
        "win_sink": nrm(ks[10], (DEPTH, WIN_HEADS), 0.5),
        "cm_ln_g": 1.0 + nrm(ks[11], (DEPTH, GROUP_W), 0.02),
        "cm_ln_b": nrm(ks[12], (DEPTH, GROUP_W), 0.02),
        "cm_ws": nrm(ks[13], (DEPTH, CM_GROUPS, CM_CHUNK, CM_CHUNK), CM_CHUNK ** -0.5),
        "cm_bs": nrm(ks[14], (DEPTH, CM_GROUPS, CM_CHUNK), 0.1),
        "qk_g": 1.0 + nrm(ks[15], (DEPTH, 2, HEAD_DIM), 0.02),
        "mix_g": 1.0 + nrm(ks[16], (DEPTH, MIX_W), 0.02),
        "w_out": nrm(ks[17], (DEPTH, MIX_W, D_MODEL), MIX_W ** -0.5),
        "w_ffn_in": nrm(ks[18], (DEPTH, D_MODEL, 2 * D_FF), D_MODEL ** -0.5),
        "w_ffn_out": nrm(ks[19], (DEPTH, D_FF, D_MODEL), D_FF ** -0.5),
    }


def reference(x, c, ctx, c_ctx, w_mod, b_mod, norm_g, w_in, gla_gate_up, gla_gate_b, win_sink,
              cm_ln_g, cm_ln_b, cm_ws, cm_bs, qk_g, mix_g, w_out, w_ffn_in, w_ffn_out):
    cos, sin = axial_rope_tables(x.shape[1])
    sc = jax.nn.silu(c)
    sc_ctx = jax.nn.silu(c_ctx)
    for l in range(DEPTH):
        mod_l = jnp.einsum('bd,de->be', sc, w_mod[l]) + b_mod[l]
        mod_c = jnp.einsum('d,de->e', sc_ctx, w_mod[l]) + b_mod[l]
        x, ctx = trunk_layer(x, ctx, mod_l, mod_c, cos, sin, norm_g[l], w_in[l], gla_gate_up[l],
                             gla_gate_b[l], win_sink[l], cm_ln_g[l], cm_ln_b[l], cm_ws[l], cm_bs[l],
                             qk_g[l], mix_g[l], w_out[l], w_ffn_in[l], w_ffn_out[l],
                             need_ctx=(l < DEPTH - 1))
    return x
```

```python
import functools
import math

import numpy as np
import jax
import jax.numpy as jnp
from jax import lax
from jax.experimental import pallas as pl
from jax.experimental.pallas import tpu as pltpu

F32 = jnp.float32
BF16 = jnp.bfloat16

EPS = 1e-6
HEAD_DIM = 64
ATTN_SCALE = HEAD_DIM ** -0.5
ROPE_THETA = 10000.0
GRID_W = 64
GLA_HEADS = 4
GLA_TAU = 16.0
GLA_CHUNK = 64
WIN_KV = 2
WINDOW = 128
CM_GROUPS = 4
CM_CHUNK = 128
GA_KV = 2

LANES = 128
NEG_BIG = -1e30
V7X_VMEM_LIMIT = 56 * 1024 * 1024


def _cparams(n_axes, vmem=V7X_VMEM_LIMIT):
    return pltpu.CompilerParams(dimension_semantics=("arbitrary",) * n_axes,
                                vmem_limit_bytes=vmem)


def _pick(total, candidates):
    for cand in candidates:
        if total % cand == 0:
            return cand
    raise ValueError(f"no tile in {candidates} divides {total}")


def _dot(a, b):
    return jnp.dot(a, b, preferred_element_type=F32)


def _dot_nt(a, b):
    return lax.dot_general(a, b, (((1,), (1,)), ((), ())), preferred_element_type=F32)


def _rms(x, gain):
    return x * lax.rsqrt(jnp.mean(x * x, axis=-1, keepdims=True) + EPS) * gain


def _silu(x):
    return x * jax.nn.sigmoid(x)


def _gelu_tanh(x):
    return 0.5 * x * (1.0 + jnp.tanh(math.sqrt(2.0 / math.pi) * (x + 0.044715 * (x * x * x))))


def _row_is_ctx(tile_idx, tiles_per_batch, tm, ctx_len):
    row0 = (tile_idx % tiles_per_batch) * tm
    rows = row0 + lax.broadcasted_iota(jnp.int32, (tm, 1), 0)
    return rows < ctx_len


def _mod_row(mod_ref, is_ctx, k):
    return jnp.where(is_ctx, mod_ref[0, k:k + 1, :], mod_ref[0, 8 + k:9 + k, :])


def _mod_kernel(c_ref, w_ref, b_ref, o_ref):
    s = _silu(c_ref[...]).astype(BF16)
    o_ref[0] = _dot(s, w_ref[0].astype(BF16)) + b_ref[0]


def _modulation(cond, w_mod, b_mod):
    depth, d, n = w_mod.shape
    rows = cond.shape[0]
    tn = _pick(n, (1024, 512, 256, 128))
    return pl.pallas_call(
        _mod_kernel,
        grid=(depth, n // tn),
        in_specs=[pl.BlockSpec((rows, d), lambda l, j: (0, 0)),
                  pl.BlockSpec((1, d, tn), lambda l, j: (l, 0, j)),
                  pl.BlockSpec((1, 1, tn), lambda l, j: (l, 0, j))],
        out_specs=pl.BlockSpec((1, rows, tn), lambda l, j: (l, 0, j)),
        out_shape=jax.ShapeDtypeStruct((depth, rows, n), F32),
        compiler_params=_cparams(2),
        name="modulation",
    )(cond, w_mod, b_mod.reshape(depth, 1, n))


def _inproj_kernel(x_ref, mod_ref, g_ref, w_ref, wlr_ref, p_ref, lr_ref, h_scr, *,
                   tm, tiles_per_batch, ctx_len):
    i = pl.program_id(0)
    j = pl.program_id(1)

    @pl.when(j == 0)
    def _():
        is_ctx = _row_is_ctx(i, tiles_per_batch, tm, ctx_len)
        y = _rms(x_ref[...], g_ref[...])
        h = y * (1.0 + _mod_row(mod_ref, is_ctx, 1)) + _mod_row(mod_ref, is_ctx, 0)
        h_scr[...] = h.astype(BF16)
        lr_ref[...] = _dot(h_scr[...], wlr_ref[...])

    p_ref[...] = _dot(h_scr[...], w_ref[...])


def _inproj(xs, modl, gain, w_main, w_lr, *, seq_t, ctx_len):
    m, d = xs.shape
    n = w_main.shape[1]
    tm = _pick(seq_t, (768, 384, 128))
    tn = _pick(n, (1024, 512, 256, 128))
    tpb = seq_t // tm
    kern = functools.partial(_inproj_kernel, tm=tm, tiles_per_batch=tpb, ctx_len=ctx_len)
    return pl.pallas_call(
        kern,
        grid=(m // tm, n // tn),
        in_specs=[pl.BlockSpec((tm, d), lambda i, j: (i, 0)),
                  pl.BlockSpec((1, 16, d), lambda i, j: (i // tpb, 0, 0)),
                  pl.BlockSpec((1, d), lambda i, j: (0, 0)),
                  pl.BlockSpec((d, tn), lambda i, j: (0, j)),
                  pl.BlockSpec((d, LANES), lambda i, j: (0, 0))],
        out_specs=[pl.BlockSpec((tm, tn), lambda i, j: (i, j)),
                   pl.BlockSpec((tm, LANES), lambda i, j: (i, 0))],
        out_shape=[jax.ShapeDtypeStruct((m, n), F32),
                   jax.ShapeDtypeStruct((m, LANES), F32)],
        scratch_shapes=[pltpu.VMEM((tm, d), BF16)],
        compiler_params=_cparams(2),
        name="inproj",
    )(xs, modl, gain, w_main, w_lr)


def _outproj_kernel(x_ref, ya_ref, yb_ref, yc_ref, yd_ref, mod_ref, g_ref, w_ref, o_ref, y_scr, *,
                    tm, tiles_per_batch, ctx_len, gw):
    i = pl.program_id(0)
    y_scr[:, 0 * gw:1 * gw] = ya_ref[...]
    y_scr[:, 1 * gw:2 * gw] = yb_ref[...]
    y_scr[:, 2 * gw:3 * gw] = yc_ref[...]
    y_scr[:, 3 * gw:4 * gw] = yd_ref[...]
    z = _dot(y_scr[...], w_ref[...])
    is_ctx = _row_is_ctx(i, tiles_per_batch, tm, ctx_len)
    o_ref[...] = x_ref[...] + _mod_row(mod_ref, is_ctx, 2) * _rms(z, g_ref[...])


def _outproj(xs, ys, modl, gain, w, *, seq_t, ctx_len):
    m, d = xs.shape
    gw = ys[0].shape[1]
    tm = _pick(seq_t, (384, 128))
    tpb = seq_t // tm
    kern = functools.partial(_outproj_kernel, tm=tm, tiles_per_batch=tpb, ctx_len=ctx_len, gw=gw)
    yspec = pl.BlockSpec((tm, gw), lambda i: (i, 0))
    return pl.pallas_call(
        kern,
        grid=(m // tm,),
        in_specs=[pl.BlockSpec((tm, d), lambda i: (i, 0)), yspec, yspec, yspec, yspec,
                  pl.BlockSpec((1, 16, d), lambda i: (i // tpb, 0, 0)),
                  pl.BlockSpec((1, d), lambda i: (0, 0)),
                  pl.BlockSpec((4 * gw, d), lambda i: (0, 0))],
        out_specs=pl.BlockSpec((tm, d), lambda i: (i, 0)),
        out_shape=jax.ShapeDtypeStruct((m, d), F32),
        scratch_shapes=[pltpu.VMEM((tm, 4 * gw), BF16)],
        compiler_params=_cparams(1),
        name="outproj",
    )(xs, *ys, modl, gain, w)


def _ffn_kernel(x_ref, mod_ref, g2_ref, g3_ref, wg_ref, wu_ref, wo_ref, o_ref, h_scr, acc_scr, *,
                tm, tiles_per_batch, ctx_len, nf):
    i = pl.program_id(0)
    j = pl.program_id(1)
    is_ctx = _row_is_ctx(i, tiles_per_batch, tm, ctx_len)

    @pl.when(j == 0)
    def _():
        y = _rms(x_ref[...], g2_ref[...])
        h = y * (1.0 + _mod_row(mod_ref, is_ctx, 4)) + _mod_row(mod_ref, is_ctx, 3)
        h_scr[...] = h.astype(BF16)

    h = h_scr[...]
    act = (_silu(_dot(h, wg_ref[...])) * _dot(h, wu_ref[...])).astype(BF16)
    part = _dot(act, wo_ref[...])

    @pl.when(j == 0)
    def _():
        acc_scr[...] = part

    @pl.when(j > 0)
    def _():
        acc_scr[...] += part

    @pl.when(j == nf - 1)
    def _():
        o_ref[...] = x_ref[...] + _mod_row(mod_ref, is_ctx, 5) * _rms(acc_scr[...], g3_ref[...])


def _ffn(xs, modl, g2, g3, w_in, w_out, *, seq_t, ctx_len):
    m, d = xs.shape
    dff = w_out.shape[0]
    tm = _pick(seq_t, (576, 384, 128))
    tf = _pick(dff, (512, 256, 128))
    nf = dff // tf
    tpb = seq_t // tm
    kern = functools.partial(_ffn_kernel, tm=tm, tiles_per_batch=tpb, ctx_len=ctx_len, nf=nf)
    return pl.pallas_call(
        kern,
        grid=(m // tm, nf),
        in_specs=[pl.BlockSpec((tm, d), lambda i, j: (i, 0)),
                  pl.BlockSpec((1, 16, d), lambda i, j: (i // tpb, 0, 0)),
                  pl.BlockSpec((1, d), lambda i, j: (0, 0)),
                  pl.BlockSpec((1, d), lambda i, j: (0, 0)),
                  pl.BlockSpec((d, tf), lambda i, j: (0, j)),
                  pl.BlockSpec((d, tf), lambda i, j: (0, nf + j)),
                  pl.BlockSpec((tf, d), lambda i, j: (j, 0))],
        out_specs=pl.BlockSpec((tm, d), lambda i, j: (i, 0)),
        out_shape=jax.ShapeDtypeStruct((m, d), F32),
        scratch_shapes=[pltpu.VMEM((tm, d), BF16), pltpu.VMEM((tm, d), F32)],
        compiler_params=_cparams(2),
        name="ffn",
    )(xs, modl, g2, g3, w_in, w_in, w_out)


def _cmlp_kernel(u_ref, v_ref, lng_ref, lnb_ref, ws_ref, bsb_ref, gc_ref, o_ref, *, tc, cw):
    u = _gelu_tanh(u_ref[0])
    v = _gelu_tanh(v_ref[0])
    vc = v - jnp.mean(v, axis=-1, keepdims=True)
    vn = vc * lax.rsqrt(jnp.mean(vc * vc, axis=-1, keepdims=True) + EPS) * lng_ref[...] + lnb_ref[...]
    vb = vn.astype(BF16)
    rows = []
    for ch in range(tc // CM_CHUNK):
        cols = []
        for g in range(CM_GROUPS):
            blk = vb[ch * CM_CHUNK:(ch + 1) * CM_CHUNK, g * cw:(g + 1) * cw]
            cols.append(_dot(ws_ref[g], blk) + bsb_ref[:, g * cw:(g + 1) * cw])
        rows.append(jnp.concatenate(cols, axis=1))
    s = jnp.concatenate(rows, axis=0)
    o_ref[0] = _rms(u * s, gc_ref[...]).astype(BF16)


def _cmlp(p, lng, lnb, ws, bsb, gc, *, u_blk, v_blk):
    b, t, _ = p.shape
    gw = lng.shape[1]
    tc = _pick(t, (256, 128))
    kern = functools.partial(_cmlp_kernel, tc=tc, cw=gw // CM_GROUPS)
    vec = pl.BlockSpec((1, gw), lambda bi, i: (0, 0))
    return pl.pallas_call(
        kern,
        grid=(b, t // tc),
        in_specs=[pl.BlockSpec((1, tc, gw), lambda bi, i: (bi, i, u_blk)),
                  pl.BlockSpec((1, tc, gw), lambda bi, i: (bi, i, v_blk)),
                  vec, vec,
                  pl.BlockSpec(ws.shape, lambda bi, i: (0, 0, 0)),
                  pl.BlockSpec(bsb.shape, lambda bi, i: (0, 0)),
                  vec],
        out_specs=pl.BlockSpec((1, tc, gw), lambda bi, i: (bi, i, 0)),
        out_shape=jax.ShapeDtypeStruct((b, t, gw), BF16),
        compiler_params=_cparams(2),
        name="chunk_mlp",
    )(p, p, lng, lnb, ws, bsb, gc)


def _group_mean_sq(x, ones_bd):
    sq = x * x
    hi = sq.astype(BF16)
    lo = (sq - hi.astype(F32)).astype(BF16)
    return (_dot(hi, ones_bd) + _dot(lo, ones_bd)) * (1.0 / HEAD_DIM)


def _rope(x, cos, sin_signed):
    w = x.shape[1]
    lane = lax.broadcasted_iota(jnp.int32, x.shape, 1)
    first = (lane % (HEAD_DIM // 2)) < (HEAD_DIM // 4)
    partner = jnp.where(first, pltpu.roll(x, w - HEAD_DIM // 4, 1), pltpu.roll(x, HEAD_DIM // 4, 1))
    return x * cos + partner * sin_signed


def _attn_kernel(q_ref, kv_ref, cq_ref, sq_ref, ck_ref, sk_ref, qg_ref, kg_ref, onesq_ref, onesk_ref,
                 sink_ref, mg_ref, o_ref, k_scr, v_scr, *, mode, tq, seq_t, ctx_len, n_pairs):
    i = pl.program_id(1)
    n_ctx_tiles = ctx_len // tq

    @pl.when(i == 0)
    def _():
        kv = kv_ref[0]
        k = kv[:, :LANES]
        if mode == "global":
            k = k * lax.rsqrt(_group_mean_sq(k, onesk_ref[...]) + EPS) * kg_ref[...]
        k_scr[...] = _rope(k, ck_ref[...], sk_ref[...]).astype(BF16)
        v_scr[...] = kv[:, LANES:].astype(BF16)

    q = q_ref[0]
    if mode == "global":
        q = q * lax.rsqrt(_group_mean_sq(q, onesq_ref[...]) + EPS) * qg_ref[...]
    q = _rope(q, cq_ref[...], sq_ref[...]) * ATTN_SCALE

    lane = lax.broadcasted_iota(jnp.int32, (tq, LANES), 1)
    lo = lane < HEAD_DIM
    lane1 = lax.broadcasted_iota(jnp.int32, (1, LANES), 1)
    row2 = lax.broadcasted_iota(jnp.int32, (2 * tq, 1), 0)
    latent_tile = i >= n_ctx_tiles

    outs = []
    for j in range(n_pairs):
        qp = q[:, j * LANES:(j + 1) * LANES]
        q2 = jnp.concatenate([jnp.where(lo, qp, 0.0), jnp.where(lo, 0.0, qp)], axis=0).astype(BF16)
        if mode == "window":
            wk = tq + 2 * WINDOW
            start = pl.multiple_of(jnp.clip(i * tq - WINDOW, ctx_len, seq_t - wk), LANES)
            s_w = _dot_nt(q2, k_scr[pl.ds(start, wk), :])
            r = lax.broadcasted_iota(jnp.int32, (2 * tq, wk), 0)
            r = jnp.where(r >= tq, r - tq, r)
            c = lax.broadcasted_iota(jnp.int32, (2 * tq, wk), 1)
            qpos = i * tq + r - ctx_len
            kpos = start + c - ctx_len
            valid = (jnp.abs(qpos - kpos) <= WINDOW) & (kpos >= 0) & latent_tile
            s_w = jnp.where(valid, s_w, NEG_BIG)
            s_c = _dot_nt(q2, k_scr[0:ctx_len, :])
            sv = sink_ref[:, j * LANES:(j + 1) * LANES]
            s_lo = jnp.max(jnp.where(lane1 < HEAD_DIM, sv, NEG_BIG), axis=1, keepdims=True)
            s_hi = jnp.max(jnp.where(lane1 < HEAD_DIM, NEG_BIG, sv), axis=1, keepdims=True)
            sk = jnp.where(row2 < tq, s_lo, s_hi)
            mx = jnp.maximum(jnp.maximum(jnp.max(s_w, axis=1, keepdims=True),
                                         jnp.max(s_c, axis=1, keepdims=True)), sk)
            p_w = jnp.exp(s_w - mx)
            p_c = jnp.exp(s_c - mx)
            den = (jnp.sum(p_w, axis=1, keepdims=True) + jnp.sum(p_c, axis=1, keepdims=True)
                   + jnp.exp(sk - mx))
            o2 = (_dot(p_w.astype(BF16), v_scr[pl.ds(start, wk), :])
                  + _dot(p_c.astype(BF16), v_scr[0:ctx_len, :]))
        else:
            s = _dot_nt(q2, k_scr[...])
            c = lax.broadcasted_iota(jnp.int32, (2 * tq, seq_t), 1)
            s = jnp.where((c < ctx_len) | latent_tile, s, NEG_BIG)
            mx = jnp.max(s, axis=1, keepdims=True)
            p = jnp.exp(s - mx)
            den = jnp.sum(p, axis=1, keepdims=True)
            o2 = _dot(p.astype(BF16), v_scr[...])
        o2 = o2 / den
        outs.append(jnp.where(lo, o2[:tq], o2[tq:]))
    o = jnp.concatenate(outs, axis=1)
    o_ref[0] = _rms(o, mg_ref[...]).astype(BF16)


def _attention(p, tabs, qg, kg, sink, mg, *, mode, q_blk, kv_blk, ctx_len):
    b, t, _ = p.shape
    gw = mg.shape[1]
    tq = _pick(t, (256, 128))
    assert ctx_len % tq == 0
    cq, sq, ck, sk, ones_q, ones_k = tabs
    kern = functools.partial(_attn_kernel, mode=mode, tq=tq, seq_t=t, ctx_len=ctx_len,
                             n_pairs=gw // LANES)
    const = lambda shape: pl.BlockSpec(shape, lambda bi, i: (0,) * len(shape))
    return pl.pallas_call(
        kern,
        grid=(b, t // tq),
        in_specs=[pl.BlockSpec((1, tq, gw), lambda bi, i: (bi, i, q_blk)),
                  pl.BlockSpec((1, t, 2 * LANES), lambda bi, i: (bi, 0, kv_blk)),
                  pl.BlockSpec((tq, gw), lambda bi, i: (i, 0)),
                  pl.BlockSpec((tq, gw), lambda bi, i: (i, 0)),
                  const((t, LANES)), const((t, LANES)),
                  const((1, gw)), const((1, LANES)),
                  const((gw, gw)), const((LANES, LANES)),
                  const((1, gw)), const((1, gw))],
        out_specs=pl.BlockSpec((1, tq, gw), lambda bi, i: (bi, i, 0)),
        out_shape=jax.ShapeDtypeStruct((b, t, gw), BF16),
        scratch_shapes=[pltpu.VMEM((t, LANES), BF16), pltpu.VMEM((t, LANES), BF16)],
        compiler_params=_cparams(2),
        name="attn_" + mode,
    )(p, p, cq, sq, ck, sk, qg, kg, ones_q, ones_k, sink, mg)


GLA_LEVELS = int(math.log2(GLA_CHUNK))
GLA_ROW_PRE = GLA_LEVELS * GLA_CHUNK
GLA_ROW_SUF = GLA_ROW_PRE + GLA_CHUNK
GLA_ROW_LAST = GLA_ROW_SUF + GLA_CHUNK
GLA_ROWS = GLA_ROW_LAST + 16


def _gla_constants(dk):
    n = GLA_CHUNK
    csum = np.zeros((2, GLA_ROWS, n), np.float32)
    qmask = np.zeros((2, GLA_LEVELS + 1, n, 1), np.float32)
    kmask = np.zeros((2, GLA_LEVELS + 1, n, 1), np.float32)
    smask = np.zeros((2, GLA_LEVELS + 1, n, n), np.float32)
    for d in range(2):
        tau = np.arange(n) if d == 0 else n - 1 - np.arange(n)
        qmask[d, 0] = 1.0
        kmask[d, 0] = 1.0
        smask[d, 0] = np.eye(n)
        for lv in range(1, GLA_LEVELS + 1):
            h = 1 << (lv - 1)
            blk = tau // (2 * h)
            upper = (tau % (2 * h)) >= h
            same = blk[:, None] == blk[None, :]
            both_up = upper[:, None] & upper[None, :]
            both_lo = (~upper[:, None]) & (~upper[None, :])
            c = np.where(upper[:, None],
                         same & both_up & (tau[None, :] <= tau[:, None]),
                         same & both_lo & (tau[None, :] > tau[:, None]))
            csum[d, (lv - 1) * n:lv * n] = c
            qmask[d, lv, :, 0] = upper
            kmask[d, lv, :, 0] = ~upper
            smask[d, lv] = same & upper[:, None] & (~upper[None, :])
        csum[d, GLA_ROW_PRE:GLA_ROW_SUF] = tau[None, :] <= tau[:, None]
        csum[d, GLA_ROW_SUF:GLA_ROW_LAST] = tau[None, :] > tau[:, None]
        csum[d, GLA_ROW_LAST:] = 1.0
    hk = GLA_HEADS * dk
    qmask = np.broadcast_to(qmask, (2, GLA_LEVELS + 1, n, hk)).copy()
    kmask = np.broadcast_to(kmask, (2, GLA_LEVELS + 1, n, hk)).copy()
    smask = np.tile(smask, (1, 1, 1, GLA_HEADS))
    return csum, qmask, kmask, smask


def _gla_kernel(pa_ref, lr_ref, gw_ref, gb_ref, gain_ref, cs_ref, qm_ref, km_ref, sm_ref, bdk_ref,
                bdv_ref, o_ref, la_scr, oacc_scr, st_scr, *, seq_t, ctx_len, dk, dv):
    hk = GLA_HEADS * dk
    hv = GLA_HEADS * dv
    n = GLA_CHUNK
    nc = seq_t // n
    nc_ctx = ctx_len // n

    z = _dot(lr_ref[0].astype(BF16), gw_ref[...]) + gb_ref[...]
    la_scr[...] = (jnp.minimum(z, 0.0) - jnp.log1p(jnp.exp(-jnp.abs(z)))) * (1.0 / GLA_TAU)
    oacc_scr[...] = jnp.zeros_like(oacc_scr)
    st_scr[...] = jnp.zeros_like(st_scr)

    bdk = bdk_ref[...]
    bdv = bdv_ref[...]

    def chunk_step(s, carry):
        for d in range(2):
            if d == 0:
                c = s
            else:
                c = jnp.where(s < nc_ctx, nc_ctx - 1 - s, nc - 1 - (s - nc_ctx))
            r0 = pl.multiple_of(c * n, n)
            q = pa_ref[0, pl.ds(r0, n), 0:hk] * (dk ** -0.5)
            k = pa_ref[0, pl.ds(r0, n), hk:2 * hk]
            v = pa_ref[0, pl.ds(r0, n), 2 * hk:2 * hk + hv]
            a = la_scr[pl.ds(r0, n), d * hk:(d + 1) * hk]
            a_hi = a.astype(BF16)
            a_lo = (a - a_hi.astype(F32)).astype(BF16)
            ex2 = _dot(cs_ref[d], jnp.concatenate([a_hi, a_lo], axis=1))
            e = jnp.exp(ex2[:, :hk] + ex2[:, hk:])

            scores = jnp.zeros((n, GLA_HEADS * n), F32)
            for lv in range(GLA_LEVELS + 1):
                if lv == 0:
                    ql, kl = q, k
                else:
                    el = e[(lv - 1) * n:lv * n]
                    ql = q * el * qm_ref[d, lv]
                    kl = k * el * km_ref[d, lv]
                kbd = (jnp.concatenate([kl] * GLA_HEADS, axis=0) * bdk).astype(BF16)
                scores = scores + _dot_nt(ql.astype(BF16), kbd) * sm_ref[d, lv]

            qh = (q * e[GLA_ROW_PRE:GLA_ROW_SUF]).astype(BF16)
            kh = k * e[GLA_ROW_SUF:GLA_ROW_LAST]
            e_last = e[GLA_ROW_LAST:GLA_ROW_LAST + 1]
            vbd = (jnp.concatenate([v] * GLA_HEADS, axis=0) * bdv).astype(BF16)
            st = st_scr[d]
            o = _dot(scores.astype(BF16), vbd) + _dot(qh, st.astype(BF16))
            oacc_scr[pl.ds(r0, n), :] += o

            kstack = jnp.concatenate([kh, jnp.broadcast_to(e_last, (n, hk))], axis=0)
            kt = kstack.T
            decay_col = kt[:, n:n + 1]
            vpad = jnp.concatenate([v, jnp.zeros_like(v)], axis=0).astype(BF16)
            st_scr[d] = st * decay_col + _dot(kt.astype(BF16), vpad) * bdv
        return carry

    lax.fori_loop(0, nc, chunk_step, 0)

    tfin = _pick(seq_t, (256, 128))
    for t0 in range(0, seq_t, tfin):
        o = oacc_scr[t0:t0 + tfin, :]
        parts = [_rms(o[:, h * dv:(h + 1) * dv], gain_ref[:, h * dv:(h + 1) * dv])
                 for h in range(GLA_HEADS)]
        gate = pa_ref[0, t0:t0 + tfin, 2 * hk + hv:2 * hk + 2 * hv]
        o_ref[0, t0:t0 + tfin, :] = (jnp.concatenate(parts, axis=1) * _silu(gate)).astype(BF16)


def _gla(p, lr, gw, gb, gain, consts, *, ctx_len, dk, dv):
    b, t, _ = p.shape
    hk = GLA_HEADS * dk
    hv = GLA_HEADS * dv
    pa_w = 2 * hk + 2 * hv
    cs, qm, km, sm, bdk, bdv = consts
    kern = functools.partial(_gla_kernel, seq_t=t, ctx_len=ctx_len, dk=dk, dv=dv)
    const = lambda a: pl.BlockSpec(a.shape, lambda bi: (0,) * a.ndim)
    return pl.pallas_call(
        kern,
        grid=(b,),
        in_specs=[pl.BlockSpec((1, t, pa_w), lambda bi: (bi, 0, 0)),
                  pl.BlockSpec((1, t, LANES), lambda bi: (bi, 0, 0)),
                  const(gw), const(gb), const(gain), const(cs), const(qm), const(km), const(sm),
                  const(bdk), const(bdv)],
        out_specs=pl.BlockSpec((1, t, hv), lambda bi: (bi, 0, 0)),
        out_shape=jax.ShapeDtypeStruct((b, t, hv), BF16),
        scratch_shapes=[pltpu.VMEM((t, 2 * hk), F32), pltpu.VMEM((t, hv), F32),
                        pltpu.VMEM((2, hk, hv), F32)],
        compiler_params=_cparams(1),
        name="gla",
    )(p, lr, gw, gb, gain, cs, qm, km, sm, bdk, bdv)


def _rope_tables(length, ctx_len, n_heads):
    rows = length // GRID_W
    row = jnp.repeat(jnp.arange(rows), GRID_W).astype(F32)
    col = jnp.tile(jnp.arange(GRID_W), rows).astype(F32)
    quarter = HEAD_DIM // 4
    inv = ROPE_THETA ** (-jnp.arange(quarter, dtype=F32) / quarter)
    ar, ac = row[:, None] * inv, col[:, None] * inv
    ang = jnp.concatenate([ar, ar, ac, ac], axis=-1)
    cos, sin = jnp.cos(ang), jnp.sin(ang)
    first = (jnp.arange(HEAD_DIM) % (HEAD_DIM // 2)) < quarter
    sin_signed = jnp.where(first[None, :], -sin, sin)
    cos = jnp.concatenate([jnp.ones((ctx_len, HEAD_DIM), F32), cos], axis=0)
    sin_signed = jnp.concatenate([jnp.zeros((ctx_len, HEAD_DIM), F32), sin_signed], axis=0)
    return jnp.tile(cos, (1, n_heads)), jnp.tile(sin_signed, (1, n_heads))


def _pair_order(n_heads, n_kv):
    g = n_heads // n_kv
    return np.array([kv * g + j for j in range(g) for kv in range(n_kv)])


def _head_cols(order, width):
    return (order[:, None] * width + np.arange(width)[None, :]).reshape(-1)


def _block_diag_mask(rows, cols, rb, cb):
    r = np.arange(rows)[:, None] // rb
    c = np.arange(cols)[None, :] // cb
    return (r == c).astype(np.float32)


def kernel(x, c, ctx, c_ctx, w_mod, b_mod, norm_g, w_in, gla_gate_up, gla_gate_b, win_sink, cm_ln_g,
           cm_ln_b, cm_ws, cm_bs, qk_g, mix_g, w_out, w_ffn_in, w_ffn_out):
    bsz, seq, d = x.shape
    ctx_len = ctx.shape[1]
    seq_t = ctx_len + seq
    depth = w_mod.shape[0]
    gw = d // 4
    rank = gla_gate_up.shape[2]
    dk = gla_gate_up.shape[3] // GLA_HEADS
    dv = gw // GLA_HEADS
    n_heads = gw // HEAD_DIM
    kvw = WIN_KV * HEAD_DIM
    assert kvw == LANES and GA_KV == WIN_KV and gw % LANES == 0 and 2 * rank <= LANES
    assert ctx_len % CM_CHUNK == 0 and seq % CM_CHUNK == 0 and seq % GRID_W == 0

    sizes = [GLA_HEADS * dk, GLA_HEADS * dk, gw, gw, rank, rank, gw, kvw, kvw, gw, gw, gw, kvw, kvw]
    off = np.concatenate([[0], np.cumsum(sizes)])
    a_q, a_k, a_v, a_g, a_lrf, a_lrb, b_q, b_k, b_v, c_u, c_v, d_q, d_k, d_v = [
        np.arange(off[i], off[i + 1]) for i in range(14)]
    order = _pair_order(n_heads, WIN_KV)
    hcols = _head_cols(order, HEAD_DIM)
    main_cols = np.concatenate([a_q, a_k, a_v, a_g, c_u, c_v, b_q[hcols], d_q[hcols], b_k, b_v, d_k, d_v])
    lr_cols = np.concatenate([a_lrf, a_lrb])
    assert 2 * GLA_HEADS * dk + 2 * gw == 3 * gw
    blk_cu, blk_cv, blk_bq, blk_dq = 3, 4, 5, 6
    blk_bkv = (7 * gw) // (2 * LANES)
    blk_dkv = blk_bkv + 1

    cos_q, sin_q = _rope_tables(seq, ctx_len, n_heads)
    cos_k, sin_k = cos_q[:, :LANES], sin_q[:, :LANES]
    ones_q = jnp.asarray(_block_diag_mask(gw, gw, HEAD_DIM, HEAD_DIM), BF16)
    ones_k = ones_q[:LANES, :LANES]
    tabs = (cos_q, sin_q, cos_k, sin_k, ones_q, ones_k)

    csum, qmask, kmask, smask = _gla_constants(dk)
    gla_consts = (jnp.asarray(csum, BF16), jnp.asarray(qmask), jnp.asarray(kmask), jnp.asarray(smask),
                  jnp.asarray(_block_diag_mask(GLA_HEADS * GLA_CHUNK, GLA_HEADS * dk, GLA_CHUNK, dk)),
                  jnp.asarray(_block_diag_mask(GLA_HEADS * dk, gw, dk, dv)))

    rows = 16
    cond = jnp.concatenate([c, c_ctx[None, :], jnp.zeros((rows - bsz - 1, d), F32)], axis=0)
    mod_all = _modulation(cond, w_mod, b_mod)
    mod_lat = mod_all[:, :bsz].reshape(depth, bsz, 6, d)
    mod_ctx = jnp.broadcast_to(mod_all[:, bsz].reshape(depth, 1, 6, d), (depth, bsz, 6, d))
    pad2 = jnp.zeros((depth, bsz, 2, d), F32)
    mods = jnp.concatenate([mod_ctx, pad2, mod_lat, pad2], axis=2)

    xs = jnp.concatenate([ctx, x], axis=1).reshape(bsz * seq_t, d)
    zero_sink = jnp.zeros((1, gw), F32)
    one_gain_q = jnp.ones((1, gw), F32)
    one_gain_k = jnp.ones((1, LANES), F32)

    for l in range(depth):
        w_main = w_in[l][:, main_cols].astype(BF16)
        w_lr = jnp.pad(w_in[l][:, lr_cols], ((0, 0), (0, LANES - 2 * rank))).astype(BF16)
        gate_w = jnp.zeros((LANES, 2 * GLA_HEADS * dk), F32)
        gate_w = gate_w.at[:rank, :GLA_HEADS * dk].set(gla_gate_up[l, 0])
        gate_w = gate_w.at[rank:2 * rank, GLA_HEADS * dk:].set(gla_gate_up[l, 1]).astype(BF16)
        gate_b = gla_gate_b[l].reshape(1, -1)
        mg = mix_g[l]
        ga, gb_, gc, gd = (mg[0:gw], mg[gw:2 * gw][hcols], mg[2 * gw:3 * gw], mg[3 * gw:][hcols])
        wo = w_out[l]
        wo = jnp.concatenate([wo[0:gw], wo[gw:2 * gw][hcols], wo[2 * gw:3 * gw], wo[3 * gw:][hcols]],
                             axis=0).astype(BF16)
        sink = jnp.repeat(win_sink[l][order], HEAD_DIM)[None, :]
        qg = jnp.tile(qk_g[l, 0], n_heads)[None, :]
        kg = jnp.tile(qk_g[l, 1], WIN_KV)[None, :]
        bsb = jnp.repeat(cm_bs[l].T, gw // CM_GROUPS, axis=1)
        modl = mods[l]

        p, lr = _inproj(xs, modl, norm_g[l, 0][None, :], w_main, w_lr, seq_t=seq_t, ctx_len=ctx_len)
        p3 = p.reshape(bsz, seq_t, -1)
        lr3 = lr.reshape(bsz, seq_t, LANES)
        y_a = _gla(p3, lr3, gate_w, gate_b, ga[None, :], gla_consts, ctx_len=ctx_len, dk=dk, dv=dv)
        y_b = _attention(p3, tabs, one_gain_q, one_gain_k, sink, gb_[None, :], mode="window",
                         q_blk=blk_bq, kv_blk=blk_bkv, ctx_len=ctx_len)
        y_c = _cmlp(p3, cm_ln_g[l][None, :], cm_ln_b[l][None, :], cm_ws[l].astype(BF16), bsb,
                    gc[None, :], u_blk=blk_cu, v_blk=blk_cv)
        y_d = _attention(p3, tabs, qg, kg, zero_sink, gd[None, :], mode="global",
                         q_blk=blk_dq, kv_blk=blk_dkv, ctx_len=ctx_len)
        ys = [y.reshape(bsz * seq_t, gw) for y in (y_a, y_b, y_c, y_d)]
        xs = _outproj(xs, ys, modl, norm_g[l, 1][None, :], wo, seq_t=seq_t, ctx_len=ctx_len)
        xs = _ffn(xs, modl, norm_g[l, 2][None, :], norm_g[l, 3][None, :], w_ffn_in[l].astype(BF16),
                  w_ffn_out[l].astype(BF16), seq_t=seq_t, ctx_len=ctx_len)

    return xs.reshape(bsz, seq_t, d)[:, ctx_len:]
```

```python
import functools
import math

import numpy as np
import jax
import jax.numpy as jnp
from jax import lax
from jax.experimental import pallas as pl
from jax.experimental.pallas import tpu as pltpu

F32 = jnp.float32
BF16 = jnp.bfloat16

EPS = 1e-6
HEAD_DIM = 64
ATTN_SCALE = HEAD_DIM ** -0.5
ROPE_THETA = 10000.0
GRID_W = 64
GLA_HEADS = 4
GLA_TAU = 16.0
GLA_CHUNK = 64
WIN_KV = 2
WINDOW = 128
CM_GROUPS = 4
CM_CHUNK = 128
GA_KV = 2

LANES = 128
NEG_BIG = -1e30
V7X_VMEM_LIMIT = 56 * 1024 * 1024


def _cparams(n_axes, vmem=V7X_VMEM_LIMIT):
    return pltpu.CompilerParams(dimension_semantics=("arbitrary",) * n_axes,
                                vmem_limit_bytes=vmem)


def _pick(total, candidates):
    for cand in candidates:
        if total % cand == 0:
            return cand
    raise ValueError(f"no tile in {candidates} divides {total}")


def _dot(a, b):
    return jnp.dot(a, b, preferred_element_type=F32)


def _dot_nt(a, b):
    return lax.dot_general(a, b, (((1,), (1,)), ((), ())), preferred_element_type=F32)


def _rms(x, gain):
    return x * lax.rsqrt(jnp.mean(x * x, axis=-1, keepdims=True) + EPS) * gain


def _silu(x):
    return x * jax.nn.sigmoid(x)


def _gelu_tanh(x):
    return 0.5 * x * (1.0 + jnp.tanh(math.sqrt(2.0 / math.pi) * (x + 0.044715 * (x * x * x))))


ROW_CHUNK = 64


def _mod_vec(mod_ref, k, chunk_is_ctx):
    return jnp.where(chunk_is_ctx, mod_ref[0, k:k + 1, :], mod_ref[0, 8 + k:9 + k, :])


def _norm_modulate_rows(x_ref, h_ref, g_ref, mod_ref, k_shift, k_scale, row0, ctx_len):
    def step(c, carry):
        r = pl.multiple_of(c * ROW_CHUNK, ROW_CHUNK)
        is_ctx = row0 + r < ctx_len
        gain = g_ref[...] * (1.0 + _mod_vec(mod_ref, k_scale, is_ctx))
        x = x_ref[pl.ds(r, ROW_CHUNK), :]
        inv = lax.rsqrt(jnp.mean(x * x, axis=-1, keepdims=True) + EPS)
        h_ref[pl.ds(r, ROW_CHUNK), :] = (x * inv * gain + _mod_vec(mod_ref, k_shift, is_ctx)).astype(BF16)
        return carry

    n_chunks = x_ref.shape[0] // ROW_CHUNK
    lax.fori_loop(0, n_chunks, step, 0, unroll=3 if n_chunks % 3 == 0 else 2)


def _gated_residual_rows(x_ref, z_ref, o_ref, g_ref, mod_ref, k_gate, row0, ctx_len):
    def step(c, carry):
        r = pl.multiple_of(c * ROW_CHUNK, ROW_CHUNK)
        gain = g_ref[...] * _mod_vec(mod_ref, k_gate, row0 + r < ctx_len)
        z = z_ref[pl.ds(r, ROW_CHUNK), :]
        inv = lax.rsqrt(jnp.mean(z * z, axis=-1, keepdims=True) + EPS)
        o_ref[pl.ds(r, ROW_CHUNK), :] = x_ref[pl.ds(r, ROW_CHUNK), :] + z * inv * gain
        return carry

    n_chunks = x_ref.shape[0] // ROW_CHUNK
    lax.fori_loop(0, n_chunks, step, 0, unroll=3 if n_chunks % 3 == 0 else 2)


def _mod_kernel(c_ref, w_ref, b_ref, o_ref):
    s = _silu(c_ref[...]).astype(BF16)
    o_ref[0] = _dot(s, w_ref[0].astype(BF16)) + b_ref[0]


def _modulation(cond, w_mod, b_mod):
    depth, d, n = w_mod.shape
    rows = cond.shape[0]
    tn = _pick(n, (1024, 512, 256, 128))
    return pl.pallas_call(
        _mod_kernel,
        grid=(depth, n // tn),
        in_specs=[pl.BlockSpec((rows, d), lambda l, j: (0, 0)),
                  pl.BlockSpec((1, d, tn), lambda l, j: (l, 0, j)),
                  pl.BlockSpec((1, 1, tn), lambda l, j: (l, 0, j))],
        out_specs=pl.BlockSpec((1, rows, tn), lambda l, j: (l, 0, j)),
        out_shape=jax.ShapeDtypeStruct((depth, rows, n), F32),
        compiler_params=_cparams(2),
        name="modulation",
    )(cond, w_mod, b_mod.reshape(depth, 1, n))


def _inproj_kernel(x_ref, mod_ref, g_ref, w_ref, wlr_ref, p_ref, lr_ref, h_scr, *,
                   tm, tiles_per_batch, ctx_len):
    i = pl.program_id(0)
    j = pl.program_id(1)

    @pl.when(j == 0)
    def _():
        row0 = (i % tiles_per_batch) * tm
        _norm_modulate_rows(x_ref, h_scr, g_ref, mod_ref, 0, 1, row0, ctx_len)
        lr_ref[...] = _dot(h_scr[...], wlr_ref[...])

    p_ref[...] = _dot(h_scr[...], w_ref[...])


def _inproj(xs, modl, gain, w_main, w_lr, *, seq_t, ctx_len):
    m, d = xs.shape
    n = w_main.shape[1]
    tm = _pick(seq_t, (768, 384, 128))
    tn = _pick(n, (1024, 512, 256, 128))
    tpb = seq_t // tm
    kern = functools.partial(_inproj_kernel, tm=tm, tiles_per_batch=tpb, ctx_len=ctx_len)
    return pl.pallas_call(
        kern,
        grid=(m // tm, n // tn),
        in_specs=[pl.BlockSpec((tm, d), lambda i, j: (i, 0)),
                  pl.BlockSpec((1, 16, d), lambda i, j: (i // tpb, 0, 0)),
                  pl.BlockSpec((1, d), lambda i, j: (0, 0)),
                  pl.BlockSpec((d, tn), lambda i, j: (0, j)),
                  pl.BlockSpec((d, LANES), lambda i, j: (0, 0))],
        out_specs=[pl.BlockSpec((tm, tn), lambda i, j: (i, j)),
                   pl.BlockSpec((tm, LANES), lambda i, j: (i, 0))],
        out_shape=[jax.ShapeDtypeStruct((m, n), F32),
                   jax.ShapeDtypeStruct((m, LANES), F32)],
        scratch_shapes=[pltpu.VMEM((tm, d), BF16)],
        compiler_params=_cparams(2),
        name="inproj",
    )(xs, modl, gain, w_main, w_lr)


def _outproj_kernel(x_ref, ya_ref, yb_ref, yc_ref, yd_ref, mod_ref, g_ref, w_ref, o_ref, y_scr, *,
                    tm, tiles_per_batch, ctx_len, gw):
    i = pl.program_id(0)
    y_scr[:, 0 * gw:1 * gw] = ya_ref[...]
    y_scr[:, 1 * gw:2 * gw] = yb_ref[...]
    y_scr[:, 2 * gw:3 * gw] = yc_ref[...]
    y_scr[:, 3 * gw:4 * gw] = yd_ref[...]
    zn = _rms(_dot(y_scr[...], w_ref[...]), g_ref[...])
    row0 = (i % tiles_per_batch) * tm

    @pl.when(row0 >= ctx_len)
    def _():
        o_ref[...] = x_ref[...] + mod_ref[0, 10:11, :] * zn

    @pl.when(row0 < ctx_len)
    def _():
        is_ctx = row0 + lax.broadcasted_iota(jnp.int32, (tm, 1), 0) < ctx_len
        o_ref[...] = x_ref[...] + jnp.where(is_ctx, mod_ref[0, 2:3, :], mod_ref[0, 10:11, :]) * zn


def _outproj(xs, ys, modl, gain, w, *, seq_t, ctx_len):
    m, d = xs.shape
    gw = ys[0].shape[1]
    tm = _pick(seq_t, (384, 128))
    tpb = seq_t // tm
    kern = functools.partial(_outproj_kernel, tm=tm, tiles_per_batch=tpb, ctx_len=ctx_len, gw=gw)
    yspec = pl.BlockSpec((tm, gw), lambda i: (i, 0))
    return pl.pallas_call(
        kern,
        grid=(m // tm,),
        in_specs=[pl.BlockSpec((tm, d), lambda i: (i, 0)), yspec, yspec, yspec, yspec,
                  pl.BlockSpec((1, 16, d), lambda i: (i // tpb, 0, 0)),
                  pl.BlockSpec((1, d), lambda i: (0, 0)),
                  pl.BlockSpec((4 * gw, d), lambda i: (0, 0))],
        out_specs=pl.BlockSpec((tm, d), lambda i: (i, 0)),
        out_shape=jax.ShapeDtypeStruct((m, d), F32),
        scratch_shapes=[pltpu.VMEM((tm, 4 * gw), BF16)],
        compiler_params=_cparams(1),
        name="outproj",
    )(xs, *ys, modl, gain, w)


def _ffn_kernel(x_ref, mod_ref, g2_ref, g3_ref, wg_ref, wu_ref, wo_ref, o_ref, h_scr, acc_scr, *,
                tm, tiles_per_batch, ctx_len, nf):
    i = pl.program_id(0)
    j = pl.program_id(1)

    row0 = (i % tiles_per_batch) * tm

    @pl.when(j == 0)
    def _():
        _norm_modulate_rows(x_ref, h_scr, g2_ref, mod_ref, 3, 4, row0, ctx_len)
        acc_scr[...] = jnp.zeros_like(acc_scr)

    h = h_scr[...]
    act = (_silu(_dot(h, wg_ref[...])) * _dot(h, wu_ref[...])).astype(BF16)
    acc_scr[...] += _dot(act, wo_ref[...])

    @pl.when(j == nf - 1)
    def _():
        _gated_residual_rows(x_ref, acc_scr, o_ref, g3_ref, mod_ref, 5, row0, ctx_len)


def _ffn(xs, modl, g2, g3, w_in, w_out, *, seq_t, ctx_len):
    m, d = xs.shape
    dff = w_out.shape[0]
    tm = _pick(seq_t, (576, 384, 128))
    tf = _pick(dff, (512, 256, 128))
    nf = dff // tf
    tpb = seq_t // tm
    kern = functools.partial(_ffn_kernel, tm=tm, tiles_per_batch=tpb, ctx_len=ctx_len, nf=nf)
    return pl.pallas_call(
        kern,
        grid=(m // tm, nf),
        in_specs=[pl.BlockSpec((tm, d), lambda i, j: (i, 0)),
                  pl.BlockSpec((1, 16, d), lambda i, j: (i // tpb, 0, 0)),
                  pl.BlockSpec((1, d), lambda i, j: (0, 0)),
                  pl.BlockSpec((1, d), lambda i, j: (0, 0)),
                  pl.BlockSpec((d, tf), lambda i, j: (0, j)),
                  pl.BlockSpec((d, tf), lambda i, j: (0, nf + j)),
                  pl.BlockSpec((tf, d), lambda i, j: (j, 0))],
        out_specs=pl.BlockSpec((tm, d), lambda i, j: (i, 0)),
        out_shape=jax.ShapeDtypeStruct((m, d), F32),
        scratch_shapes=[pltpu.VMEM((tm, d), BF16), pltpu.VMEM((tm, d), F32)],
        compiler_params=_cparams(2),
        name="ffn",
    )(xs, modl, g2, g3, w_in, w_in, w_out)


def _cmlp_kernel(u_ref, v_ref, lng_ref, lnb_ref, ws_ref, bsb_ref, gc_ref, o_ref, *, tc, cw):
    u = _gelu_tanh(u_ref[0])
    v = _gelu_tanh(v_ref[0])
    vc = v - jnp.mean(v, axis=-1, keepdims=True)
    vn = vc * lax.rsqrt(jnp.mean(vc * vc, axis=-1, keepdims=True) + EPS) * lng_ref[...] + lnb_ref[...]
    vb = vn.astype(BF16)
    rows = []
    for ch in range(tc // CM_CHUNK):
        cols = []
        for g in range(CM_GROUPS):
            blk = vb[ch * CM_CHUNK:(ch + 1) * CM_CHUNK, g * cw:(g + 1) * cw]
            cols.append(_dot(ws_ref[g], blk) + bsb_ref[:, g * cw:(g + 1) * cw])
        rows.append(jnp.concatenate(cols, axis=1))
    s = jnp.concatenate(rows, axis=0)
    o_ref[0] = _rms(u * s, gc_ref[...]).astype(BF16)


def _cmlp(p, lng, lnb, ws, bsb, gc, *, u_blk, v_blk):
    b, t, _ = p.shape
    gw = lng.shape[1]
    tc = _pick(t, (256, 128))
    kern = functools.partial(_cmlp_kernel, tc=tc, cw=gw // CM_GROUPS)
    vec = pl.BlockSpec((1, gw), lambda bi, i: (0, 0))
    return pl.pallas_call(
        kern,
        grid=(b, t // tc),
        in_specs=[pl.BlockSpec((1, tc, gw), lambda bi, i: (bi, i, u_blk)),
                  pl.BlockSpec((1, tc, gw), lambda bi, i: (bi, i, v_blk)),
                  vec, vec,
                  pl.BlockSpec(ws.shape, lambda bi, i: (0, 0, 0)),
                  pl.BlockSpec(bsb.shape, lambda bi, i: (0, 0)),
                  vec],
        out_specs=pl.BlockSpec((1, tc, gw), lambda bi, i: (bi, i, 0)),
        out_shape=jax.ShapeDtypeStruct((b, t, gw), BF16),
        compiler_params=_cparams(2),
        name="chunk_mlp",
    )(p, p, lng, lnb, ws, bsb, gc)


def _group_mean_sq(x, ones_bd):
    sq = x * x
    hi = sq.astype(BF16)
    lo = (sq - hi.astype(F32)).astype(BF16)
    return (_dot(hi, ones_bd) + _dot(lo, ones_bd)) * (1.0 / HEAD_DIM)


def _rope(x, cos, sin_signed):
    w = x.shape[1]
    lane = lax.broadcasted_iota(jnp.int32, x.shape, 1)
    first = (lane % (HEAD_DIM // 2)) < (HEAD_DIM // 4)
    partner = jnp.where(first, pltpu.roll(x, w - HEAD_DIM // 4, 1), pltpu.roll(x, HEAD_DIM // 4, 1))
    return x * cos + partner * sin_signed


def _attn_kernel(q_ref, kv_ref, cq_ref, sq_ref, ck_ref, sk_ref, qg_ref, kg_ref, onesq_ref, onesk_ref,
                 sink_ref, mg_ref, o_ref, k_scr, v_scr, *, mode, tq, seq_t, ctx_len, n_pairs):
    i = pl.program_id(1)
    n_ctx_tiles = ctx_len // tq

    @pl.when(i == 0)
    def _():
        kv = kv_ref[0]
        k = kv[:, :LANES]
        if mode == "global":
            k = k * lax.rsqrt(_group_mean_sq(k, onesk_ref[...]) + EPS) * kg_ref[...]
        k_scr[...] = _rope(k, ck_ref[...], sk_ref[...]).astype(BF16)
        v_scr[...] = kv[:, LANES:].astype(BF16)

    q = q_ref[0]
    if mode == "global":
        q = q * lax.rsqrt(_group_mean_sq(q, onesq_ref[...]) + EPS) * qg_ref[...]
    q = _rope(q, cq_ref[...], sq_ref[...]) * ATTN_SCALE

    lane = lax.broadcasted_iota(jnp.int32, (tq, LANES), 1)
    lo = lane < HEAD_DIM
    lane1 = lax.broadcasted_iota(jnp.int32, (1, LANES), 1)
    row2 = lax.broadcasted_iota(jnp.int32, (2 * tq, 1), 0)

    def mix(latent_tile):
        key_sets = [(0, ctx_len, None)]
        if latent_tile and mode == "window":
            wk = tq + 2 * WINDOW
            start = pl.multiple_of(jnp.clip(i * tq - WINDOW, ctx_len, seq_t - wk), LANES)
            r = lax.broadcasted_iota(jnp.int32, (2 * tq, wk), 0)
            r = jnp.where(r >= tq, r - tq, r)
            c = lax.broadcasted_iota(jnp.int32, (2 * tq, wk), 1)
            dist = (i * tq + r) - (start + c)
            key_sets.append((start, wk, (jnp.abs(dist) <= WINDOW)))
        elif latent_tile:
            key_sets = [(0, seq_t, None)]

        outs = []
        for j in range(n_pairs):
            qp = q[:, j * LANES:(j + 1) * LANES]
            q2 = jnp.concatenate([jnp.where(lo, qp, 0.0), jnp.where(lo, 0.0, qp)],
                                 axis=0).astype(BF16)
            scores = []
            for k0, kn, valid in key_sets:
                s = _dot_nt(q2, k_scr[pl.ds(k0, kn), :])
                scores.append(s if valid is None else jnp.where(valid, s, NEG_BIG))
            mx = functools.reduce(jnp.maximum, [jnp.max(s, axis=1, keepdims=True) for s in scores])
            if mode == "window":
                sv = sink_ref[:, j * LANES:(j + 1) * LANES]
                s_lo = jnp.max(jnp.where(lane1 < HEAD_DIM, sv, NEG_BIG), axis=1, keepdims=True)
                s_hi = jnp.max(jnp.where(lane1 < HEAD_DIM, NEG_BIG, sv), axis=1, keepdims=True)
                sk = jnp.where(row2 < tq, s_lo, s_hi)
                mx = jnp.maximum(mx, sk)
                den = jnp.exp(sk - mx)
            else:
                den = 0.0
            o2 = 0.0
            for (k0, kn, _), s in zip(key_sets, scores):
                p = jnp.exp(s - mx)
                den = den + jnp.sum(p, axis=1, keepdims=True)
                o2 = o2 + _dot(p.astype(BF16), v_scr[pl.ds(k0, kn), :])
            o2 = o2 / den
            outs.append(jnp.where(lo, o2[:tq], o2[tq:]))
        o = jnp.concatenate(outs, axis=1)
        o_ref[0] = _rms(o, mg_ref[...]).astype(BF16)

    pl.when(i < n_ctx_tiles)(functools.partial(mix, False))
    pl.when(i >= n_ctx_tiles)(functools.partial(mix, True))


def _attention(p, tabs, qg, kg, sink, mg, *, mode, q_blk, kv_blk, ctx_len):
    b, t, _ = p.shape
    gw = mg.shape[1]
    tq = _pick(t, (256, 128))
    assert ctx_len % tq == 0
    cq, sq, ck, sk, ones_q, ones_k = tabs
    kern = functools.partial(_attn_kernel, mode=mode, tq=tq, seq_t=t, ctx_len=ctx_len,
                             n_pairs=gw // LANES)
    const = lambda shape: pl.BlockSpec(shape, lambda bi, i: (0,) * len(shape))
    return pl.pallas_call(
        kern,
        grid=(b, t // tq),
        in_specs=[pl.BlockSpec((1, tq, gw), lambda bi, i: (bi, i, q_blk)),
                  pl.BlockSpec((1, t, 2 * LANES), lambda bi, i: (bi, 0, kv_blk)),
                  pl.BlockSpec((tq, gw), lambda bi, i: (i, 0)),
                  pl.BlockSpec((tq, gw), lambda bi, i: (i, 0)),
                  const((t, LANES)), const((t, LANES)),
                  const((1, gw)), const((1, LANES)),
                  const((gw, gw)), const((LANES, LANES)),
                  const((1, gw)), const((1, gw))],
        out_specs=pl.BlockSpec((1, tq, gw), lambda bi, i: (bi, i, 0)),
        out_shape=jax.ShapeDtypeStruct((b, t, gw), BF16),
        scratch_shapes=[pltpu.VMEM((t, LANES), BF16), pltpu.VMEM((t, LANES), BF16)],
        compiler_params=_cparams(2),
        name="attn_" + mode,
    )(p, p, cq, sq, ck, sk, qg, kg, ones_q, ones_k, sink, mg)


GLA_LEVELS = int(math.log2(GLA_CHUNK))
GLA_ROW_PRE = GLA_LEVELS * GLA_CHUNK
GLA_ROW_SUF = GLA_ROW_PRE + GLA_CHUNK
GLA_ROW_LAST = GLA_ROW_SUF + GLA_CHUNK
GLA_ROWS = GLA_ROW_LAST + 16
GLA_STEPS_PER_ITER = 2


def _gla_constants(dk):
    n = GLA_CHUNK
    csum = np.zeros((2, GLA_ROWS, n), np.float32)
    qmask = np.zeros((2, GLA_LEVELS + 1, n, 1), np.float32)
    kmask = np.zeros((2, GLA_LEVELS + 1, n, 1), np.float32)
    smask = np.zeros((2, GLA_LEVELS + 1, n, n), np.float32)
    for d in range(2):
        tau = np.arange(n) if d == 0 else n - 1 - np.arange(n)
        qmask[d, 0] = 1.0
        kmask[d, 0] = 1.0
        smask[d, 0] = np.eye(n)
        for lv in range(1, GLA_LEVELS + 1):
            h = 1 << (lv - 1)
            blk = tau // (2 * h)
            upper = (tau % (2 * h)) >= h
            same = blk[:, None] == blk[None, :]
            both_up = upper[:, None] & upper[None, :]
            both_lo = (~upper[:, None]) & (~upper[None, :])
            c = np.where(upper[:, None],
                         same & both_up & (tau[None, :] <= tau[:, None]),
                         same & both_lo & (tau[None, :] > tau[:, None]))
            csum[d, (lv - 1) * n:lv * n] = c
            qmask[d, lv, :, 0] = upper
            kmask[d, lv, :, 0] = ~upper
            smask[d, lv] = same & upper[:, None] & (~upper[None, :])
        csum[d, GLA_ROW_PRE:GLA_ROW_SUF] = tau[None, :] <= tau[:, None]
        csum[d, GLA_ROW_SUF:GLA_ROW_LAST] = tau[None, :] > tau[:, None]
        csum[d, GLA_ROW_LAST:] = 1.0
    hk = GLA_HEADS * dk
    qmask = np.broadcast_to(qmask, (2, GLA_LEVELS + 1, n, hk)).copy()
    kmask = np.broadcast_to(kmask, (2, GLA_LEVELS + 1, n, hk)).copy()
    smask = np.tile(smask, (1, 1, 1, GLA_HEADS))
    return csum, qmask, kmask, smask


def _gla_kernel(pa_ref, lr_ref, gw_ref, gb_ref, gain_ref, cs_ref, qm_ref, km_ref, sm_ref, bdk_ref,
                bdv_ref, o_ref, la_scr, oacc_scr, st_scr, *, seq_t, ctx_len, dk, dv):
    hk = GLA_HEADS * dk
    hv = GLA_HEADS * dv
    n = GLA_CHUNK
    nc = seq_t // n
    nc_ctx = ctx_len // n

    z = _dot(lr_ref[0].astype(BF16), gw_ref[...]) + gb_ref[...]
    la_scr[...] = (jnp.minimum(z, 0.0) - jnp.log1p(jnp.exp(-jnp.abs(z)))) * (1.0 / GLA_TAU)
    oacc_scr[...] = jnp.zeros_like(oacc_scr)
    st_scr[...] = jnp.zeros_like(st_scr)

    bdk = bdk_ref[...]
    bdv = bdv_ref[...]

    steps = GLA_STEPS_PER_ITER
    assert nc % steps == 0
    chains = [(d, u) for u in range(steps) for d in (0, 1)]

    def scan_iter(it, carry):
        r0 = []
        for d, u in chains:
            s = it * steps + u
            c = s if d == 0 else jnp.where(s < nc_ctx, nc_ctx - 1 - s, nc - 1 - (s - nc_ctx))
            r0.append(pl.multiple_of(c * n, n))
        q = [pa_ref[0, pl.ds(r, n), 0:hk] * (dk ** -0.5) for r in r0]
        k = [pa_ref[0, pl.ds(r, n), hk:2 * hk] for r in r0]
        v = [pa_ref[0, pl.ds(r, n), 2 * hk:2 * hk + hv] for r in r0]
        ex2 = []
        for (d, _), r in zip(chains, r0):
            a = la_scr[pl.ds(r, n), d * hk:(d + 1) * hk]
            a_hi = a.astype(BF16)
            a_lo = (a - a_hi.astype(F32)).astype(BF16)
            ex2.append(_dot(cs_ref[d], jnp.concatenate([a_hi, a_lo], axis=1)))
        e = [jnp.exp(x[:, :hk] + x[:, hk:]) for x in ex2]

        scores = [jnp.zeros((n, GLA_HEADS * n), F32) for _ in chains]
        for lv in range(GLA_LEVELS + 1):
            for ci, (d, _) in enumerate(chains):
                if lv == 0:
                    ql, kl = q[ci], k[ci]
                else:
                    el = e[ci][(lv - 1) * n:lv * n]
                    ql = q[ci] * el * qm_ref[d, lv]
                    kl = k[ci] * el * km_ref[d, lv]
                kbd = (jnp.concatenate([kl] * GLA_HEADS, axis=0) * bdk).astype(BF16)
                scores[ci] = scores[ci] + _dot_nt(ql.astype(BF16), kbd) * sm_ref[d, lv]

        kt = []
        for ci in range(len(chains)):
            kh = k[ci] * e[ci][GLA_ROW_SUF:GLA_ROW_LAST]
            e_last = e[ci][GLA_ROW_LAST:GLA_ROW_LAST + 1]
            kt.append(jnp.concatenate([kh, jnp.broadcast_to(e_last, (n, hk))], axis=0).T)
        dstate = []
        o_intra = []
        for ci in range(len(chains)):
            vpad = jnp.concatenate([v[ci], jnp.zeros_like(v[ci])], axis=0).astype(BF16)
            dstate.append(_dot(kt[ci].astype(BF16), vpad) * bdv)
            vbd = (jnp.concatenate([v[ci]] * GLA_HEADS, axis=0) * bdv).astype(BF16)
            o_intra.append(_dot(scores[ci].astype(BF16), vbd))

        st = [st_scr[0], st_scr[1]]
        for ci, (d, _) in enumerate(chains):
            qh = (q[ci] * e[ci][GLA_ROW_PRE:GLA_ROW_SUF]).astype(BF16)
            oacc_scr[pl.ds(r0[ci], n), :] += o_intra[ci] + _dot(qh, st[d].astype(BF16))
            st[d] = st[d] * kt[ci][:, n:n + 1] + dstate[ci]
        st_scr[0] = st[0]
        st_scr[1] = st[1]
        return carry

    lax.fori_loop(0, nc // steps, scan_iter, 0)

    tfin = _pick(seq_t, (256, 128))
    for t0 in range(0, seq_t, tfin):
        o = oacc_scr[t0:t0 + tfin, :]
        parts = [_rms(o[:, h * dv:(h + 1) * dv], gain_ref[:, h * dv:(h + 1) * dv])
                 for h in range(GLA_HEADS)]
        gate = pa_ref[0, t0:t0 + tfin, 2 * hk + hv:2 * hk + 2 * hv]
        o_ref[0, t0:t0 + tfin, :] = (jnp.concatenate(parts, axis=1) * _silu(gate)).astype(BF16)


def _gla(p, lr, gw, gb, gain, consts, *, ctx_len, dk, dv):
    b, t, _ = p.shape
    hk = GLA_HEADS * dk
    hv = GLA_HEADS * dv
    pa_w = 2 * hk + 2 * hv
    cs, qm, km, sm, bdk, bdv = consts
    kern = functools.partial(_gla_kernel, seq_t=t, ctx_len=ctx_len, dk=dk, dv=dv)
    const = lambda a: pl.BlockSpec(a.shape, lambda bi: (0,) * a.ndim)
    return pl.pallas_call(
        kern,
        grid=(b,),
        in_specs=[pl.BlockSpec((1, t, pa_w), lambda bi: (bi, 0, 0)),
                  pl.BlockSpec((1, t, LANES), lambda bi: (bi, 0, 0)),
                  const(gw), const(gb), const(gain), const(cs), const(qm), const(km), const(sm),
                  const(bdk), const(bdv)],
        out_specs=pl.BlockSpec((1, t, hv), lambda bi: (bi, 0, 0)),
        out_shape=jax.ShapeDtypeStruct((b, t, hv), BF16),
        scratch_shapes=[pltpu.VMEM((t, 2 * hk), F32), pltpu.VMEM((t, hv), F32),
                        pltpu.VMEM((2, hk, hv), F32)],
        compiler_params=_cparams(1),
        name="gla",
    )(p, lr, gw, gb, gain, cs, qm, km, sm, bdk, bdv)


def _rope_tables(length, ctx_len, n_heads):
    rows = length // GRID_W
    row = jnp.repeat(jnp.arange(rows), GRID_W).astype(F32)
    col = jnp.tile(jnp.arange(GRID_W), rows).astype(F32)
    quarter = HEAD_DIM // 4
    inv = ROPE_THETA ** (-jnp.arange(quarter, dtype=F32) / quarter)
    ar, ac = row[:, None] * inv, col[:, None] * inv
    ang = jnp.concatenate([ar, ar, ac, ac], axis=-1)
    cos, sin = jnp.cos(ang), jnp.sin(ang)
    first = (jnp.arange(HEAD_DIM) % (HEAD_DIM // 2)) < quarter
    sin_signed = jnp.where(first[None, :], -sin, sin)
    cos = jnp.concatenate([jnp.ones((ctx_len, HEAD_DIM), F32), cos], axis=0)
    sin_signed = jnp.concatenate([jnp.zeros((ctx_len, HEAD_DIM), F32), sin_signed], axis=0)
    return jnp.tile(cos, (1, n_heads)), jnp.tile(sin_signed, (1, n_heads))


def _pair_order(n_heads, n_kv):
    g = n_heads // n_kv
    return np.array([kv * g + j for j in range(g) for kv in range(n_kv)])


def _head_cols(order, width):
    return (order[:, None] * width + np.arange(width)[None, :]).reshape(-1)


def _block_diag_mask(rows, cols, rb, cb):
    r = np.arange(rows)[:, None] // rb
    c = np.arange(cols)[None, :] // cb
    return (r == c).astype(np.float32)


def kernel(x, c, ctx, c_ctx, w_mod, b_mod, norm_g, w_in, gla_gate_up, gla_gate_b, win_sink, cm_ln_g,
           cm_ln_b, cm_ws, cm_bs, qk_g, mix_g, w_out, w_ffn_in, w_ffn_out):
    bsz, seq, d = x.shape
    ctx_len = ctx.shape[1]
    seq_t = ctx_len + seq
    depth = w_mod.shape[0]
    gw = d // 4
    rank = gla_gate_up.shape[2]
    dk = gla_gate_up.shape[3] // GLA_HEADS
    dv = gw // GLA_HEADS
    n_heads = gw // HEAD_DIM
    kvw = WIN_KV * HEAD_DIM
    assert kvw == LANES and GA_KV == WIN_KV and gw % LANES == 0 and 2 * rank <= LANES
    assert ctx_len % CM_CHUNK == 0 and seq % CM_CHUNK == 0 and seq % GRID_W == 0

    sizes = [GLA_HEADS * dk, GLA_HEADS * dk, gw, gw, rank, rank, gw, kvw, kvw, gw, gw, gw, kvw, kvw]
    off = np.concatenate([[0], np.cumsum(sizes)])
    a_q, a_k, a_v, a_g, a_lrf, a_lrb, b_q, b_k, b_v, c_u, c_v, d_q, d_k, d_v = [
        np.arange(off[i], off[i + 1]) for i in range(14)]
    order = _pair_order(n_heads, WIN_KV)
    hcols = _head_cols(order, HEAD_DIM)
    main_cols = np.concatenate([a_q, a_k, a_v, a_g, c_u, c_v, b_q[hcols], d_q[hcols], b_k, b_v, d_k, d_v])
    lr_cols = np.concatenate([a_lrf, a_lrb])
    assert 2 * GLA_HEADS * dk + 2 * gw == 3 * gw
    blk_cu, blk_cv, blk_bq, blk_dq = 3, 4, 5, 6
    blk_bkv = (7 * gw) // (2 * LANES)
    blk_dkv = blk_bkv + 1

    cos_q, sin_q = _rope_tables(seq, ctx_len, n_heads)
    cos_k, sin_k = cos_q[:, :LANES], sin_q[:, :LANES]
    ones_q = jnp.asarray(_block_diag_mask(gw, gw, HEAD_DIM, HEAD_DIM), BF16)
    ones_k = ones_q[:LANES, :LANES]
    tabs = (cos_q, sin_q, cos_k, sin_k, ones_q, ones_k)

    csum, qmask, kmask, smask = _gla_constants(dk)
    gla_consts = (jnp.asarray(csum, BF16), jnp.asarray(qmask), jnp.asarray(kmask), jnp.asarray(smask),
                  jnp.asarray(_block_diag_mask(GLA_HEADS * GLA_CHUNK, GLA_HEADS * dk, GLA_CHUNK, dk)),
                  jnp.asarray(_block_diag_mask(GLA_HEADS * dk, gw, dk, dv)))

    rows = 16
    cond = jnp.concatenate([c, c_ctx[None, :], jnp.zeros((rows - bsz - 1, d), F32)], axis=0)
    mod_all = _modulation(cond, w_mod, b_mod)
    mod_lat = mod_all[:, :bsz].reshape(depth, bsz, 6, d)
    mod_ctx = jnp.broadcast_to(mod_all[:, bsz].reshape(depth, 1, 6, d), (depth, bsz, 6, d))
    pad2 = jnp.zeros((depth, bsz, 2, d), F32)
    mods = jnp.concatenate([mod_ctx, pad2, mod_lat, pad2], axis=2)

    xs = jnp.concatenate([ctx, x], axis=1).reshape(bsz * seq_t, d)
    zero_sink = jnp.zeros((1, gw), F32)
    one_gain_q = jnp.ones((1, gw), F32)
    one_gain_k = jnp.ones((1, LANES), F32)

    for l in range(depth):
        w_main = w_in[l][:, main_cols].astype(BF16)
        w_lr = jnp.pad(w_in[l][:, lr_cols], ((0, 0), (0, LANES - 2 * rank))).astype(BF16)
        gate_w = jnp.zeros((LANES, 2 * GLA_HEADS * dk), F32)
        gate_w = gate_w.at[:rank, :GLA_HEADS * dk].set(gla_gate_up[l, 0])
        gate_w = gate_w.at[rank:2 * rank, GLA_HEADS * dk:].set(gla_gate_up[l, 1]).astype(BF16)
        gate_b = gla_gate_b[l].reshape(1, -1)
        mg = mix_g[l]
        ga, gb_, gc, gd = (mg[0:gw], mg[gw:2 * gw][hcols], mg[2 * gw:3 * gw], mg[3 * gw:][hcols])
        wo = w_out[l]
        wo = jnp.concatenate([wo[0:gw], wo[gw:2 * gw][hcols], wo[2 * gw:3 * gw], wo[3 * gw:][hcols]],
                             axis=0).astype(BF16)
        sink = jnp.repeat(win_sink[l][order], HEAD_DIM)[None, :]
        qg = jnp.tile(qk_g[l, 0], n_heads)[None, :]
        kg = jnp.tile(qk_g[l, 1], WIN_KV)[None, :]
        bsb = jnp.repeat(cm_bs[l].T, gw // CM_GROUPS, axis=1)
        modl = mods[l]

        p, lr = _inproj(xs, modl, norm_g[l, 0][None, :], w_main, w_lr, seq_t=seq_t, ctx_len=ctx_len)
        p3 = p.reshape(bsz, seq_t, -1)
        lr3 = lr.reshape(bsz, seq_t, LANES)
        y_a = _gla(p3, lr3, gate_w, gate_b, ga[None, :], gla_consts, ctx_len=ctx_len, dk=dk, dv=dv)
        y_b = _attention(p3, tabs, one_gain_q, one_gain_k, sink, gb_[None, :], mode="window",
                         q_blk=blk_bq, kv_blk=blk_bkv, ctx_len=ctx_len)
        y_c = _cmlp(p3, cm_ln_g[l][None, :], cm_ln_b[l][None, :], cm_ws[l].astype(BF16), bsb,
                    gc[None, :], u_blk=blk_cu, v_blk=blk_cv)
        y_d = _attention(p3, tabs, qg, kg, zero_sink, gd[None, :], mode="global",
                         q_blk=blk_dq, kv_blk=blk_dkv, ctx_len=ctx_len)
        ys = [y.reshape(bsz * seq_t, gw) for y in (y_a, y_b, y_c, y_d)]
        xs = _outproj(xs, ys, modl, norm_g[l, 1][None, :], wo, seq_t=seq_t, ctx_len=ctx_len)
        xs = _ffn(xs, modl, norm_g[l, 2][None, :], norm_g[l, 3][None, :], w_ffn_in[l].astype(BF16),
                  w_ffn_out[l].astype(BF16), seq_t=seq_t, ctx_len=ctx_len)

    return xs.reshape(bsz, seq_t, d)[:, ctx_len:]
```

```python
import functools
import math

import numpy as np
import jax
import jax.numpy as jnp
from jax import lax
from jax.experimental import pallas as pl
from jax.experimental.pallas import tpu as pltpu

F32 = jnp.float32
BF16 = jnp.bfloat16

EPS = 1e-6
HEAD_DIM = 64
ATTN_SCALE = HEAD_DIM ** -0.5
ROPE_THETA = 10000.0
GRID_W = 64
GLA_HEADS = 4
GLA_TAU = 16.0
GLA_CHUNK = 64
WIN_KV = 2
WINDOW = 128
CM_GROUPS = 4
CM_CHUNK = 128
GA_KV = 2

LANES = 128
NEG_BIG = -1e30
V7X_VMEM_LIMIT = 56 * 1024 * 1024
ROW_CHUNK = 64


def _cparams(n_axes, vmem=V7X_VMEM_LIMIT):
    return pltpu.CompilerParams(dimension_semantics=("arbitrary",) * n_axes,
                                vmem_limit_bytes=vmem)


def _pick(total, candidates):
    for cand in candidates:
        if total % cand == 0:
            return cand
    raise ValueError(f"no tile in {candidates} divides {total}")


def _resident(shape):
    return pl.BlockSpec(shape, lambda *_: (0,) * len(shape), pipeline_mode=pl.Buffered(1))


def _dot(a, b):
    return jnp.dot(a, b, preferred_element_type=F32)


def _dot_nt(a, b):
    return lax.dot_general(a, b, (((1,), (1,)), ((), ())), preferred_element_type=F32)


def _rms(x, gain):
    return x * lax.rsqrt(jnp.mean(x * x, axis=-1, keepdims=True) + EPS) * gain


def _silu(x):
    return x * jax.nn.sigmoid(x)


def _gelu_tanh(x):
    return 0.5 * x * (1.0 + jnp.tanh(math.sqrt(2.0 / math.pi) * (x + 0.044715 * (x * x * x))))


def _mod_vec(mod_ref, k, chunk_is_ctx):
    return jnp.where(chunk_is_ctx, mod_ref[0, k:k + 1, :], mod_ref[0, 8 + k:9 + k, :])


def _row_loop(n_rows, step):
    n_chunks = n_rows // ROW_CHUNK
    lax.fori_loop(0, n_chunks, step, 0, unroll=3 if n_chunks % 3 == 0 else 2)


def _norm_modulate_rows(x_ref, h_ref, g_ref, mod_ref, k_shift, k_scale, row0, ctx_len):
    def step(c, carry):
        r = pl.multiple_of(c * ROW_CHUNK, ROW_CHUNK)
        is_ctx = row0 + r < ctx_len
        gain = g_ref[...] * (1.0 + _mod_vec(mod_ref, k_scale, is_ctx))
        x = x_ref[pl.ds(r, ROW_CHUNK), :]
        inv = lax.rsqrt(jnp.mean(x * x, axis=-1, keepdims=True) + EPS)
        h_ref[pl.ds(r, ROW_CHUNK), :] = (x * inv * gain + _mod_vec(mod_ref, k_shift, is_ctx)).astype(BF16)
        return carry

    _row_loop(x_ref.shape[0], step)


def _gated_residual_rows(x_ref, z_ref, o_ref, g_ref, mod_ref, k_gate, row0, ctx_len):
    def step(c, carry):
        r = pl.multiple_of(c * ROW_CHUNK, ROW_CHUNK)
        gain = g_ref[...] * _mod_vec(mod_ref, k_gate, row0 + r < ctx_len)
        z = z_ref[pl.ds(r, ROW_CHUNK), :]
        inv = lax.rsqrt(jnp.mean(z * z, axis=-1, keepdims=True) + EPS)
        o_ref[pl.ds(r, ROW_CHUNK), :] = x_ref[pl.ds(r, ROW_CHUNK), :] + z * inv * gain
        return carry

    _row_loop(x_ref.shape[0], step)


def _mod_kernel(c_ref, w_ref, b_ref, o_ref):
    s = _silu(c_ref[...]).astype(BF16)
    o_ref[0] = _dot(s, w_ref[0].astype(BF16)) + b_ref[0]


def _modulation(cond, w_mod, b_mod):
    depth, d, n = w_mod.shape
    rows = cond.shape[0]
    tn = _pick(n, (1024, 512, 256, 128))
    return pl.pallas_call(
        _mod_kernel,
        grid=(depth, n // tn),
        in_specs=[pl.BlockSpec((rows, d), lambda l, j: (0, 0)),
                  pl.BlockSpec((1, d, tn), lambda l, j: (l, 0, j)),
                  pl.BlockSpec((1, 1, tn), lambda l, j: (l, 0, j))],
        out_specs=pl.BlockSpec((1, rows, tn), lambda l, j: (l, 0, j)),
        out_shape=jax.ShapeDtypeStruct((depth, rows, n), F32),
        compiler_params=_cparams(2),
        name="modulation",
    )(cond, w_mod, b_mod.reshape(depth, 1, n))


def _inproj_kernel(x_ref, mod_ref, g_ref, w_ref, wlr_ref, p_ref, lr_ref, h_scr, *,
                   tm, tiles_per_batch, ctx_len):
    row0 = (pl.program_id(0) % tiles_per_batch) * tm
    _norm_modulate_rows(x_ref, h_scr, g_ref, mod_ref, 0, 1, row0, ctx_len)
    h = h_scr[...]
    lr_ref[...] = _dot(h, wlr_ref[...])
    p_ref[...] = _dot(h, w_ref[...])


def _inproj(xs, modl, gain, w_main, w_lr, *, seq_t, ctx_len):
    m, d = xs.shape
    n = w_main.shape[1]
    tm = _pick(seq_t, (576, 384, 128))
    tpb = seq_t // tm
    kern = functools.partial(_inproj_kernel, tm=tm, tiles_per_batch=tpb, ctx_len=ctx_len)
    return pl.pallas_call(
        kern,
        grid=(m // tm,),
        in_specs=[pl.BlockSpec((tm, d), lambda i: (i, 0)),
                  pl.BlockSpec((1, 16, d), lambda i: (i // tpb, 0, 0)),
                  _resident((1, d)), _resident((d, n)), _resident((d, LANES))],
        out_specs=[pl.BlockSpec((tm, n), lambda i: (i, 0)),
                   pl.BlockSpec((tm, LANES), lambda i: (i, 0))],
        out_shape=[jax.ShapeDtypeStruct((m, n), F32),
                   jax.ShapeDtypeStruct((m, LANES), F32)],
        scratch_shapes=[pltpu.VMEM((tm, d), BF16)],
        compiler_params=_cparams(1),
        name="inproj",
    )(xs, modl, gain, w_main, w_lr)


def _outproj_kernel(x_ref, ya_ref, yb_ref, yc_ref, yd_ref, mod_ref, g_ref, w_ref, o_ref, y_scr, *,
                    tm, tiles_per_batch, ctx_len, gw):
    i = pl.program_id(0)
    y_scr[:, 0 * gw:1 * gw] = ya_ref[...]
    y_scr[:, 1 * gw:2 * gw] = yb_ref[...]
    y_scr[:, 2 * gw:3 * gw] = yc_ref[...]
    y_scr[:, 3 * gw:4 * gw] = yd_ref[...]
    zn = _rms(_dot(y_scr[...], w_ref[...]), g_ref[...])
    row0 = (i % tiles_per_batch) * tm

    @pl.when(row0 >= ctx_len)
    def _():
        o_ref[...] = x_ref[...] + mod_ref[0, 10:11, :] * zn

    @pl.when(row0 < ctx_len)
    def _():
        is_ctx = row0 + lax.broadcasted_iota(jnp.int32, (tm, 1), 0) < ctx_len
        o_ref[...] = x_ref[...] + jnp.where(is_ctx, mod_ref[0, 2:3, :], mod_ref[0, 10:11, :]) * zn


def _outproj(xs, ys, modl, gain, w, *, seq_t, ctx_len):
    m, d = xs.shape
    gw = ys[0].shape[1]
    tm = _pick(seq_t, (576, 384, 128))
    tpb = seq_t // tm
    kern = functools.partial(_outproj_kernel, tm=tm, tiles_per_batch=tpb, ctx_len=ctx_len, gw=gw)
    yspec = pl.BlockSpec((tm, gw), lambda i: (i, 0))
    return pl.pallas_call(
        kern,
        grid=(m // tm,),
        in_specs=[pl.BlockSpec((tm, d), lambda i: (i, 0)), yspec, yspec, yspec, yspec,
                  pl.BlockSpec((1, 16, d), lambda i: (i // tpb, 0, 0)),
                  _resident((1, d)), _resident((4 * gw, d))],
        out_specs=pl.BlockSpec((tm, d), lambda i: (i, 0)),
        out_shape=jax.ShapeDtypeStruct((m, d), F32),
        scratch_shapes=[pltpu.VMEM((tm, 4 * gw), BF16)],
        compiler_params=_cparams(1),
        name="outproj",
    )(xs, *ys, modl, gain, w)


def _ffn_kernel(x_ref, mod_ref, g2_ref, g3_ref, wg_ref, wu_ref, wo_ref, o_ref, h_scr, acc_scr, *,
                tm, tiles_per_batch, ctx_len, nf):
    i = pl.program_id(0)
    j = pl.program_id(1)
    row0 = (i % tiles_per_batch) * tm

    @pl.when(j == 0)
    def _():
        _norm_modulate_rows(x_ref, h_scr, g2_ref, mod_ref, 3, 4, row0, ctx_len)
        acc_scr[...] = jnp.zeros_like(acc_scr)

    h = h_scr[...]
    act = (_silu(_dot(h, wg_ref[...])) * _dot(h, wu_ref[...])).astype(BF16)
    acc_scr[...] += _dot(act, wo_ref[...])

    @pl.when(j == nf - 1)
    def _():
        _gated_residual_rows(x_ref, acc_scr, o_ref, g3_ref, mod_ref, 5, row0, ctx_len)


def _ffn(xs, modl, g2, g3, w_in, w_out, *, seq_t, ctx_len):
    m, d = xs.shape
    dff = w_out.shape[0]
    tm = _pick(seq_t, (768, 384, 128))
    tf = _pick(dff, (512, 256, 128))
    nf = dff // tf
    tpb = seq_t // tm
    kern = functools.partial(_ffn_kernel, tm=tm, tiles_per_batch=tpb, ctx_len=ctx_len, nf=nf)
    return pl.pallas_call(
        kern,
        grid=(m // tm, nf),
        in_specs=[pl.BlockSpec((tm, d), lambda i, j: (i, 0)),
                  pl.BlockSpec((1, 16, d), lambda i, j: (i // tpb, 0, 0)),
                  _resident((1, d)), _resident((1, d)),
                  pl.BlockSpec((d, tf), lambda i, j: (0, j)),
                  pl.BlockSpec((d, tf), lambda i, j: (0, nf + j)),
                  pl.BlockSpec((tf, d), lambda i, j: (j, 0))],
        out_specs=pl.BlockSpec((tm, d), lambda i, j: (i, 0)),
        out_shape=jax.ShapeDtypeStruct((m, d), F32),
        scratch_shapes=[pltpu.VMEM((tm, d), BF16), pltpu.VMEM((tm, d), F32)],
        compiler_params=_cparams(2),
        name="ffn",
    )(xs, modl, g2, g3, w_in, w_in, w_out)


def _cmlp_tile(u_ref, v_ref, lng_ref, lnb_ref, ws_ref, bsb_ref, gc_ref, o_ref):
    tc, gw = u_ref.shape[1], u_ref.shape[2]
    cw = gw // CM_GROUPS
    u = _gelu_tanh(u_ref[0])
    v = _gelu_tanh(v_ref[0])
    vc = v - jnp.mean(v, axis=-1, keepdims=True)
    vn = vc * lax.rsqrt(jnp.mean(vc * vc, axis=-1, keepdims=True) + EPS) * lng_ref[...] + lnb_ref[...]
    vb = vn.astype(BF16)
    rows = []
    for ch in range(tc // CM_CHUNK):
        cols = []
        for g in range(CM_GROUPS):
            blk = vb[ch * CM_CHUNK:(ch + 1) * CM_CHUNK, g * cw:(g + 1) * cw]
            cols.append(_dot(ws_ref[g], blk) + bsb_ref[:, g * cw:(g + 1) * cw])
        rows.append(jnp.concatenate(cols, axis=1))
    s = jnp.concatenate(rows, axis=0)
    o_ref[0] = _rms(u * s, gc_ref[...]).astype(BF16)


def _group_mean_sq(x, ones_bd):
    sq = x * x
    hi = sq.astype(BF16)
    lo = (sq - hi.astype(F32)).astype(BF16)
    return (_dot(hi, ones_bd) + _dot(lo, ones_bd)) * (1.0 / HEAD_DIM)


def _rope(x, cos, sin_signed):
    w = x.shape[1]
    lane = lax.broadcasted_iota(jnp.int32, x.shape, 1)
    first = (lane % (HEAD_DIM // 2)) < (HEAD_DIM // 4)
    partner = jnp.where(first, pltpu.roll(x, w - HEAD_DIM // 4, 1), pltpu.roll(x, HEAD_DIM // 4, 1))
    return x * cos + partner * sin_signed


def _attn_tile(q, k_scr, v_scr, sink_ref, mg_ref, o_ref, *, mode, latent_tile, i, tq, seq_t, ctx_len):
    n_pairs = q.shape[1] // LANES
    lo = lax.broadcasted_iota(jnp.int32, (tq, LANES), 1) < HEAD_DIM
    lane1 = lax.broadcasted_iota(jnp.int32, (1, LANES), 1)
    row2 = lax.broadcasted_iota(jnp.int32, (2 * tq, 1), 0)

    key_sets = [(0, ctx_len, None)]
    if latent_tile and mode == "window":
        wk = tq + 2 * WINDOW
        start = pl.multiple_of(jnp.clip(i * tq - WINDOW, ctx_len, seq_t - wk), LANES)
        r = lax.broadcasted_iota(jnp.int32, (2 * tq, wk), 0)
        r = jnp.where(r >= tq, r - tq, r)
        c = lax.broadcasted_iota(jnp.int32, (2 * tq, wk), 1)
        dist = (i * tq + r) - (start + c)
        key_sets.append((start, wk, (jnp.abs(dist) <= WINDOW)))
    elif latent_tile:
        key_sets = [(0, seq_t, None)]

    outs = []
    for j in range(n_pairs):
        qp = q[:, j * LANES:(j + 1) * LANES]
        q2 = jnp.concatenate([jnp.where(lo, qp, 0.0), jnp.where(lo, 0.0, qp)], axis=0).astype(BF16)
        scores = []
        for k0, kn, valid in key_sets:
            s = _dot_nt(q2, k_scr[pl.ds(k0, kn), :])
            scores.append(s if valid is None else jnp.where(valid, s, NEG_BIG))
        mx = functools.reduce(jnp.maximum, [jnp.max(s, axis=1, keepdims=True) for s in scores])
        if mode == "window":
            sv = sink_ref[:, j * LANES:(j + 1) * LANES]
            s_lo = jnp.max(jnp.where(lane1 < HEAD_DIM, sv, NEG_BIG), axis=1, keepdims=True)
            s_hi = jnp.max(jnp.where(lane1 < HEAD_DIM, NEG_BIG, sv), axis=1, keepdims=True)
            sk = jnp.where(row2 < tq, s_lo, s_hi)
            mx = jnp.maximum(mx, sk)
            den = jnp.exp(sk - mx)
        else:
            den = 0.0
        o2 = 0.0
        for (k0, kn, _), s in zip(key_sets, scores):
            p = jnp.exp(s - mx)
            den = den + jnp.sum(p, axis=1, keepdims=True)
            o2 = o2 + _dot(p.astype(BF16), v_scr[pl.ds(k0, kn), :])
        o2 = o2 / den
        outs.append(jnp.where(lo, o2[:tq], o2[tq:]))
    o = jnp.concatenate(outs, axis=1)
    o_ref[0] = _rms(o, mg_ref[...]).astype(BF16)


def _mixers_kernel(bq_ref, dq_ref, cu_ref, cv_ref, bkv_ref, dkv_ref, cq_ref, sq_ref, ck_ref, sk_ref,
                   qg_ref, kg_ref, onesq_ref, onesk_ref, sink_ref, gb_ref, gd_ref,
                   lng_ref, lnb_ref, ws_ref, bsb_ref, gc_ref,
                   yb_ref, yc_ref, yd_ref, kb_scr, vb_scr, kd_scr, vd_scr, *, tq, seq_t, ctx_len):
    i = pl.program_id(1)
    n_ctx_tiles = ctx_len // tq

    @pl.when(i == 0)
    def _():
        kv = bkv_ref[0]
        kb_scr[...] = _rope(kv[:, :LANES], ck_ref[...], sk_ref[...]).astype(BF16)
        vb_scr[...] = kv[:, LANES:].astype(BF16)
        kv = dkv_ref[0]
        k = kv[:, :LANES]
        k = k * lax.rsqrt(_group_mean_sq(k, onesk_ref[...]) + EPS) * kg_ref[...]
        kd_scr[...] = _rope(k, ck_ref[...], sk_ref[...]).astype(BF16)
        vd_scr[...] = kv[:, LANES:].astype(BF16)

    qb = _rope(bq_ref[0], cq_ref[...], sq_ref[...]) * ATTN_SCALE
    qd = dq_ref[0]
    qd = qd * lax.rsqrt(_group_mean_sq(qd, onesq_ref[...]) + EPS) * qg_ref[...]
    qd = _rope(qd, cq_ref[...], sq_ref[...]) * ATTN_SCALE

    def attend(latent_tile):
        common = dict(latent_tile=latent_tile, i=i, tq=tq, seq_t=seq_t, ctx_len=ctx_len)
        _attn_tile(qb, kb_scr, vb_scr, sink_ref, gb_ref, yb_ref, mode="window", **common)
        _attn_tile(qd, kd_scr, vd_scr, None, gd_ref, yd_ref, mode="global", **common)

    pl.when(i < n_ctx_tiles)(functools.partial(attend, False))
    pl.when(i >= n_ctx_tiles)(functools.partial(attend, True))
    _cmlp_tile(cu_ref, cv_ref, lng_ref, lnb_ref, ws_ref, bsb_ref, gc_ref, yc_ref)


def _mixers(p, tabs, qg, kg, sink, gb, gd, lng, lnb, ws, bsb, gc, *, blocks, ctx_len):
    b, t, _ = p.shape
    gw = gb.shape[1]
    tq = _pick(t, (256, 128))
    assert ctx_len % tq == 0 and tq % CM_CHUNK == 0
    cq, sq, ck, sk, ones_q, ones_k = tabs
    kern = functools.partial(_mixers_kernel, tq=tq, seq_t=t, ctx_len=ctx_len)
    const = lambda a: _resident(a.shape)
    qspec = lambda blk: pl.BlockSpec((1, tq, gw), lambda bi, i: (bi, i, blk))
    kvspec = lambda blk: pl.BlockSpec((1, t, 2 * LANES), lambda bi, i: (bi, 0, blk))
    tabspec = pl.BlockSpec((tq, gw), lambda bi, i: (i, 0))
    out = jax.ShapeDtypeStruct((b, t, gw), BF16)
    ospec = pl.BlockSpec((1, tq, gw), lambda bi, i: (bi, i, 0))
    kvscr = pltpu.VMEM((t, LANES), BF16)
    return pl.pallas_call(
        kern,
        grid=(b, t // tq),
        in_specs=[qspec(blocks["b_q"]), qspec(blocks["d_q"]), qspec(blocks["c_u"]), qspec(blocks["c_v"]),
                  kvspec(blocks["b_kv"]), kvspec(blocks["d_kv"]),
                  tabspec, tabspec, const(ck), const(sk), const(qg), const(kg),
                  const(ones_q), const(ones_k), const(sink), const(gb), const(gd),
                  const(lng), const(lnb), const(ws), const(bsb), const(gc)],
        out_specs=[ospec, ospec, ospec],
        out_shape=[out, out, out],
        scratch_shapes=[kvscr, kvscr, kvscr, kvscr],
        compiler_params=_cparams(2),
        name="mixers_bcd",
    )(p, p, p, p, p, p, cq, sq, ck, sk, qg, kg, ones_q, ones_k, sink, gb, gd, lng, lnb, ws, bsb, gc)


GLA_LEVELS = int(math.log2(GLA_CHUNK))
GLA_ROW_PRE = GLA_LEVELS * GLA_CHUNK
GLA_ROW_SUF = GLA_ROW_PRE + GLA_CHUNK
GLA_ROW_LAST = GLA_ROW_SUF + GLA_CHUNK
GLA_ROWS = GLA_ROW_LAST + 16
GLA_STEPS_PER_ITER = 2


def _gla_constants(dk):
    n = GLA_CHUNK
    csum = np.zeros((2, GLA_ROWS, n), np.float32)
    qmask = np.zeros((2, GLA_LEVELS + 1, n, 1), np.float32)
    kmask = np.zeros((2, GLA_LEVELS + 1, n, 1), np.float32)
    smask = np.zeros((2, GLA_LEVELS + 1, n, n), np.float32)
    for d in range(2):
        tau = np.arange(n) if d == 0 else n - 1 - np.arange(n)
        qmask[d, 0] = 1.0
        kmask[d, 0] = 1.0
        smask[d, 0] = np.eye(n)
        for lv in range(1, GLA_LEVELS + 1):
            h = 1 << (lv - 1)
            blk = tau // (2 * h)
            upper = (tau % (2 * h)) >= h
            same = blk[:, None] == blk[None, :]
            both_up = upper[:, None] & upper[None, :]
            both_lo = (~upper[:, None]) & (~upper[None, :])
            c = np.where(upper[:, None],
                         same & both_up & (tau[None, :] <= tau[:, None]),
                         same & both_lo & (tau[None, :] > tau[:, None]))
            csum[d, (lv - 1) * n:lv * n] = c
            qmask[d, lv, :, 0] = upper
            kmask[d, lv, :, 0] = ~upper
            smask[d, lv] = same & upper[:, None] & (~upper[None, :])
        csum[d, GLA_ROW_PRE:GLA_ROW_SUF] = tau[None, :] <= tau[:, None]
        csum[d, GLA_ROW_SUF:GLA_ROW_LAST] = tau[None, :] > tau[:, None]
        csum[d, GLA_ROW_LAST:] = 1.0
    hk = GLA_HEADS * dk
    qmask = np.broadcast_to(qmask, (2, GLA_LEVELS + 1, n, hk)).copy()
    kmask = np.broadcast_to(kmask, (2, GLA_LEVELS + 1, n, hk)).copy()
    smask = np.tile(smask, (1, 1, 1, GLA_HEADS))
    return csum, qmask, kmask, smask


def _gla_kernel(pa_ref, lr_ref, gw_ref, gb_ref, gain_ref, cs_ref, qm_ref, km_ref, sm_ref, bdk_ref,
                bdv_ref, o_ref, la_scr, oacc_scr, st_scr, *, seq_t, ctx_len, dk, dv):
    hk = GLA_HEADS * dk
    hv = GLA_HEADS * dv
    n = GLA_CHUNK
    nc = seq_t // n
    nc_ctx = ctx_len // n

    z = _dot(lr_ref[0].astype(BF16), gw_ref[...]) + gb_ref[...]
    la_scr[...] = (jnp.minimum(z, 0.0) - jnp.log1p(jnp.exp(-jnp.abs(z)))) * (1.0 / GLA_TAU)
    oacc_scr[...] = jnp.zeros_like(oacc_scr)
    st_scr[...] = jnp.zeros_like(st_scr)

    bdk = bdk_ref[...]
    bdv = bdv_ref[...]

    steps = GLA_STEPS_PER_ITER
    assert nc % steps == 0
    chains = [(d, u) for u in range(steps) for d in (0, 1)]

    def scan_iter(it, carry):
        r0 = []
        for d, u in chains:
            s = it * steps + u
            c = s if d == 0 else jnp.where(s < nc_ctx, nc_ctx - 1 - s, nc - 1 - (s - nc_ctx))
            r0.append(pl.multiple_of(c * n, n))
        q = [pa_ref[0, pl.ds(r, n), 0:hk] * (dk ** -0.5) for r in r0]
        k = [pa_ref[0, pl.ds(r, n), hk:2 * hk] for r in r0]
        v = [pa_ref[0, pl.ds(r, n), 2 * hk:2 * hk + hv] for r in r0]
        ex2 = []
        for (d, _), r in zip(chains, r0):
            a = la_scr[pl.ds(r, n), d * hk:(d + 1) * hk]
            a_hi = a.astype(BF16)
            a_lo = (a - a_hi.astype(F32)).astype(BF16)
            ex2.append(_dot(cs_ref[d], jnp.concatenate([a_hi, a_lo], axis=1)))
        e = [jnp.exp(x[:, :hk] + x[:, hk:]) for x in ex2]

        scores = [jnp.zeros((n, GLA_HEADS * n), F32) for _ in chains]
        for lv in range(GLA_LEVELS + 1):
            for ci, (d, _) in enumerate(chains):
                if lv == 0:
                    ql, kl = q[ci], k[ci]
                else:
                    el = e[ci][(lv - 1) * n:lv * n]
                    ql = q[ci] * el * qm_ref[d, lv]
                    kl = k[ci] * el * km_ref[d, lv]
                kbd = (jnp.concatenate([kl] * GLA_HEADS, axis=0) * bdk).astype(BF16)
                scores[ci] = scores[ci] + _dot_nt(ql.astype(BF16), kbd) * sm_ref[d, lv]

        kt = []
        for ci in range(len(chains)):
            kh = k[ci] * e[ci][GLA_ROW_SUF:GLA_ROW_LAST]
            e_last = e[ci][GLA_ROW_LAST:GLA_ROW_LAST + 1]
            kt.append(jnp.concatenate([kh, jnp.broadcast_to(e_last, (n, hk))], axis=0).T)
        dstate = []
        o_intra = []
        for ci in range(len(chains)):
            vpad = jnp.concatenate([v[ci], jnp.zeros_like(v[ci])], axis=0).astype(BF16)
            dstate.append(_dot(kt[ci].astype(BF16), vpad) * bdv)
            vbd = (jnp.concatenate([v[ci]] * GLA_HEADS, axis=0) * bdv).astype(BF16)
            o_intra.append(_dot(scores[ci].astype(BF16), vbd))

        st = [st_scr[0], st_scr[1]]
        for ci, (d, _) in enumerate(chains):
            qh = (q[ci] * e[ci][GLA_ROW_PRE:GLA_ROW_SUF]).astype(BF16)
            oacc_scr[pl.ds(r0[ci], n), :] += o_intra[ci] + _dot(qh, st[d].astype(BF16))
            st[d] = st[d] * kt[ci][:, n:n + 1] + dstate[ci]
        st_scr[0] = st[0]
        st_scr[1] = st[1]
        return carry

    lax.fori_loop(0, nc // steps, scan_iter, 0)

    tfin = _pick(seq_t, (256, 128))
    for t0 in range(0, seq_t, tfin):
        o = oacc_scr[t0:t0 + tfin, :]
        parts = [_rms(o[:, h * dv:(h + 1) * dv], gain_ref[:, h * dv:(h + 1) * dv])
                 for h in range(GLA_HEADS)]
        gate = pa_ref[0, t0:t0 + tfin, 2 * hk + hv:2 * hk + 2 * hv]
        o_ref[0, t0:t0 + tfin, :] = (jnp.concatenate(parts, axis=1) * _silu(gate)).astype(BF16)


def _gla(p, lr, gw, gb, gain, consts, *, ctx_len, dk, dv):
    b, t, _ = p.shape
    hk = GLA_HEADS * dk
    hv = GLA_HEADS * dv
    pa_w = 2 * hk + 2 * hv
    cs, qm, km, sm, bdk, bdv = consts
    kern = functools.partial(_gla_kernel, seq_t=t, ctx_len=ctx_len, dk=dk, dv=dv)
    const = lambda a: _resident(a.shape)
    return pl.pallas_call(
        kern,
        grid=(b,),
        in_specs=[pl.BlockSpec((1, t, pa_w), lambda bi: (bi, 0, 0)),
                  pl.BlockSpec((1, t, LANES), lambda bi: (bi, 0, 0)),
                  const(gw), const(gb), const(gain), const(cs), const(qm), const(km), const(sm),
                  const(bdk), const(bdv)],
        out_specs=pl.BlockSpec((1, t, hv), lambda bi: (bi, 0, 0)),
        out_shape=jax.ShapeDtypeStruct((b, t, hv), BF16),
        scratch_shapes=[pltpu.VMEM((t, 2 * hk), F32), pltpu.VMEM((t, hv), F32),
                        pltpu.VMEM((2, hk, hv), F32)],
        compiler_params=_cparams(1),
        name="gla",
    )(p, lr, gw, gb, gain, cs, qm, km, sm, bdk, bdv)


def _rope_tables(length, ctx_len, n_heads):
    rows = length // GRID_W
    row = jnp.repeat(jnp.arange(rows), GRID_W).astype(F32)
    col = jnp.tile(jnp.arange(GRID_W), rows).astype(F32)
    quarter = HEAD_DIM // 4
    inv = ROPE_THETA ** (-jnp.arange(quarter, dtype=F32) / quarter)
    ar, ac = row[:, None] * inv, col[:, None] * inv
    ang = jnp.concatenate([ar, ar, ac, ac], axis=-1)
    cos, sin = jnp.cos(ang), jnp.sin(ang)
    first = (jnp.arange(HEAD_DIM) % (HEAD_DIM // 2)) < quarter
    sin_signed = jnp.where(first[None, :], -sin, sin)
    cos = jnp.concatenate([jnp.ones((ctx_len, HEAD_DIM), F32), cos], axis=0)
    sin_signed = jnp.concatenate([jnp.zeros((ctx_len, HEAD_DIM), F32), sin_signed], axis=0)
    return jnp.tile(cos, (1, n_heads)), jnp.tile(sin_signed, (1, n_heads))


def _pair_heads(a, axis, n_kv):
    axis = axis % a.ndim
    g = a.shape[axis] // (n_kv * HEAD_DIM)
    shape = a.shape[:axis] + (n_kv, g, HEAD_DIM) + a.shape[axis + 1:]
    return jnp.swapaxes(a.reshape(shape), axis, axis + 1).reshape(a.shape)


def _block_diag_mask(rows, cols, rb, cb):
    r = np.arange(rows)[:, None] // rb
    c = np.arange(cols)[None, :] // cb
    return (r == c).astype(np.float32)


def kernel(x, c, ctx, c_ctx, w_mod, b_mod, norm_g, w_in, gla_gate_up, gla_gate_b, win_sink, cm_ln_g,
           cm_ln_b, cm_ws, cm_bs, qk_g, mix_g, w_out, w_ffn_in, w_ffn_out):
    bsz, seq, d = x.shape
    ctx_len = ctx.shape[1]
    seq_t = ctx_len + seq
    depth = w_mod.shape[0]
    gw = d // 4
    rank = gla_gate_up.shape[2]
    dk = gla_gate_up.shape[3] // GLA_HEADS
    dv = gw // GLA_HEADS
    n_heads = gw // HEAD_DIM
    kvw = WIN_KV * HEAD_DIM
    hk = GLA_HEADS * dk
    assert kvw == LANES and GA_KV == WIN_KV and gw % LANES == 0 and 2 * rank <= LANES
    assert dk == GLA_CHUNK and 2 * hk == gw
    assert ctx_len % CM_CHUNK == 0 and seq % CM_CHUNK == 0 and seq % GRID_W == 0
    assert ctx_len % ROW_CHUNK == 0

    sizes = [hk, hk, gw, gw, rank, rank, gw, kvw, kvw, gw, gw, gw, kvw, kvw]
    off = [int(v) for v in np.concatenate([[0], np.cumsum(sizes)])]
    blocks = {"c_u": 3, "c_v": 4, "b_q": 5, "d_q": 6,
              "b_kv": (7 * gw) // (2 * LANES), "d_kv": (7 * gw) // (2 * LANES) + 1}

    cos_q, sin_q = _rope_tables(seq, ctx_len, n_heads)
    cos_k, sin_k = cos_q[:, :LANES], sin_q[:, :LANES]
    ones_q = jnp.asarray(_block_diag_mask(gw, gw, HEAD_DIM, HEAD_DIM), BF16)
    ones_k = ones_q[:LANES, :LANES]
    tabs = (cos_q, sin_q, cos_k, sin_k, ones_q, ones_k)

    csum, qmask, kmask, smask = _gla_constants(dk)
    gla_consts = (jnp.asarray(csum, BF16), jnp.asarray(qmask), jnp.asarray(kmask), jnp.asarray(smask),
                  jnp.asarray(_block_diag_mask(GLA_HEADS * GLA_CHUNK, hk, GLA_CHUNK, dk)),
                  jnp.asarray(_block_diag_mask(hk, gw, dk, dv)))

    rows = 16
    cond = jnp.concatenate([c, c_ctx[None, :], jnp.zeros((rows - bsz - 1, d), F32)], axis=0)
    mod_all = _modulation(cond, w_mod, b_mod)
    mod_lat = mod_all[:, :bsz].reshape(depth, bsz, 6, d)
    mod_ctx = jnp.broadcast_to(mod_all[:, bsz].reshape(depth, 1, 6, d), (depth, bsz, 6, d))
    pad2 = jnp.zeros((depth, bsz, 2, d), F32)
    mods = jnp.concatenate([mod_ctx, pad2, mod_lat, pad2], axis=2)

    xs = jnp.concatenate([ctx, x], axis=1).reshape(bsz * seq_t, d)

    for l in range(depth):
        wl = w_in[l]
        w_main = jnp.concatenate(
            [wl[:, off[0]:off[4]], wl[:, off[9]:off[11]],
             _pair_heads(wl[:, off[6]:off[7]], 1, WIN_KV), _pair_heads(wl[:, off[11]:off[12]], 1, GA_KV),
             wl[:, off[7]:off[9]], wl[:, off[12]:off[14]]], axis=1).astype(BF16)
        w_lr = jnp.pad(wl[:, off[4]:off[6]], ((0, 0), (0, LANES - 2 * rank))).astype(BF16)
        gate_w = jnp.zeros((LANES, 2 * hk), F32)
        gate_w = gate_w.at[:rank, :hk].set(gla_gate_up[l, 0])
        gate_w = gate_w.at[rank:2 * rank, hk:].set(gla_gate_up[l, 1]).astype(BF16)
        gate_b = gla_gate_b[l].reshape(1, -1)
        mg = mix_g[l]
        ga, gc = mg[0:gw], mg[2 * gw:3 * gw]
        gb_ = _pair_heads(mg[gw:2 * gw], 0, WIN_KV)
        gd = _pair_heads(mg[3 * gw:], 0, GA_KV)
        wo = w_out[l]
        wo = jnp.concatenate([wo[0:gw], _pair_heads(wo[gw:2 * gw], 0, WIN_KV), wo[2 * gw:3 * gw],
                              _pair_heads(wo[3 * gw:], 0, GA_KV)], axis=0).astype(BF16)
        sink = _pair_heads(jnp.repeat(win_sink[l], HEAD_DIM), 0, WIN_KV)[None, :]
        qg = jnp.tile(qk_g[l, 0], n_heads)[None, :]
        kg = jnp.tile(qk_g[l, 1], WIN_KV)[None, :]
        bsb = jnp.repeat(cm_bs[l].T, gw // CM_GROUPS, axis=1)
        modl = mods[l]

        p, lr = _inproj(xs, modl, norm_g[l, 0][None, :], w_main, w_lr, seq_t=seq_t, ctx_len=ctx_len)
        p3 = p.reshape(bsz, seq_t, -1)
        lr3 = lr.reshape(bsz, seq_t, LANES)
        y_a = _gla(p3, lr3, gate_w, gate_b, ga[None, :], gla_consts, ctx_len=ctx_len, dk=dk, dv=dv)
        y_b, y_c, y_d = _mixers(p3, tabs, qg, kg, sink, gb_[None, :], gd[None, :],
                                cm_ln_g[l][None, :], cm_ln_b[l][None, :], cm_ws[l].astype(BF16), bsb,
                                gc[None, :], blocks=blocks, ctx_len=ctx_len)
        ys = [y.reshape(bsz * seq_t, gw) for y in (y_a, y_b, y_c, y_d)]
        xs = _outproj(xs, ys, modl, norm_g[l, 1][None, :], wo, seq_t=seq_t, ctx_len=ctx_len)
        xs = _ffn(xs, modl, norm_g[l, 2][None, :], norm_g[l, 3][None, :], w_ffn_in[l].astype(BF16),
                  w_ffn_out[l].astype(BF16), seq_t=seq_t, ctx_len=ctx_len)

    return xs.reshape(bsz, seq_t, d)[:, ctx_len:]
```

```python
import functools
import math

import numpy as np
import jax
import jax.numpy as jnp
from jax import lax
from jax.experimental import pallas as pl
from jax.experimental.pallas import tpu as pltpu

F32 = jnp.float32
BF16 = jnp.bfloat16

EPS = 1e-6
HEAD_DIM = 64
ATTN_SCALE = HEAD_DIM ** -0.5
ROPE_THETA = 10000.0
GRID_W = 64
GLA_HEADS = 4
GLA_TAU = 16.0
GLA_CHUNK = 64
WIN_KV = 2
WINDOW = 128
CM_GROUPS = 4
CM_CHUNK = 128
GA_KV = 2

LANES = 128
NEG_BIG = -1e30
V7X_VMEM_LIMIT = 56 * 1024 * 1024
ROW_CHUNK = 64


def _cparams(n_axes, vmem=V7X_VMEM_LIMIT):
    return pltpu.CompilerParams(dimension_semantics=("arbitrary",) * n_axes,
                                vmem_limit_bytes=vmem)


def _pick(total, candidates):
    for cand in candidates:
        if total % cand == 0:
            return cand
    raise ValueError(f"no tile in {candidates} divides {total}")


def _resident(shape):
    return pl.BlockSpec(shape, lambda *_: (0,) * len(shape), pipeline_mode=pl.Buffered(1))


def _dot(a, b):
    return jnp.dot(a, b, preferred_element_type=F32)


def _dot_nt(a, b):
    return lax.dot_general(a, b, (((1,), (1,)), ((), ())), preferred_element_type=F32)


def _rms(x, gain):
    return x * lax.rsqrt(jnp.mean(x * x, axis=-1, keepdims=True) + EPS) * gain


def _silu(x):
    return x * jax.nn.sigmoid(x)


def _gelu_tanh(x):
    return 0.5 * x * (1.0 + jnp.tanh(math.sqrt(2.0 / math.pi) * (x + 0.044715 * (x * x * x))))


def _mod_vec(mod_ref, k, chunk_is_ctx):
    return jnp.where(chunk_is_ctx, mod_ref[0, k:k + 1, :], mod_ref[0, 8 + k:9 + k, :])


def _row_loop(n_rows, step):
    n_chunks = n_rows // ROW_CHUNK
    lax.fori_loop(0, n_chunks, step, 0, unroll=3 if n_chunks % 3 == 0 else 2)


def _norm_modulate_rows(x_ref, h_ref, g_ref, mod_ref, k_shift, k_scale, row0, ctx_len):
    def step(c, carry):
        r = pl.multiple_of(c * ROW_CHUNK, ROW_CHUNK)
        is_ctx = row0 + r < ctx_len
        gain = g_ref[...] * (1.0 + _mod_vec(mod_ref, k_scale, is_ctx))
        x = x_ref[pl.ds(r, ROW_CHUNK), :]
        inv = lax.rsqrt(jnp.mean(x * x, axis=-1, keepdims=True) + EPS)
        h_ref[pl.ds(r, ROW_CHUNK), :] = (x * inv * gain + _mod_vec(mod_ref, k_shift, is_ctx)).astype(BF16)
        return carry

    _row_loop(x_ref.shape[0], step)


def _gated_residual_rows(x_ref, z_ref, o_ref, g_ref, mod_ref, k_gate, row0, ctx_len):
    def step(c, carry):
        r = pl.multiple_of(c * ROW_CHUNK, ROW_CHUNK)
        gain = g_ref[...] * _mod_vec(mod_ref, k_gate, row0 + r < ctx_len)
        z = z_ref[pl.ds(r, ROW_CHUNK), :]
        inv = lax.rsqrt(jnp.mean(z * z, axis=-1, keepdims=True) + EPS)
        o_ref[pl.ds(r, ROW_CHUNK), :] = x_ref[pl.ds(r, ROW_CHUNK), :] + z * inv * gain
        return carry

    _row_loop(x_ref.shape[0], step)


def _mod_kernel(c_ref, w_ref, b_ref, o_ref):
    s = _silu(c_ref[...]).astype(BF16)
    o_ref[0] = _dot(s, w_ref[0].astype(BF16)) + b_ref[0]


def _modulation(cond, w_mod, b_mod):
    depth, d, n = w_mod.shape
    rows = cond.shape[0]
    tn = _pick(n, (1024, 512, 256, 128))
    return pl.pallas_call(
        _mod_kernel,
        grid=(depth, n // tn),
        in_specs=[pl.BlockSpec((rows, d), lambda l, j: (0, 0)),
                  pl.BlockSpec((1, d, tn), lambda l, j: (l, 0, j)),
                  pl.BlockSpec((1, 1, tn), lambda l, j: (l, 0, j))],
        out_specs=pl.BlockSpec((1, rows, tn), lambda l, j: (l, 0, j)),
        out_shape=jax.ShapeDtypeStruct((depth, rows, n), F32),
        compiler_params=_cparams(2),
        name="modulation",
    )(cond, w_mod, b_mod.reshape(depth, 1, n))


def _inproj_kernel(x_ref, mod_ref, g_ref, w_ref, wlr_ref, p_ref, lr_ref, h_scr, *,
                   tm, tiles_per_batch, ctx_len):
    row0 = (pl.program_id(0) % tiles_per_batch) * tm
    _norm_modulate_rows(x_ref, h_scr, g_ref, mod_ref, 0, 1, row0, ctx_len)
    h = h_scr[...]
    lr_ref[...] = _dot(h, wlr_ref[...])
    p_ref[...] = _dot(h, w_ref[...])


def _inproj(xs, modl, gain, w_main, w_lr, *, seq_t, ctx_len):
    m, d = xs.shape
    n = w_main.shape[1]
    tm = _pick(seq_t, (576, 384, 128))
    tpb = seq_t // tm
    kern = functools.partial(_inproj_kernel, tm=tm, tiles_per_batch=tpb, ctx_len=ctx_len)
    return pl.pallas_call(
        kern,
        grid=(m // tm,),
        in_specs=[pl.BlockSpec((tm, d), lambda i: (i, 0)),
                  pl.BlockSpec((1, 16, d), lambda i: (i // tpb, 0, 0)),
                  _resident((1, d)), _resident((d, n)), _resident((d, LANES))],
        out_specs=[pl.BlockSpec((tm, n), lambda i: (i, 0)),
                   pl.BlockSpec((tm, LANES), lambda i: (i, 0))],
        out_shape=[jax.ShapeDtypeStruct((m, n), F32),
                   jax.ShapeDtypeStruct((m, LANES), F32)],
        scratch_shapes=[pltpu.VMEM((tm, d), BF16)],
        compiler_params=_cparams(1),
        name="inproj",
    )(xs, modl, gain, w_main, w_lr)


def _outproj_kernel(x_ref, ya_ref, yb_ref, yc_ref, yd_ref, mod_ref, g_ref, w_ref, o_ref, y_scr, *,
                    tm, tiles_per_batch, row_base, ctx_len, gw):
    i = pl.program_id(0)
    y_scr[:, 0 * gw:1 * gw] = ya_ref[...]
    y_scr[:, 1 * gw:2 * gw] = yb_ref[...]
    y_scr[:, 2 * gw:3 * gw] = yc_ref[...]
    y_scr[:, 3 * gw:4 * gw] = yd_ref[...]
    zn = _rms(_dot(y_scr[...], w_ref[...]), g_ref[...])
    row0 = row_base + (i % tiles_per_batch) * tm

    @pl.when(row0 >= ctx_len)
    def _():
        o_ref[...] = x_ref[...] + mod_ref[0, 10:11, :] * zn

    @pl.when(row0 < ctx_len)
    def _():
        is_ctx = row0 + lax.broadcasted_iota(jnp.int32, (tm, 1), 0) < ctx_len
        o_ref[...] = x_ref[...] + jnp.where(is_ctx, mod_ref[0, 2:3, :], mod_ref[0, 10:11, :]) * zn


def _outproj(xs, ys, modl, gain, w, *, seq_t, ctx_len, latent_only):
    m, d = xs.shape
    gw = ys[0].shape[1]
    if latent_only:
        seq = seq_t - ctx_len
        tm = _pick(seq, (512, 256, 128))
        tpb = seq // tm
        n_tiles = (m // seq_t) * tpb
        align = math.gcd(seq_t, ctx_len, tm)
        row_start = lambda i: pl.multiple_of((i // tpb) * seq_t + ctx_len + (i % tpb) * tm, align)
        xspec = pl.BlockSpec((pl.Element(tm), pl.Element(d)), lambda i: (row_start(i), 0))
        yspec = pl.BlockSpec((pl.Element(tm), pl.Element(gw)), lambda i: (row_start(i), 0))
        row_base = ctx_len
    else:
        tm = _pick(seq_t, (576, 384, 128))
        tpb = seq_t // tm
        n_tiles = m // tm
        xspec = pl.BlockSpec((tm, d), lambda i: (i, 0))
        yspec = pl.BlockSpec((tm, gw), lambda i: (i, 0))
        row_base = 0
    kern = functools.partial(_outproj_kernel, tm=tm, tiles_per_batch=tpb, row_base=row_base,
                             ctx_len=ctx_len, gw=gw)
    return pl.pallas_call(
        kern,
        grid=(n_tiles,),
        in_specs=[xspec, yspec, yspec, yspec, yspec,
                  pl.BlockSpec((1, 16, d), lambda i: (i // tpb, 0, 0)),
                  _resident((1, d)), _resident((4 * gw, d))],
        out_specs=pl.BlockSpec((tm, d), lambda i: (i, 0)),
        out_shape=jax.ShapeDtypeStruct((n_tiles * tm, d), F32),
        scratch_shapes=[pltpu.VMEM((tm, 4 * gw), BF16)],
        compiler_params=_cparams(1),
        name="outproj",
    )(xs, *ys, modl, gain, w)


FFN_TILE = 512


def _ffn_kernel(x_ref, mod_ref, g2_ref, g3_ref, wgu_ref, wo_ref, o_ref, h_scr, acc_scr, *,
                tm, tiles_per_batch, row_base, ctx_len, nf, tf):
    i = pl.program_id(0)
    j = pl.program_id(1)
    row0 = row_base + (i % tiles_per_batch) * tm

    @pl.when(j == 0)
    def _():
        _norm_modulate_rows(x_ref, h_scr, g2_ref, mod_ref, 3, 4, row0, ctx_len)
        acc_scr[...] = jnp.zeros_like(acc_scr)

    gu = _dot(h_scr[...], wgu_ref[0])
    act = (_silu(gu[:, :tf]) * gu[:, tf:]).astype(BF16)
    acc_scr[...] += _dot(act, wo_ref[...])

    @pl.when(j == nf - 1)
    def _():
        _gated_residual_rows(x_ref, acc_scr, o_ref, g3_ref, mod_ref, 5, row0, ctx_len)


def _ffn_weights(w_in, w_out):
    d, two_f = w_in.shape
    dff = two_f // 2
    nf = dff // FFN_TILE
    gate = w_in[:, :dff].reshape(d, nf, FFN_TILE)
    up = w_in[:, dff:].reshape(d, nf, FFN_TILE)
    wgu = jnp.transpose(jnp.concatenate([gate, up], axis=2), (1, 0, 2)).astype(BF16)
    return wgu, w_out.astype(BF16)


def _ffn(xs, modl, g2, g3, wgu, w_out, *, rows_per_batch, row_base, ctx_len):
    m, d = xs.shape
    nf, _, tf2 = wgu.shape
    tf = tf2 // 2
    tm = _pick(rows_per_batch, (768, 512, 384, 128))
    tpb = rows_per_batch // tm
    kern = functools.partial(_ffn_kernel, tm=tm, tiles_per_batch=tpb, row_base=row_base,
                             ctx_len=ctx_len, nf=nf, tf=tf)
    return pl.pallas_call(
        kern,
        grid=(m // tm, nf),
        in_specs=[pl.BlockSpec((tm, d), lambda i, j: (i, 0)),
                  pl.BlockSpec((1, 16, d), lambda i, j: (i // tpb, 0, 0)),
                  _resident((1, d)), _resident((1, d)),
                  pl.BlockSpec((1, d, tf2), lambda i, j: (j, 0, 0)),
                  pl.BlockSpec((tf, d), lambda i, j: (j, 0))],
        out_specs=pl.BlockSpec((tm, d), lambda i, j: (i, 0)),
        out_shape=jax.ShapeDtypeStruct((m, d), F32),
        scratch_shapes=[pltpu.VMEM((tm, d), BF16), pltpu.VMEM((tm, d), F32)],
        compiler_params=_cparams(2),
        name="ffn",
    )(xs, modl, g2, g3, wgu, w_out)


def _cmlp_tile(u_ref, v_ref, lng_ref, lnb_ref, ws_ref, bsb_ref, gc_ref, o_ref):
    tc, gw = u_ref.shape[1], u_ref.shape[2]
    cw = gw // CM_GROUPS
    u = _gelu_tanh(u_ref[0])
    v = _gelu_tanh(v_ref[0])
    vc = v - jnp.mean(v, axis=-1, keepdims=True)
    vn = vc * lax.rsqrt(jnp.mean(vc * vc, axis=-1, keepdims=True) + EPS) * lng_ref[...] + lnb_ref[...]
    vb = vn.astype(BF16)
    rows = []
    for ch in range(tc // CM_CHUNK):
        cols = []
        for g in range(CM_GROUPS):
            blk = vb[ch * CM_CHUNK:(ch + 1) * CM_CHUNK, g * cw:(g + 1) * cw]
            cols.append(_dot(ws_ref[g], blk) + bsb_ref[:, g * cw:(g + 1) * cw])
        rows.append(jnp.concatenate(cols, axis=1))
    s = jnp.concatenate(rows, axis=0)
    o_ref[0] = _rms(u * s, gc_ref[...]).astype(BF16)


def _group_mean_sq(x, ones_bd):
    sq = x * x
    hi = sq.astype(BF16)
    lo = (sq - hi.astype(F32)).astype(BF16)
    return (_dot(hi, ones_bd) + _dot(lo, ones_bd)) * (1.0 / HEAD_DIM)


def _rope(x, cos, sin_signed):
    w = x.shape[1]
    lane = lax.broadcasted_iota(jnp.int32, x.shape, 1)
    first = (lane % (HEAD_DIM // 2)) < (HEAD_DIM // 4)
    partner = jnp.where(first, pltpu.roll(x, w - HEAD_DIM // 4, 1), pltpu.roll(x, HEAD_DIM // 4, 1))
    return x * cos + partner * sin_signed


def _attn_tile(q, k_scr, v_scr, sink_ref, mg_ref, o_ref, *, mode, latent_tile, i, tq, seq_t, ctx_len):
    n_pairs = q.shape[1] // LANES
    lo = lax.broadcasted_iota(jnp.int32, (tq, LANES), 1) < HEAD_DIM
    lane1 = lax.broadcasted_iota(jnp.int32, (1, LANES), 1)
    row2 = lax.broadcasted_iota(jnp.int32, (2 * tq, 1), 0)

    key_sets = [(0, ctx_len, None)]
    if latent_tile and mode == "window":
        wk = tq + 2 * WINDOW
        start = pl.multiple_of(jnp.clip(i * tq - WINDOW, ctx_len, seq_t - wk), LANES)
        r = lax.broadcasted_iota(jnp.int32, (2 * tq, wk), 0)
        r = jnp.where(r >= tq, r - tq, r)
        c = lax.broadcasted_iota(jnp.int32, (2 * tq, wk), 1)
        dist = (i * tq + r) - (start + c)
        key_sets.append((start, wk, (jnp.abs(dist) <= WINDOW)))
    elif latent_tile:
        key_sets = [(0, seq_t, None)]

    outs = []
    for j in range(n_pairs):
        qp = q[:, j * LANES:(j + 1) * LANES]
        q2 = jnp.concatenate([jnp.where(lo, qp, 0.0), jnp.where(lo, 0.0, qp)], axis=0).astype(BF16)
        scores = []
        for k0, kn, valid in key_sets:
            s = _dot_nt(q2, k_scr[pl.ds(k0, kn), :])
            scores.append(s if valid is None else jnp.where(valid, s, NEG_BIG))
        mx = functools.reduce(jnp.maximum, [jnp.max(s, axis=1, keepdims=True) for s in scores])
        if mode == "window":
            sv = sink_ref[:, j * LANES:(j + 1) * LANES]
            s_lo = jnp.max(jnp.where(lane1 < HEAD_DIM, sv, NEG_BIG), axis=1, keepdims=True)
            s_hi = jnp.max(jnp.where(lane1 < HEAD_DIM, NEG_BIG, sv), axis=1, keepdims=True)
            sk = jnp.where(row2 < tq, s_lo, s_hi)
            mx = jnp.maximum(mx, sk)
            den = jnp.exp(sk - mx)
        else:
            den = 0.0
        o2 = 0.0
        for (k0, kn, _), s in zip(key_sets, scores):
            p = jnp.exp(s - mx)
            den = den + jnp.sum(p, axis=1, keepdims=True)
            o2 = o2 + _dot(p.astype(BF16), v_scr[pl.ds(k0, kn), :])
        o2 = o2 / den
        outs.append(jnp.where(lo, o2[:tq], o2[tq:]))
    o = jnp.concatenate(outs, axis=1)
    o_ref[0] = _rms(o, mg_ref[...]).astype(BF16)


def _mixers_kernel(bq_ref, dq_ref, cu_ref, cv_ref, bkv_ref, dkv_ref, cq_ref, sq_ref, ck_ref, sk_ref,
                   qg_ref, kg_ref, onesq_ref, onesk_ref, sink_ref, gb_ref, gd_ref,
                   lng_ref, lnb_ref, ws_ref, bsb_ref, gc_ref,
                   yb_ref, yc_ref, yd_ref, kb_scr, vb_scr, kd_scr, vd_scr, *, tq, seq_t, ctx_len):
    i = pl.program_id(1)
    n_ctx_tiles = ctx_len // tq

    @pl.when(i == 0)
    def _():
        kv = bkv_ref[0]
        kb_scr[...] = _rope(kv[:, :LANES], ck_ref[...], sk_ref[...]).astype(BF16)
        vb_scr[...] = kv[:, LANES:].astype(BF16)
        kv = dkv_ref[0]
        k = kv[:, :LANES]
        k = k * lax.rsqrt(_group_mean_sq(k, onesk_ref[...]) + EPS) * kg_ref[...]
        kd_scr[...] = _rope(k, ck_ref[...], sk_ref[...]).astype(BF16)
        vd_scr[...] = kv[:, LANES:].astype(BF16)

    qb = _rope(bq_ref[0], cq_ref[...], sq_ref[...]) * ATTN_SCALE
    qd = dq_ref[0]
    qd = qd * lax.rsqrt(_group_mean_sq(qd, onesq_ref[...]) + EPS) * qg_ref[...]
    qd = _rope(qd, cq_ref[...], sq_ref[...]) * ATTN_SCALE

    def attend(latent_tile):
        common = dict(latent_tile=latent_tile, i=i, tq=tq, seq_t=seq_t, ctx_len=ctx_len)
        _attn_tile(qb, kb_scr, vb_scr, sink_ref, gb_ref, yb_ref, mode="window", **common)
        _attn_tile(qd, kd_scr, vd_scr, None, gd_ref, yd_ref, mode="global", **common)

    pl.when(i < n_ctx_tiles)(functools.partial(attend, False))
    pl.when(i >= n_ctx_tiles)(functools.partial(attend, True))
    _cmlp_tile(cu_ref, cv_ref, lng_ref, lnb_ref, ws_ref, bsb_ref, gc_ref, yc_ref)


def _mixers(p, tabs, qg, kg, sink, gb, gd, lng, lnb, ws, bsb, gc, *, blocks, ctx_len):
    b, t, _ = p.shape
    gw = gb.shape[1]
    tq = _pick(t, (256, 128))
    assert ctx_len % tq == 0 and tq % CM_CHUNK == 0
    cq, sq, ck, sk, ones_q, ones_k = tabs
    kern = functools.partial(_mixers_kernel, tq=tq, seq_t=t, ctx_len=ctx_len)
    const = lambda a: _resident(a.shape)
    qspec = lambda blk: pl.BlockSpec((1, tq, gw), lambda bi, i: (bi, i, blk))
    kvspec = lambda blk: pl.BlockSpec((1, t, 2 * LANES), lambda bi, i: (bi, 0, blk))
    tabspec = pl.BlockSpec((tq, gw), lambda bi, i: (i, 0))
    out = jax.ShapeDtypeStruct((b, t, gw), BF16)
    ospec = pl.BlockSpec((1, tq, gw), lambda bi, i: (bi, i, 0))
    kvscr = pltpu.VMEM((t, LANES), BF16)
    return pl.pallas_call(
        kern,
        grid=(b, t // tq),
        in_specs=[qspec(blocks["b_q"]), qspec(blocks["d_q"]), qspec(blocks["c_u"]), qspec(blocks["c_v"]),
                  kvspec(blocks["b_kv"]), kvspec(blocks["d_kv"]),
                  tabspec, tabspec, const(ck), const(sk), const(qg), const(kg),
                  const(ones_q), const(ones_k), const(sink), const(gb), const(gd),
                  const(lng), const(lnb), const(ws), const(bsb), const(gc)],
        out_specs=[ospec, ospec, ospec],
        out_shape=[out, out, out],
        scratch_shapes=[kvscr, kvscr, kvscr, kvscr],
        compiler_params=_cparams(2),
        name="mixers_bcd",
    )(p, p, p, p, p, p, cq, sq, ck, sk, qg, kg, ones_q, ones_k, sink, gb, gd, lng, lnb, ws, bsb, gc)


GLA_LEVELS = int(math.log2(GLA_CHUNK))
GLA_ROW_PRE = GLA_LEVELS * GLA_CHUNK
GLA_ROW_SUF = GLA_ROW_PRE + GLA_CHUNK
GLA_ROW_LAST = GLA_ROW_SUF + GLA_CHUNK
GLA_ROWS = GLA_ROW_LAST + 16
GLA_STEPS_PER_ITER = 2


def _gla_constants(dk):
    n = GLA_CHUNK
    csum = np.zeros((2, GLA_ROWS, n), np.float32)
    qmask = np.zeros((2, GLA_LEVELS + 1, n, 1), np.float32)
    kmask = np.zeros((2, GLA_LEVELS + 1, n, 1), np.float32)
    smask = np.zeros((2, GLA_LEVELS + 1, n, n), np.float32)
    for d in range(2):
        tau = np.arange(n) if d == 0 else n - 1 - np.arange(n)
        qmask[d, 0] = 1.0
        kmask[d, 0] = 1.0
        smask[d, 0] = np.eye(n)
        for lv in range(1, GLA_LEVELS + 1):
            h = 1 << (lv - 1)
            blk = tau // (2 * h)
            upper = (tau % (2 * h)) >= h
            same = blk[:, None] == blk[None, :]
            both_up = upper[:, None] & upper[None, :]
            both_lo = (~upper[:, None]) & (~upper[None, :])
            c = np.where(upper[:, None],
                         same & both_up & (tau[None, :] <= tau[:, None]),
                         same & both_lo & (tau[None, :] > tau[:, None]))
            csum[d, (lv - 1) * n:lv * n] = c
            qmask[d, lv, :, 0] = upper
            kmask[d, lv, :, 0] = ~upper
            smask[d, lv] = same & upper[:, None] & (~upper[None, :])
        csum[d, GLA_ROW_PRE:GLA_ROW_SUF] = tau[None, :] <= tau[:, None]
        csum[d, GLA_ROW_SUF:GLA_ROW_LAST] = tau[None, :] > tau[:, None]
        csum[d, GLA_ROW_LAST:] = 1.0
    hk = GLA_HEADS * dk
    qmask = np.broadcast_to(qmask, (2, GLA_LEVELS + 1, n, hk)).copy()
    kmask = np.broadcast_to(kmask, (2, GLA_LEVELS + 1, n, hk)).copy()
    smask = np.tile(smask, (1, 1, 1, GLA_HEADS))
    return csum, qmask, kmask, smask


def _gla_kernel(pa_ref, lr_ref, gw_ref, gb_ref, gain_ref, cs_ref, qm_ref, km_ref, sm_ref,
                o_ref, la_scr, oacc_scr, st_scr, *, seq_t, ctx_len, dk, dv):
    hk = GLA_HEADS * dk
    hv = GLA_HEADS * dv
    n = GLA_CHUNK
    nc = seq_t // n
    nc_ctx = ctx_len // n

    z = _dot(lr_ref[0].astype(BF16), gw_ref[...]) + gb_ref[...]
    la_scr[...] = (jnp.minimum(z, 0.0) - jnp.log1p(jnp.exp(-jnp.abs(z)))) * (1.0 / GLA_TAU)
    oacc_scr[...] = jnp.zeros_like(oacc_scr)
    st_scr[...] = jnp.zeros_like(st_scr)

    lane_head = lax.broadcasted_iota(jnp.int32, (n, hk), 1) // dk
    head_lanes = [lane_head == h for h in range(GLA_HEADS)]

    def per_head_rows(x):
        return jnp.concatenate([jnp.where(m, x, 0.0) for m in head_lanes], axis=0).astype(BF16)

    steps = GLA_STEPS_PER_ITER
    assert nc % steps == 0
    chains = [(d, u) for u in range(steps) for d in (0, 1)]

    def scan_iter(it, carry):
        r0 = []
        for d, u in chains:
            s = it * steps + u
            c = s if d == 0 else jnp.where(s < nc_ctx, nc_ctx - 1 - s, nc - 1 - (s - nc_ctx))
            r0.append(pl.multiple_of(c * n, n))
        q = [pa_ref[0, pl.ds(r, n), 0:hk] * (dk ** -0.5) for r in r0]
        k = [pa_ref[0, pl.ds(r, n), hk:2 * hk] for r in r0]
        v = [pa_ref[0, pl.ds(r, n), 2 * hk:2 * hk + hv] for r in r0]
        ex2 = []
        for (d, _), r in zip(chains, r0):
            a = la_scr[pl.ds(r, n), d * hk:(d + 1) * hk]
            a_hi = a.astype(BF16)
            a_lo = (a - a_hi.astype(F32)).astype(BF16)
            ex2.append(_dot(cs_ref[d], jnp.concatenate([a_hi, a_lo], axis=1)))
        e = [jnp.exp(x[:, :hk] + x[:, hk:]) for x in ex2]

        scores = [jnp.zeros((n, GLA_HEADS * n), F32) for _ in chains]
        for lv in range(GLA_LEVELS + 1):
            for ci, (d, _) in enumerate(chains):
                if lv == 0:
                    ql, kl = q[ci], k[ci]
                else:
                    el = e[ci][(lv - 1) * n:lv * n]
                    ql = q[ci] * el * qm_ref[d, lv]
                    kl = k[ci] * el * km_ref[d, lv]
                scores[ci] = scores[ci] + _dot_nt(ql.astype(BF16), per_head_rows(kl)) * sm_ref[d, lv]

        kt = []
        for ci in range(len(chains)):
            kh = k[ci] * e[ci][GLA_ROW_SUF:GLA_ROW_LAST]
            e_last = e[ci][GLA_ROW_LAST:GLA_ROW_LAST + 1]
            kt.append(jnp.concatenate([kh, jnp.broadcast_to(e_last, (n, hk))], axis=0).T)
        dstate = []
        o_intra = []
        for ci in range(len(chains)):
            vpad = jnp.concatenate([v[ci], jnp.zeros_like(v[ci])], axis=0).astype(BF16)
            full = _dot(kt[ci].astype(BF16), vpad)
            dstate.append(jnp.concatenate(
                [full[h * dk:(h + 1) * dk, h * dv:(h + 1) * dv] for h in range(GLA_HEADS)], axis=0))
            v_rows = jnp.concatenate([v[ci][:, h * dv:(h + 1) * dv] for h in range(GLA_HEADS)],
                                     axis=0).astype(BF16)
            o_intra.append(_dot(per_head_rows(scores[ci]), v_rows))

        st = [st_scr[0], st_scr[1]]
        for ci, (d, _) in enumerate(chains):
            qh = q[ci] * e[ci][GLA_ROW_PRE:GLA_ROW_SUF]
            o_rows = o_intra[ci] + _dot(per_head_rows(qh), st[d].astype(BF16))
            oacc_scr[pl.ds(r0[ci], n), :] += jnp.concatenate(
                [o_rows[h * n:(h + 1) * n, :] for h in range(GLA_HEADS)], axis=1)
            st[d] = st[d] * kt[ci][:, n:n + 1] + dstate[ci]
        st_scr[0] = st[0]
        st_scr[1] = st[1]
        return carry

    lax.fori_loop(0, nc // steps, scan_iter, 0)

    tfin = _pick(seq_t, (256, 128))
    for t0 in range(0, seq_t, tfin):
        o = oacc_scr[t0:t0 + tfin, :]
        parts = [_rms(o[:, h * dv:(h + 1) * dv], gain_ref[:, h * dv:(h + 1) * dv])
                 for h in range(GLA_HEADS)]
        gate = pa_ref[0, t0:t0 + tfin, 2 * hk + hv:2 * hk + 2 * hv]
        o_ref[0, t0:t0 + tfin, :] = (jnp.concatenate(parts, axis=1) * _silu(gate)).astype(BF16)


def _gla(p, lr, gw, gb, gain, consts, *, ctx_len, dk, dv):
    b, t, _ = p.shape
    hk = GLA_HEADS * dk
    hv = GLA_HEADS * dv
    pa_w = 2 * hk + 2 * hv
    cs, qm, km, sm = consts
    kern = functools.partial(_gla_kernel, seq_t=t, ctx_len=ctx_len, dk=dk, dv=dv)
    const = lambda a: _resident(a.shape)
    return pl.pallas_call(
        kern,
        grid=(b,),
        in_specs=[pl.BlockSpec((1, t, pa_w), lambda bi: (bi, 0, 0)),
                  pl.BlockSpec((1, t, LANES), lambda bi: (bi, 0, 0)),
                  const(gw), const(gb), const(gain), const(cs), const(qm), const(km), const(sm)],
        out_specs=pl.BlockSpec((1, t, hv), lambda bi: (bi, 0, 0)),
        out_shape=jax.ShapeDtypeStruct((b, t, hv), BF16),
        scratch_shapes=[pltpu.VMEM((t, 2 * hk), F32), pltpu.VMEM((t, hv), F32),
                        pltpu.VMEM((2, hk, dv), F32)],
        compiler_params=_cparams(1),
        name="gla",
    )(p, lr, gw, gb, gain, cs, qm, km, sm)


def _rope_tables(length, ctx_len, n_heads):
    rows = length // GRID_W
    row = jnp.repeat(jnp.arange(rows), GRID_W).astype(F32)
    col = jnp.tile(jnp.arange(GRID_W), rows).astype(F32)
    quarter = HEAD_DIM // 4
    inv = ROPE_THETA ** (-jnp.arange(quarter, dtype=F32) / quarter)
    ar, ac = row[:, None] * inv, col[:, None] * inv
    ang = jnp.concatenate([ar, ar, ac, ac], axis=-1)
    cos, sin = jnp.cos(ang), jnp.sin(ang)
    first = (jnp.arange(HEAD_DIM) % (HEAD_DIM // 2)) < quarter
    sin_signed = jnp.where(first[None, :], -sin, sin)
    cos = jnp.concatenate([jnp.ones((ctx_len, HEAD_DIM), F32), cos], axis=0)
    sin_signed = jnp.concatenate([jnp.zeros((ctx_len, HEAD_DIM), F32), sin_signed], axis=0)
    return jnp.tile(cos, (1, n_heads)), jnp.tile(sin_signed, (1, n_heads))


def _pair_heads(a, axis, n_kv):
    axis = axis % a.ndim
    g = a.shape[axis] // (n_kv * HEAD_DIM)
    shape = a.shape[:axis] + (n_kv, g, HEAD_DIM) + a.shape[axis + 1:]
    return jnp.swapaxes(a.reshape(shape), axis, axis + 1).reshape(a.shape)


def _block_diag_mask(rows, cols, rb, cb):
    r = np.arange(rows)[:, None] // rb
    c = np.arange(cols)[None, :] // cb
    return (r == c).astype(np.float32)


def kernel(x, c, ctx, c_ctx, w_mod, b_mod, norm_g, w_in, gla_gate_up, gla_gate_b, win_sink, cm_ln_g,
           cm_ln_b, cm_ws, cm_bs, qk_g, mix_g, w_out, w_ffn_in, w_ffn_out):
    bsz, seq, d = x.shape
    ctx_len = ctx.shape[1]
    seq_t = ctx_len + seq
    depth = w_mod.shape[0]
    gw = d // 4
    rank = gla_gate_up.shape[2]
    dk = gla_gate_up.shape[3] // GLA_HEADS
    dv = gw // GLA_HEADS
    n_heads = gw // HEAD_DIM
    kvw = WIN_KV * HEAD_DIM
    hk = GLA_HEADS * dk
    assert kvw == LANES and GA_KV == WIN_KV and gw % LANES == 0 and 2 * rank <= LANES
    assert dk == GLA_CHUNK and 2 * hk == gw
    assert ctx_len % CM_CHUNK == 0 and seq % CM_CHUNK == 0 and seq % GRID_W == 0
    assert ctx_len % ROW_CHUNK == 0

    sizes = [hk, hk, gw, gw, rank, rank, gw, kvw, kvw, gw, gw, gw, kvw, kvw]
    off = [int(v) for v in np.concatenate([[0], np.cumsum(sizes)])]
    blocks = {"c_u": 3, "c_v": 4, "b_q": 5, "d_q": 6,
              "b_kv": (7 * gw) // (2 * LANES), "d_kv": (7 * gw) // (2 * LANES) + 1}

    cos_q, sin_q = _rope_tables(seq, ctx_len, n_heads)
    cos_k, sin_k = cos_q[:, :LANES], sin_q[:, :LANES]
    ones_q = jnp.asarray(_block_diag_mask(gw, gw, HEAD_DIM, HEAD_DIM), BF16)
    ones_k = ones_q[:LANES, :LANES]
    tabs = (cos_q, sin_q, cos_k, sin_k, ones_q, ones_k)

    csum, qmask, kmask, smask = _gla_constants(dk)
    gla_consts = (jnp.asarray(csum, BF16), jnp.asarray(qmask), jnp.asarray(kmask), jnp.asarray(smask))

    rows = 16
    cond = jnp.concatenate([c, c_ctx[None, :], jnp.zeros((rows - bsz - 1, d), F32)], axis=0)
    mod_all = _modulation(cond, w_mod, b_mod)
    mod_lat = mod_all[:, :bsz].reshape(depth, bsz, 6, d)
    mod_ctx = jnp.broadcast_to(mod_all[:, bsz].reshape(depth, 1, 6, d), (depth, bsz, 6, d))
    pad2 = jnp.zeros((depth, bsz, 2, d), F32)
    mods = jnp.concatenate([mod_ctx, pad2, mod_lat, pad2], axis=2)

    xs = jnp.concatenate([ctx, x], axis=1).reshape(bsz * seq_t, d)

    for l in range(depth):
        wl = w_in[l]
        w_main = jnp.concatenate(
            [wl[:, off[0]:off[4]], wl[:, off[9]:off[11]],
             _pair_heads(wl[:, off[6]:off[7]], 1, WIN_KV), _pair_heads(wl[:, off[11]:off[12]], 1, GA_KV),
             wl[:, off[7]:off[9]], wl[:, off[12]:off[14]]], axis=1).astype(BF16)
        w_lr = jnp.pad(wl[:, off[4]:off[6]], ((0, 0), (0, LANES - 2 * rank))).astype(BF16)
        gate_w = jnp.zeros((LANES, 2 * hk), F32)
        gate_w = gate_w.at[:rank, :hk].set(gla_gate_up[l, 0])
        gate_w = gate_w.at[rank:2 * rank, hk:].set(gla_gate_up[l, 1]).astype(BF16)
        gate_b = gla_gate_b[l].reshape(1, -1)
        mg = mix_g[l]
        ga, gc = mg[0:gw], mg[2 * gw:3 * gw]
        gb_ = _pair_heads(mg[gw:2 * gw], 0, WIN_KV)
        gd = _pair_heads(mg[3 * gw:], 0, GA_KV)
        wo = w_out[l]
        wo = jnp.concatenate([wo[0:gw], _pair_heads(wo[gw:2 * gw], 0, WIN_KV), wo[2 * gw:3 * gw],
                              _pair_heads(wo[3 * gw:], 0, GA_KV)], axis=0).astype(BF16)
        sink = _pair_heads(jnp.repeat(win_sink[l], HEAD_DIM), 0, WIN_KV)[None, :]
        qg = jnp.tile(qk_g[l, 0], n_heads)[None, :]
        kg = jnp.tile(qk_g[l, 1], WIN_KV)[None, :]
        bsb = jnp.repeat(cm_bs[l].T, gw // CM_GROUPS, axis=1)
        modl = mods[l]

        p, lr = _inproj(xs, modl, norm_g[l, 0][None, :], w_main, w_lr, seq_t=seq_t, ctx_len=ctx_len)
        p3 = p.reshape(bsz, seq_t, -1)
        lr3 = lr.reshape(bsz, seq_t, LANES)
        y_a = _gla(p3, lr3, gate_w, gate_b, ga[None, :], gla_consts, ctx_len=ctx_len, dk=dk, dv=dv)
        y_b, y_c, y_d = _mixers(p3, tabs, qg, kg, sink, gb_[None, :], gd[None, :],
                                cm_ln_g[l][None, :], cm_ln_b[l][None, :], cm_ws[l].astype(BF16), bsb,
                                gc[None, :], blocks=blocks, ctx_len=ctx_len)
        ys = [y.reshape(bsz * seq_t, gw) for y in (y_a, y_b, y_c, y_d)]
        last = l == depth - 1
        xs = _outproj(xs, ys, modl, norm_g[l, 1][None, :], wo, seq_t=seq_t, ctx_len=ctx_len,
                      latent_only=last)
        wgu, wdown = _ffn_weights(w_ffn_in[l], w_ffn_out[l])
        xs = _ffn(xs, modl, norm_g[l, 2][None, :], norm_g[l, 3][None, :], wgu, wdown,
                  rows_per_batch=seq if last else seq_t, row_base=ctx_len if last else 0,
                  ctx_len=ctx_len)

    return xs.reshape(bsz, seq, d)
```

```python
import functools
import math

import numpy as np
import jax
import jax.numpy as jnp
from jax import lax
from jax.experimental import pallas as pl
from jax.experimental.pallas import tpu as pltpu

F32 = jnp.float32
BF16 = jnp.bfloat16

EPS = 1e-6
HEAD_DIM = 64
ATTN_SCALE = HEAD_DIM ** -0.5
ROPE_THETA = 10000.0
GRID_W = 64
GLA_HEADS = 4
GLA_TAU = 16.0
GLA_CHUNK = 64
WIN_KV = 2
WINDOW = 128
CM_GROUPS = 4
CM_CHUNK = 128
GA_KV = 2

LANES = 128
NEG_BIG = -1e30
V7X_VMEM_LIMIT = 56 * 1024 * 1024
ROW_CHUNK = 64


def _cparams(n_axes, vmem=V7X_VMEM_LIMIT, flags=None):
    return pltpu.CompilerParams(dimension_semantics=("arbitrary",) * n_axes,
                                vmem_limit_bytes=vmem, flags=flags)


def _pick(total, candidates):
    for cand in candidates:
        if total % cand == 0:
            return cand
    raise ValueError(f"no tile in {candidates} divides {total}")


def _resident(shape):
    return pl.BlockSpec(shape, lambda *_: (0,) * len(shape), pipeline_mode=pl.Buffered(1))


def _dot(a, b):
    return jnp.dot(a, b, preferred_element_type=F32)


def _dot_nt(a, b):
    return lax.dot_general(a, b, (((1,), (1,)), ((), ())), preferred_element_type=F32)


def _rms(x, gain):
    return x * lax.rsqrt(jnp.mean(x * x, axis=-1, keepdims=True) + EPS) * gain


def _silu(x):
    return x * jax.nn.sigmoid(x)


def _gelu_tanh(x):
    return 0.5 * x * (1.0 + jnp.tanh(math.sqrt(2.0 / math.pi) * (x + 0.044715 * (x * x * x))))


def _mod_vec(mod_ref, k, chunk_is_ctx):
    return jnp.where(chunk_is_ctx, mod_ref[0, k:k + 1, :], mod_ref[0, 8 + k:9 + k, :])


def _row_loop(n_rows, step):
    n_chunks = n_rows // ROW_CHUNK
    lax.fori_loop(0, n_chunks, step, 0, unroll=3 if n_chunks % 3 == 0 else 2)


def _norm_modulate_rows(x_ref, h_ref, g_ref, mod_ref, k_shift, k_scale, row0, ctx_len):
    def step(c, carry):
        r = pl.multiple_of(c * ROW_CHUNK, ROW_CHUNK)
        is_ctx = row0 + r < ctx_len
        gain = g_ref[...] * (1.0 + _mod_vec(mod_ref, k_scale, is_ctx))
        x = x_ref[pl.ds(r, ROW_CHUNK), :]
        inv = lax.rsqrt(jnp.mean(x * x, axis=-1, keepdims=True) + EPS)
        h_ref[pl.ds(r, ROW_CHUNK), :] = (x * inv * gain + _mod_vec(mod_ref, k_shift, is_ctx)).astype(BF16)
        return carry

    _row_loop(x_ref.shape[0], step)


def _gated_residual_rows(x_ref, z_ref, o_ref, g_ref, mod_ref, k_gate, row0, ctx_len):
    def step(c, carry):
        r = pl.multiple_of(c * ROW_CHUNK, ROW_CHUNK)
        gain = g_ref[...] * _mod_vec(mod_ref, k_gate, row0 + r < ctx_len)
        z = z_ref[pl.ds(r, ROW_CHUNK), :]
        inv = lax.rsqrt(jnp.mean(z * z, axis=-1, keepdims=True) + EPS)
        o_ref[pl.ds(r, ROW_CHUNK), :] = x_ref[pl.ds(r, ROW_CHUNK), :] + z * inv * gain
        return carry

    _row_loop(x_ref.shape[0], step)


def _mod_kernel(c_ref, w_ref, b_ref, o_ref):
    s = _silu(c_ref[...]).astype(BF16)
    o_ref[0] = _dot(s, w_ref[0].astype(BF16)) + b_ref[0]


def _modulation(cond, w_mod, b_mod):
    depth, d, n = w_mod.shape
    rows = cond.shape[0]
    tn = _pick(n, (1024, 512, 256, 128))
    return pl.pallas_call(
        _mod_kernel,
        grid=(depth, n // tn),
        in_specs=[pl.BlockSpec((rows, d), lambda l, j: (0, 0)),
                  pl.BlockSpec((1, d, tn), lambda l, j: (l, 0, j)),
                  pl.BlockSpec((1, 1, tn), lambda l, j: (l, 0, j))],
        out_specs=pl.BlockSpec((1, rows, tn), lambda l, j: (l, 0, j)),
        out_shape=jax.ShapeDtypeStruct((depth, rows, n), F32),
        compiler_params=_cparams(2),
        name="modulation",
    )(cond, w_mod, b_mod.reshape(depth, 1, n))


def _inproj_kernel(x_ref, mod_ref, g_ref, w_ref, wlr_ref, p_ref, lr_ref, h_scr, *,
                   tm, tiles_per_batch, ctx_len):
    row0 = (pl.program_id(0) % tiles_per_batch) * tm
    _norm_modulate_rows(x_ref, h_scr, g_ref, mod_ref, 0, 1, row0, ctx_len)
    h = h_scr[...]
    lr_ref[...] = _dot(h, wlr_ref[...])
    p_ref[...] = _dot(h, w_ref[...])


def _inproj(xs, modl, gain, w_main, w_lr, *, seq_t, ctx_len):
    m, d = xs.shape
    n = w_main.shape[1]
    tm = _pick(seq_t, (576, 384, 128))
    tpb = seq_t // tm
    kern = functools.partial(_inproj_kernel, tm=tm, tiles_per_batch=tpb, ctx_len=ctx_len)
    return pl.pallas_call(
        kern,
        grid=(m // tm,),
        in_specs=[pl.BlockSpec((tm, d), lambda i: (i, 0)),
                  pl.BlockSpec((1, 16, d), lambda i: (i // tpb, 0, 0)),
                  _resident((1, d)), _resident((d, n)), _resident((d, LANES))],
        out_specs=[pl.BlockSpec((tm, n), lambda i: (i, 0)),
                   pl.BlockSpec((tm, LANES), lambda i: (i, 0))],
        out_shape=[jax.ShapeDtypeStruct((m, n), F32),
                   jax.ShapeDtypeStruct((m, LANES), F32)],
        scratch_shapes=[pltpu.VMEM((tm, d), BF16)],
        compiler_params=_cparams(1),
        name="inproj",
    )(xs, modl, gain, w_main, w_lr)


def _outproj_kernel(x_ref, ya_ref, yb_ref, yc_ref, yd_ref, mod_ref, g_ref, w_ref, o_ref, y_scr, *,
                    tm, tiles_per_batch, row_base, ctx_len, gw):
    i = pl.program_id(0)
    y_scr[:, 0 * gw:1 * gw] = ya_ref[...]
    y_scr[:, 1 * gw:2 * gw] = yb_ref[...]
    y_scr[:, 2 * gw:3 * gw] = yc_ref[...]
    y_scr[:, 3 * gw:4 * gw] = yd_ref[...]
    zn = _rms(_dot(y_scr[...], w_ref[...]), g_ref[...])
    row0 = row_base + (i % tiles_per_batch) * tm

    @pl.when(row0 >= ctx_len)
    def _():
        o_ref[...] = x_ref[...] + mod_ref[0, 10:11, :] * zn

    @pl.when(row0 < ctx_len)
    def _():
        is_ctx = row0 + lax.broadcasted_iota(jnp.int32, (tm, 1), 0) < ctx_len
        o_ref[...] = x_ref[...] + jnp.where(is_ctx, mod_ref[0, 2:3, :], mod_ref[0, 10:11, :]) * zn


def _outproj(xs, ys, modl, gain, w, *, seq_t, ctx_len, latent_only):
    m, d = xs.shape
    gw = ys[0].shape[1]
    if latent_only:
        seq = seq_t - ctx_len
        tm = _pick(seq, (512, 256, 128))
        tpb = seq // tm
        n_tiles = (m // seq_t) * tpb
        align = math.gcd(seq_t, ctx_len, tm)
        row_start = lambda i: pl.multiple_of((i // tpb) * seq_t + ctx_len + (i % tpb) * tm, align)
        xspec = pl.BlockSpec((pl.Element(tm), pl.Element(d)), lambda i: (row_start(i), 0))
        yspec = pl.BlockSpec((pl.Element(tm), pl.Element(gw)), lambda i: (row_start(i), 0))
        row_base = ctx_len
    else:
        tm = _pick(seq_t, (576, 384, 128))
        tpb = seq_t // tm
        n_tiles = m // tm
        xspec = pl.BlockSpec((tm, d), lambda i: (i, 0))
        yspec = pl.BlockSpec((tm, gw), lambda i: (i, 0))
        row_base = 0
    kern = functools.partial(_outproj_kernel, tm=tm, tiles_per_batch=tpb, row_base=row_base,
                             ctx_len=ctx_len, gw=gw)
    return pl.pallas_call(
        kern,
        grid=(n_tiles,),
        in_specs=[xspec, yspec, yspec, yspec, yspec,
                  pl.BlockSpec((1, 16, d), lambda i: (i // tpb, 0, 0)),
                  _resident((1, d)), _resident((4 * gw, d))],
        out_specs=pl.BlockSpec((tm, d), lambda i: (i, 0)),
        out_shape=jax.ShapeDtypeStruct((n_tiles * tm, d), F32),
        scratch_shapes=[pltpu.VMEM((tm, 4 * gw), BF16)],
        compiler_params=_cparams(1),
        name="outproj",
    )(xs, *ys, modl, gain, w)


def _ffn_kernel(x_ref, mod_ref, g2_ref, g3_ref, wg_ref, wu_ref, wo_ref, o_ref, h_scr, acc_scr, *,
                tm, tiles_per_batch, row_base, ctx_len, nf):
    i = pl.program_id(0)
    j = pl.program_id(1)
    row0 = row_base + (i % tiles_per_batch) * tm

    @pl.when(j == 0)
    def _():
        _norm_modulate_rows(x_ref, h_scr, g2_ref, mod_ref, 3, 4, row0, ctx_len)
        acc_scr[...] = jnp.zeros_like(acc_scr)

    h = h_scr[...]
    act = (_silu(_dot(h, wg_ref[...])) * _dot(h, wu_ref[...])).astype(BF16)
    acc_scr[...] += _dot(act, wo_ref[...])

    @pl.when(j == nf - 1)
    def _():
        _gated_residual_rows(x_ref, acc_scr, o_ref, g3_ref, mod_ref, 5, row0, ctx_len)


def _ffn(xs, modl, g2, g3, w_in, w_out, *, rows_per_batch, row_base, ctx_len):
    m, d = xs.shape
    dff = w_out.shape[0]
    tm = _pick(rows_per_batch, (768, 512, 384, 128))
    tf = _pick(dff, (512, 256, 128))
    nf = dff // tf
    tpb = rows_per_batch // tm
    kern = functools.partial(_ffn_kernel, tm=tm, tiles_per_batch=tpb, row_base=row_base,
                             ctx_len=ctx_len, nf=nf)
    return pl.pallas_call(
        kern,
        grid=(m // tm, nf),
        in_specs=[pl.BlockSpec((tm, d), lambda i, j: (i, 0)),
                  pl.BlockSpec((1, 16, d), lambda i, j: (i // tpb, 0, 0)),
                  _resident((1, d)), _resident((1, d)),
                  pl.BlockSpec((d, tf), lambda i, j: (0, j)),
                  pl.BlockSpec((d, tf), lambda i, j: (0, nf + j)),
                  pl.BlockSpec((tf, d), lambda i, j: (j, 0))],
        out_specs=pl.BlockSpec((tm, d), lambda i, j: (i, 0)),
        out_shape=jax.ShapeDtypeStruct((m, d), F32),
        scratch_shapes=[pltpu.VMEM((tm, d), BF16), pltpu.VMEM((tm, d), F32)],
        compiler_params=_cparams(2),
        name="ffn",
    )(xs, modl, g2, g3, w_in, w_in, w_out)


def _cmlp_tile(u_ref, v_ref, lng_ref, lnb_ref, ws_ref, bsb_ref, gc_ref, o_ref):
    tc, gw = u_ref.shape[1], u_ref.shape[2]
    cw = gw // CM_GROUPS
    u = _gelu_tanh(u_ref[0])
    v = _gelu_tanh(v_ref[0])
    vc = v - jnp.mean(v, axis=-1, keepdims=True)
    vn = vc * lax.rsqrt(jnp.mean(vc * vc, axis=-1, keepdims=True) + EPS) * lng_ref[...] + lnb_ref[...]
    vb = vn.astype(BF16)
    rows = []
    for ch in range(tc // CM_CHUNK):
        cols = []
        for g in range(CM_GROUPS):
            blk = vb[ch * CM_CHUNK:(ch + 1) * CM_CHUNK, g * cw:(g + 1) * cw]
            cols.append(_dot(ws_ref[g], blk) + bsb_ref[:, g * cw:(g + 1) * cw])
        rows.append(jnp.concatenate(cols, axis=1))
    s = jnp.concatenate(rows, axis=0)
    o_ref[0] = _rms(u * s, gc_ref[...]).astype(BF16)


def _group_mean_sq(x, ones_bd):
    sq = x * x
    hi = sq.astype(BF16)
    lo = (sq - hi.astype(F32)).astype(BF16)
    return (_dot(hi, ones_bd) + _dot(lo, ones_bd)) * (1.0 / HEAD_DIM)


def _rope(x, cos, sin_signed):
    w = x.shape[1]
    lane = lax.broadcasted_iota(jnp.int32, x.shape, 1)
    first = (lane % (HEAD_DIM // 2)) < (HEAD_DIM // 4)
    partner = jnp.where(first, pltpu.roll(x, w - HEAD_DIM // 4, 1), pltpu.roll(x, HEAD_DIM // 4, 1))
    return x * cos + partner * sin_signed


def _attn_pairs(q, k_scr, v_scr, sink_ref, pairs, *, mode, latent_tile, i, tq, seq_t, ctx_len):
    lo = lax.broadcasted_iota(jnp.int32, (tq, LANES), 1) < HEAD_DIM
    lane1 = lax.broadcasted_iota(jnp.int32, (1, LANES), 1)
    row2 = lax.broadcasted_iota(jnp.int32, (2 * tq, 1), 0)

    key_sets = [(0, ctx_len, None)]
    if latent_tile and mode == "window":
        wk = tq + 2 * WINDOW
        start = pl.multiple_of(jnp.clip(i * tq - WINDOW, ctx_len, seq_t - wk), LANES)
        r = lax.broadcasted_iota(jnp.int32, (2 * tq, wk), 0)
        r = jnp.where(r >= tq, r - tq, r)
        c = lax.broadcasted_iota(jnp.int32, (2 * tq, wk), 1)
        dist = (i * tq + r) - (start + c)
        key_sets.append((start, wk, (jnp.abs(dist) <= WINDOW)))
    elif latent_tile:
        key_sets = [(0, seq_t, None)]

    outs = []
    for j in pairs:
        qp = q[:, j * LANES:(j + 1) * LANES]
        q2 = jnp.concatenate([jnp.where(lo, qp, 0.0), jnp.where(lo, 0.0, qp)], axis=0).astype(BF16)
        scores = []
        for k0, kn, valid in key_sets:
            s = _dot_nt(q2, k_scr[pl.ds(k0, kn), :])
            scores.append(s if valid is None else jnp.where(valid, s, NEG_BIG))
        mx = functools.reduce(jnp.maximum, [jnp.max(s, axis=1, keepdims=True) for s in scores])
        if mode == "window":
            sv = sink_ref[:, j * LANES:(j + 1) * LANES]
            s_lo = jnp.max(jnp.where(lane1 < HEAD_DIM, sv, NEG_BIG), axis=1, keepdims=True)
            s_hi = jnp.max(jnp.where(lane1 < HEAD_DIM, NEG_BIG, sv), axis=1, keepdims=True)
            sk = jnp.where(row2 < tq, s_lo, s_hi)
            mx = jnp.maximum(mx, sk)
            den = jnp.exp(sk - mx)
        else:
            den = 0.0
        o2 = 0.0
        for (k0, kn, _), s in zip(key_sets, scores):
            p = jnp.exp(s - mx)
            den = den + jnp.sum(p, axis=1, keepdims=True)
            o2 = o2 + _dot(p.astype(BF16), v_scr[pl.ds(k0, kn), :])
        o2 = o2 / den
        outs.append(jnp.where(lo, o2[:tq], o2[tq:]))
    return outs


def _mixers_kernel(bq_ref, dq_ref, cu_ref, cv_ref, bkv_ref, dkv_ref, cq_ref, sq_ref, ck_ref, sk_ref,
                   qg_ref, kg_ref, onesq_ref, onesk_ref, sink_ref, gb_ref, gd_ref,
                   lng_ref, lnb_ref, ws_ref, bsb_ref, gc_ref,
                   yb_ref, yc_ref, yd_ref, kb_scr, vb_scr, kd_scr, vd_scr, *, tq, seq_t, ctx_len):
    i = pl.program_id(1)
    n_ctx_tiles = ctx_len // tq

    @pl.when(i == 0)
    def _():
        kv = bkv_ref[0]
        kb_scr[...] = _rope(kv[:, :LANES], ck_ref[...], sk_ref[...]).astype(BF16)
        vb_scr[...] = kv[:, LANES:].astype(BF16)
        kv = dkv_ref[0]
        k = kv[:, :LANES]
        k = k * lax.rsqrt(_group_mean_sq(k, onesk_ref[...]) + EPS) * kg_ref[...]
        kd_scr[...] = _rope(k, ck_ref[...], sk_ref[...]).astype(BF16)
        vd_scr[...] = kv[:, LANES:].astype(BF16)

    qb = _rope(bq_ref[0], cq_ref[...], sq_ref[...]) * ATTN_SCALE
    qd = dq_ref[0]
    qd = qd * lax.rsqrt(_group_mean_sq(qd, onesq_ref[...]) + EPS) * qg_ref[...]
    qd = _rope(qd, cq_ref[...], sq_ref[...]) * ATTN_SCALE

    n_pairs = qb.shape[1] // LANES
    half = n_pairs // 2

    def attend(latent_tile):
        common = dict(latent_tile=latent_tile, i=i, tq=tq, seq_t=seq_t, ctx_len=ctx_len)
        win = functools.partial(_attn_pairs, qb, kb_scr, vb_scr, sink_ref, mode="window", **common)
        ob = win(range(half))
        od = _attn_pairs(qd, kd_scr, vd_scr, None, range(n_pairs), mode="global", **common)
        ob = ob + win(range(half, n_pairs))
        yb_ref[0] = _rms(jnp.concatenate(ob, axis=1), gb_ref[...]).astype(BF16)
        yd_ref[0] = _rms(jnp.concatenate(od, axis=1), gd_ref[...]).astype(BF16)

    pl.when(i < n_ctx_tiles)(functools.partial(attend, False))
    pl.when(i >= n_ctx_tiles)(functools.partial(attend, True))
    _cmlp_tile(cu_ref, cv_ref, lng_ref, lnb_ref, ws_ref, bsb_ref, gc_ref, yc_ref)


def _mixers(p, tabs, qg, kg, sink, gb, gd, lng, lnb, ws, bsb, gc, *, blocks, ctx_len):
    b, t, _ = p.shape
    gw = gb.shape[1]
    tq = _pick(t, (256, 128))
    assert ctx_len % tq == 0 and tq % CM_CHUNK == 0
    cq, sq, ck, sk, ones_q, ones_k = tabs
    kern = functools.partial(_mixers_kernel, tq=tq, seq_t=t, ctx_len=ctx_len)
    const = lambda a: _resident(a.shape)
    qspec = lambda blk: pl.BlockSpec((1, tq, gw), lambda bi, i: (bi, i, blk))
    kvspec = lambda blk: pl.BlockSpec((1, t, 2 * LANES), lambda bi, i: (bi, 0, blk))
    tabspec = pl.BlockSpec((tq, gw), lambda bi, i: (i, 0))
    out = jax.ShapeDtypeStruct((b, t, gw), BF16)
    ospec = pl.BlockSpec((1, tq, gw), lambda bi, i: (bi, i, 0))
    kvscr = pltpu.VMEM((t, LANES), BF16)
    return pl.pallas_call(
        kern,
        grid=(b, t // tq),
        in_specs=[qspec(blocks["b_q"]), qspec(blocks["d_q"]), qspec(blocks["c_u"]), qspec(blocks["c_v"]),
                  kvspec(blocks["b_kv"]), kvspec(blocks["d_kv"]),
                  tabspec, tabspec, const(ck), const(sk), const(qg), const(kg),
                  const(ones_q), const(ones_k), const(sink), const(gb), const(gd),
                  const(lng), const(lnb), const(ws), const(bsb), const(gc)],
        out_specs=[ospec, ospec, ospec],
        out_shape=[out, out, out],
        scratch_shapes=[kvscr, kvscr, kvscr, kvscr],
        compiler_params=_cparams(2),
        name="mixers_bcd",
    )(p, p, p, p, p, p, cq, sq, ck, sk, qg, kg, ones_q, ones_k, sink, gb, gd, lng, lnb, ws, bsb, gc)


GLA_LEVELS = int(math.log2(GLA_CHUNK))
GLA_ROW_PRE = GLA_LEVELS * GLA_CHUNK
GLA_ROW_SUF = GLA_ROW_PRE + GLA_CHUNK
GLA_ROW_LAST = GLA_ROW_SUF + GLA_CHUNK
GLA_ROWS = GLA_ROW_LAST + 16
GLA_STEPS_PER_ITER = 2


def _gla_constants(dk):
    n = GLA_CHUNK
    csum = np.zeros((2, GLA_ROWS, n), np.float32)
    qmask = np.zeros((2, GLA_LEVELS + 1, n, 1), np.float32)
    kmask = np.zeros((2, GLA_LEVELS + 1, n, 1), np.float32)
    smask = np.zeros((2, GLA_LEVELS + 1, n, n), np.float32)
    for d in range(2):
        tau = np.arange(n) if d == 0 else n - 1 - np.arange(n)
        qmask[d, 0] = 1.0
        kmask[d, 0] = 1.0
        smask[d, 0] = np.eye(n)
        for lv in range(1, GLA_LEVELS + 1):
            h = 1 << (lv - 1)
            blk = tau // (2 * h)
            upper = (tau % (2 * h)) >= h
            same = blk[:, None] == blk[None, :]
            both_up = upper[:, None] & upper[None, :]
            both_lo = (~upper[:, None]) & (~upper[None, :])
            c = np.where(upper[:, None],
                         same & both_up & (tau[None, :] <= tau[:, None]),
                         same & both_lo & (tau[None, :] > tau[:, None]))
            csum[d, (lv - 1) * n:lv * n] = c
            qmask[d, lv, :, 0] = upper
            kmask[d, lv, :, 0] = ~upper
            smask[d, lv] = same & upper[:, None] & (~upper[None, :])
        csum[d, GLA_ROW_PRE:GLA_ROW_SUF] = tau[None, :] <= tau[:, None]
        csum[d, GLA_ROW_SUF:GLA_ROW_LAST] = tau[None, :] > tau[:, None]
        csum[d, GLA_ROW_LAST:] = 1.0
    hk = GLA_HEADS * dk
    qmask = np.broadcast_to(qmask, (2, GLA_LEVELS + 1, n, hk)).copy()
    kmask = np.broadcast_to(kmask, (2, GLA_LEVELS + 1, n, hk)).copy()
    smask = np.tile(smask, (1, 1, 1, GLA_HEADS))
    return csum, qmask, kmask, smask


def _gla_kernel(pa_ref, lr_ref, gw_ref, gb_ref, gain_ref, cs_ref, qm_ref, km_ref, sm_ref,
                o_ref, la_scr, oacc_scr, st_scr, *, seq_t, ctx_len, dk, dv):
    hk = GLA_HEADS * dk
    hv = GLA_HEADS * dv
    n = GLA_CHUNK
    nc = seq_t // n
    nc_ctx = ctx_len // n

    z = _dot(lr_ref[0].astype(BF16), gw_ref[...]) + gb_ref[...]
    la_scr[...] = (jnp.minimum(z, 0.0) - jnp.log1p(jnp.exp(-jnp.abs(z)))) * (1.0 / GLA_TAU)
    oacc_scr[...] = jnp.zeros_like(oacc_scr)
    st_scr[...] = jnp.zeros_like(st_scr)

    lane_head = lax.broadcasted_iota(jnp.int32, (n, hk), 1) // dk
    head_lanes = [lane_head == h for h in range(GLA_HEADS)]

    def per_head_rows(x):
        return jnp.concatenate([jnp.where(m, x, 0.0) for m in head_lanes], axis=0).astype(BF16)

    steps = GLA_STEPS_PER_ITER
    assert nc % steps == 0
    chains = [(d, u) for u in range(steps) for d in (0, 1)]

    def scan_iter(it, carry):
        r0 = []
        for d, u in chains:
            s = it * steps + u
            c = s if d == 0 else jnp.where(s < nc_ctx, nc_ctx - 1 - s, nc - 1 - (s - nc_ctx))
            r0.append(pl.multiple_of(c * n, n))
        q = [pa_ref[0, pl.ds(r, n), 0:hk] * (dk ** -0.5) for r in r0]
        k = [pa_ref[0, pl.ds(r, n), hk:2 * hk] for r in r0]
        v = [pa_ref[0, pl.ds(r, n), 2 * hk:2 * hk + hv] for r in r0]
        ex2 = []
        for (d, _), r in zip(chains, r0):
            a = la_scr[pl.ds(r, n), d * hk:(d + 1) * hk]
            a_hi = a.astype(BF16)
            a_lo = (a - a_hi.astype(F32)).astype(BF16)
            ex2.append(_dot(cs_ref[d], jnp.concatenate([a_hi, a_lo], axis=1)))
        e = [jnp.exp(x[:, :hk] + x[:, hk:]) for x in ex2]

        scores = [jnp.zeros((n, GLA_HEADS * n), F32) for _ in chains]
        for lv in range(GLA_LEVELS + 1):
            for ci, (d, _) in enumerate(chains):
                if lv == 0:
                    ql, kl = q[ci], k[ci]
                else:
                    el = e[ci][(lv - 1) * n:lv * n]
                    ql = q[ci] * el * qm_ref[d, lv]
                    kl = k[ci] * el * km_ref[d, lv]
                scores[ci] = scores[ci] + _dot_nt(ql.astype(BF16), per_head_rows(kl)) * sm_ref[d, lv]

        kt = []
        for ci in range(len(chains)):
            kh = k[ci] * e[ci][GLA_ROW_SUF:GLA_ROW_LAST]
            e_last = e[ci][GLA_ROW_LAST:GLA_ROW_LAST + 1]
            kt.append(jnp.concatenate([kh, jnp.broadcast_to(e_last, (n, hk))], axis=0).T)
        dstate = []
        o_intra = []
        for ci in range(len(chains)):
            vpad = jnp.concatenate([v[ci], jnp.zeros_like(v[ci])], axis=0).astype(BF16)
            full = _dot(kt[ci].astype(BF16), vpad)
            dstate.append(jnp.concatenate(
                [full[h * dk:(h + 1) * dk, h * dv:(h + 1) * dv] for h in range(GLA_HEADS)], axis=0))
            v_rows = jnp.concatenate([v[ci][:, h * dv:(h + 1) * dv] for h in range(GLA_HEADS)],
                                     axis=0).astype(BF16)
            o_intra.append(_dot(per_head_rows(scores[ci]), v_rows))

        st = [st_scr[0], st_scr[1]]
        for ci, (d, _) in enumerate(chains):
            qh = q[ci] * e[ci][GLA_ROW_PRE:GLA_ROW_SUF]
            o_rows = o_intra[ci] + _dot(per_head_rows(qh), st[d].astype(BF16))
            oacc_scr[pl.ds(r0[ci], n), :] += jnp.concatenate(
                [o_rows[h * n:(h + 1) * n, :] for h in range(GLA_HEADS)], axis=1)
            st[d] = st[d] * kt[ci][:, n:n + 1] + dstate[ci]
        st_scr[0] = st[0]
        st_scr[1] = st[1]
        return carry

    lax.fori_loop(0, nc // steps, scan_iter, 0)

    tfin = _pick(seq_t, (256, 128))
    for t0 in range(0, seq_t, tfin):
        o = oacc_scr[t0:t0 + tfin, :]
        parts = [_rms(o[:, h * dv:(h + 1) * dv], gain_ref[:, h * dv:(h + 1) * dv])
                 for h in range(GLA_HEADS)]
        gate = pa_ref[0, t0:t0 + tfin, 2 * hk + hv:2 * hk + 2 * hv]
        o_ref[0, t0:t0 + tfin, :] = (jnp.concatenate(parts, axis=1) * _silu(gate)).astype(BF16)


def _gla(p, lr, gw, gb, gain, consts, *, ctx_len, dk, dv):
    b, t, _ = p.shape
    hk = GLA_HEADS * dk
    hv = GLA_HEADS * dv
    pa_w = 2 * hk + 2 * hv
    cs, qm, km, sm = consts
    kern = functools.partial(_gla_kernel, seq_t=t, ctx_len=ctx_len, dk=dk, dv=dv)
    const = lambda a: _resident(a.shape)
    return pl.pallas_call(
        kern,
        grid=(b,),
        in_specs=[pl.BlockSpec((1, t, pa_w), lambda bi: (bi, 0, 0)),
                  pl.BlockSpec((1, t, LANES), lambda bi: (bi, 0, 0)),
                  const(gw), const(gb), const(gain), const(cs), const(qm), const(km), const(sm)],
        out_specs=pl.BlockSpec((1, t, hv), lambda bi: (bi, 0, 0)),
        out_shape=jax.ShapeDtypeStruct((b, t, hv), BF16),
        scratch_shapes=[pltpu.VMEM((t, 2 * hk), F32), pltpu.VMEM((t, hv), F32),
                        pltpu.VMEM((2, hk, dv), F32)],
        compiler_params=_cparams(1),
        name="gla",
    )(p, lr, gw, gb, gain, cs, qm, km, sm)


def _rope_tables(length, ctx_len, n_heads):
    rows = length // GRID_W
    row = jnp.repeat(jnp.arange(rows), GRID_W).astype(F32)
    col = jnp.tile(jnp.arange(GRID_W), rows).astype(F32)
    quarter = HEAD_DIM // 4
    inv = ROPE_THETA ** (-jnp.arange(quarter, dtype=F32) / quarter)
    ar, ac = row[:, None] * inv, col[:, None] * inv
    ang = jnp.concatenate([ar, ar, ac, ac], axis=-1)
    cos, sin = jnp.cos(ang), jnp.sin(ang)
    first = (jnp.arange(HEAD_DIM) % (HEAD_DIM // 2)) < quarter
    sin_signed = jnp.where(first[None, :], -sin, sin)
    cos = jnp.concatenate([jnp.ones((ctx_len, HEAD_DIM), F32), cos], axis=0)
    sin_signed = jnp.concatenate([jnp.zeros((ctx_len, HEAD_DIM), F32), sin_signed], axis=0)
    return jnp.tile(cos, (1, n_heads)), jnp.tile(sin_signed, (1, n_heads))


def _pair_heads(a, axis, n_kv):
    axis = axis % a.ndim
    g = a.shape[axis] // (n_kv * HEAD_DIM)
    shape = a.shape[:axis] + (n_kv, g, HEAD_DIM) + a.shape[axis + 1:]
    return jnp.swapaxes(a.reshape(shape), axis, axis + 1).reshape(a.shape)


def _block_diag_mask(rows, cols, rb, cb):
    r = np.arange(rows)[:, None] // rb
    c = np.arange(cols)[None, :] // cb
    return (r == c).astype(np.float32)


def kernel(x, c, ctx, c_ctx, w_mod, b_mod, norm_g, w_in, gla_gate_up, gla_gate_b, win_sink, cm_ln_g,
           cm_ln_b, cm_ws, cm_bs, qk_g, mix_g, w_out, w_ffn_in, w_ffn_out):
    bsz, seq, d = x.shape
    ctx_len = ctx.shape[1]
    seq_t = ctx_len + seq
    depth = w_mod.shape[0]
    gw = d // 4
    rank = gla_gate_up.shape[2]
    dk = gla_gate_up.shape[3] // GLA_HEADS
    dv = gw // GLA_HEADS
    n_heads = gw // HEAD_DIM
    kvw = WIN_KV * HEAD_DIM
    hk = GLA_HEADS * dk
    assert kvw == LANES and GA_KV == WIN_KV and gw % LANES == 0 and 2 * rank <= LANES
    assert dk == GLA_CHUNK and 2 * hk == gw
    assert ctx_len % CM_CHUNK == 0 and seq % CM_CHUNK == 0 and seq % GRID_W == 0
    assert ctx_len % ROW_CHUNK == 0

    sizes = [hk, hk, gw, gw, rank, rank, gw, kvw, kvw, gw, gw, gw, kvw, kvw]
    off = [int(v) for v in np.concatenate([[0], np.cumsum(sizes)])]
    blocks = {"c_u": 3, "c_v": 4, "b_q": 5, "d_q": 6,
              "b_kv": (7 * gw) // (2 * LANES), "d_kv": (7 * gw) // (2 * LANES) + 1}

    cos_q, sin_q = _rope_tables(seq, ctx_len, n_heads)
    cos_k, sin_k = cos_q[:, :LANES], sin_q[:, :LANES]
    ones_q = jnp.asarray(_block_diag_mask(gw, gw, HEAD_DIM, HEAD_DIM), BF16)
    ones_k = ones_q[:LANES, :LANES]
    tabs = (cos_q, sin_q, cos_k, sin_k, ones_q, ones_k)

    csum, qmask, kmask, smask = _gla_constants(dk)
    gla_consts = (jnp.asarray(csum, BF16), jnp.asarray(qmask), jnp.asarray(kmask), jnp.asarray(smask))

    rows = 16
    cond = jnp.concatenate([c, c_ctx[None, :], jnp.zeros((rows - bsz - 1, d), F32)], axis=0)
    mod_all = _modulation(cond, w_mod, b_mod)
    mod_lat = mod_all[:, :bsz].reshape(depth, bsz, 6, d)
    mod_ctx = jnp.broadcast_to(mod_all[:, bsz].reshape(depth, 1, 6, d), (depth, bsz, 6, d))
    pad2 = jnp.zeros((depth, bsz, 2, d), F32)
    mods = jnp.concatenate([mod_ctx, pad2, mod_lat, pad2], axis=2)

    xs = jnp.concatenate([ctx, x], axis=1).reshape(bsz * seq_t, d)

    for l in range(depth):
        wl = w_in[l]
        w_main = jnp.concatenate(
            [wl[:, off[0]:off[4]], wl[:, off[9]:off[11]],
             _pair_heads(wl[:, off[6]:off[7]], 1, WIN_KV), _pair_heads(wl[:, off[11]:off[12]], 1, GA_KV),
             wl[:, off[7]:off[9]], wl[:, off[12]:off[14]]], axis=1).astype(BF16)
        w_lr = jnp.pad(wl[:, off[4]:off[6]], ((0, 0), (0, LANES - 2 * rank))).astype(BF16)
        gate_w = jnp.zeros((LANES, 2 * hk), F32)
        gate_w = gate_w.at[:rank, :hk].set(gla_gate_up[l, 0])
        gate_w = gate_w.at[rank:2 * rank, hk:].set(gla_gate_up[l, 1]).astype(BF16)
        gate_b = gla_gate_b[l].reshape(1, -1)
        mg = mix_g[l]
        ga, gc = mg[0:gw], mg[2 * gw:3 * gw]
        gb_ = _pair_heads(mg[gw:2 * gw], 0, WIN_KV)
        gd = _pair_heads(mg[3 * gw:], 0, GA_KV)
        wo = w_out[l]
        wo = jnp.concatenate([wo[0:gw], _pair_heads(wo[gw:2 * gw], 0, WIN_KV), wo[2 * gw:3 * gw],
                              _pair_heads(wo[3 * gw:], 0, GA_KV)], axis=0).astype(BF16)
        sink = _pair_heads(jnp.repeat(win_sink[l], HEAD_DIM), 0, WIN_KV)[None, :]
        qg = jnp.tile(qk_g[l, 0], n_heads)[None, :]
        kg = jnp.tile(qk_g[l, 1], WIN_KV)[None, :]
        bsb = jnp.repeat(cm_bs[l].T, gw // CM_GROUPS, axis=1)
        modl = mods[l]

        p, lr = _inproj(xs, modl, norm_g[l, 0][None, :], w_main, w_lr, seq_t=seq_t, ctx_len=ctx_len)
        p3 = p.reshape(bsz, seq_t, -1)
        lr3 = lr.reshape(bsz, seq_t, LANES)
        y_a = _gla(p3, lr3, gate_w, gate_b, ga[None, :], gla_consts, ctx_len=ctx_len, dk=dk, dv=dv)
        y_b, y_c, y_d = _mixers(p3, tabs, qg, kg, sink, gb_[None, :], gd[None, :],
                                cm_ln_g[l][None, :], cm_ln_b[l][None, :], cm_ws[l].astype(BF16), bsb,
                                gc[None, :], blocks=blocks, ctx_len=ctx_len)
        ys = [y.reshape(bsz * seq_t, gw) for y in (y_a, y_b, y_c, y_d)]
        last = l == depth - 1
        xs = _outproj(xs, ys, modl, norm_g[l, 1][None, :], wo, seq_t=seq_t, ctx_len=ctx_len,
                      latent_only=last)
        xs = _ffn(xs, modl, norm_g[l, 2][None, :], norm_g[l, 3][None, :], w_ffn_in[l].astype(BF16),
                  w_ffn_out[l].astype(BF16), rows_per_batch=seq if last else seq_t,
                  row_base=ctx_len if last else 0, ctx_len=ctx_len)

    return xs.reshape(bsz, seq, d)
```

```python
import functools
import math

import numpy as np
import jax
import jax.numpy as jnp
from jax import lax
from jax.experimental import pallas as pl
from jax.experimental.pallas import tpu as pltpu

F32 = jnp.float32
BF16 = jnp.bfloat16

EPS = 1e-6
HEAD_DIM = 64
ATTN_SCALE = HEAD_DIM ** -0.5
ROPE_THETA = 10000.0
GRID_W = 64
GLA_HEADS = 4
GLA_TAU = 16.0
GLA_CHUNK = 64
WIN_KV = 2
WINDOW = 128
CM_GROUPS = 4
CM_CHUNK = 128
GA_KV = 2

LANES = 128
NEG_BIG = -1e30
V7X_VMEM_LIMIT = 56 * 1024 * 1024
ROW_CHUNK = 64


def _cparams(n_axes, vmem=V7X_VMEM_LIMIT, flags=None):
    return pltpu.CompilerParams(dimension_semantics=("arbitrary",) * n_axes,
                                vmem_limit_bytes=vmem, flags=flags)


def _pick(total, candidates):
    for cand in candidates:
        if total % cand == 0:
            return cand
    raise ValueError(f"no tile in {candidates} divides {total}")


def _resident(shape):
    return pl.BlockSpec(shape, lambda *_: (0,) * len(shape), pipeline_mode=pl.Buffered(1))


def _dot(a, b):
    return jnp.dot(a, b, preferred_element_type=F32)


def _dot_nt(a, b):
    return lax.dot_general(a, b, (((1,), (1,)), ((), ())), preferred_element_type=F32)


def _rms(x, gain):
    return x * lax.rsqrt(jnp.mean(x * x, axis=-1, keepdims=True) + EPS) * gain


def _silu(x):
    return x * jax.nn.sigmoid(x)


def _gelu_tanh(x):
    return 0.5 * x * (1.0 + jnp.tanh(math.sqrt(2.0 / math.pi) * (x + 0.044715 * (x * x * x))))


def _mod_vec(mod_ref, k, chunk_is_ctx):
    return jnp.where(chunk_is_ctx, mod_ref[0, k:k + 1, :], mod_ref[0, 8 + k:9 + k, :])


def _row_loop(n_rows, step):
    n_chunks = n_rows // ROW_CHUNK
    lax.fori_loop(0, n_chunks, step, 0, unroll=3 if n_chunks % 3 == 0 else 2)


def _norm_modulate_rows(x_ref, h_ref, g_ref, mod_ref, k_shift, k_scale, row0, ctx_len):
    def step(c, carry):
        r = pl.multiple_of(c * ROW_CHUNK, ROW_CHUNK)
        is_ctx = row0 + r < ctx_len
        gain = g_ref[...] * (1.0 + _mod_vec(mod_ref, k_scale, is_ctx))
        x = x_ref[pl.ds(r, ROW_CHUNK), :]
        inv = lax.rsqrt(jnp.mean(x * x, axis=-1, keepdims=True) + EPS)
        h_ref[pl.ds(r, ROW_CHUNK), :] = (x * inv * gain + _mod_vec(mod_ref, k_shift, is_ctx)).astype(BF16)
        return carry

    _row_loop(x_ref.shape[0], step)


def _gated_residual_rows(x_ref, z_ref, o_ref, g_ref, mod_ref, k_gate, row0, ctx_len):
    def step(c, carry):
        r = pl.multiple_of(c * ROW_CHUNK, ROW_CHUNK)
        gain = g_ref[...] * _mod_vec(mod_ref, k_gate, row0 + r < ctx_len)
        z = z_ref[pl.ds(r, ROW_CHUNK), :]
        inv = lax.rsqrt(jnp.mean(z * z, axis=-1, keepdims=True) + EPS)
        o_ref[pl.ds(r, ROW_CHUNK), :] = x_ref[pl.ds(r, ROW_CHUNK), :] + z * inv * gain
        return carry

    _row_loop(x_ref.shape[0], step)


def _mod_kernel(c_ref, w_ref, b_ref, o_ref):
    s = _silu(c_ref[...]).astype(BF16)
    o_ref[0] = _dot(s, w_ref[0].astype(BF16)) + b_ref[0]


def _modulation(cond, w_mod, b_mod):
    depth, d, n = w_mod.shape
    rows = cond.shape[0]
    tn = _pick(n, (1024, 512, 256, 128))
    return pl.pallas_call(
        _mod_kernel,
        grid=(depth, n // tn),
        in_specs=[pl.BlockSpec((rows, d), lambda l, j: (0, 0)),
                  pl.BlockSpec((1, d, tn), lambda l, j: (l, 0, j)),
                  pl.BlockSpec((1, 1, tn), lambda l, j: (l, 0, j))],
        out_specs=pl.BlockSpec((1, rows, tn), lambda l, j: (l, 0, j)),
        out_shape=jax.ShapeDtypeStruct((depth, rows, n), F32),
        compiler_params=_cparams(2),
        name="modulation",
    )(cond, w_mod, b_mod.reshape(depth, 1, n))


def _pair_lanes(y, n_kv):
    g = y.shape[1] // (n_kv * HEAD_DIM)
    cols = [y[:, (kv * g + j) * HEAD_DIM:(kv * g + j + 1) * HEAD_DIM] for j in range(g) for kv in range(n_kv)]
    return jnp.concatenate(cols, axis=1)


def _w_in_kernel(w_ref, main_ref, lr_ref, *, off, rank):
    x = w_ref[0]
    main = jnp.concatenate(
        [x[:, off[0]:off[4]], x[:, off[9]:off[11]],
         _pair_lanes(x[:, off[6]:off[7]], WIN_KV), _pair_lanes(x[:, off[11]:off[12]], GA_KV),
         x[:, off[7]:off[9]], x[:, off[12]:off[14]]], axis=1)
    main_ref[0] = main.astype(BF16)
    pad = jnp.zeros((x.shape[0], LANES - 2 * rank), F32)
    lr_ref[0] = jnp.concatenate([x[:, off[4]:off[6]], pad], axis=1).astype(BF16)


def _prep_w_in(w_in, off, rank):
    depth, d, n_in = w_in.shape
    n_main = n_in - 2 * rank
    rb = _pick(d, (256, 128))
    kern = functools.partial(_w_in_kernel, off=tuple(off), rank=rank)
    return pl.pallas_call(
        kern,
        grid=(depth, d // rb),
        in_specs=[pl.BlockSpec((1, rb, n_in), lambda l, i: (l, i, 0))],
        out_specs=[pl.BlockSpec((1, rb, n_main), lambda l, i: (l, i, 0)),
                   pl.BlockSpec((1, rb, LANES), lambda l, i: (l, i, 0))],
        out_shape=[jax.ShapeDtypeStruct((depth, d, n_main), BF16),
                   jax.ShapeDtypeStruct((depth, d, LANES), BF16)],
        compiler_params=_cparams(2),
        name="prep_w_in",
    )(w_in)


def _inproj_kernel(x_ref, mod_ref, g_ref, w_ref, wlr_ref, p_ref, lr_ref, h_scr, *,
                   tm, tiles_per_batch, ctx_len):
    row0 = (pl.program_id(0) % tiles_per_batch) * tm
    _norm_modulate_rows(x_ref, h_scr, g_ref, mod_ref, 0, 1, row0, ctx_len)
    h = h_scr[...]
    lr_ref[...] = _dot(h, wlr_ref[0])
    p_ref[...] = _dot(h, w_ref[0])


def _inproj(xs, modl, gain, w_main, w_lr, layer, *, seq_t, ctx_len):
    m, d = xs.shape
    n = w_main.shape[2]
    tm = _pick(seq_t, (576, 384, 128))
    tpb = seq_t // tm
    kern = functools.partial(_inproj_kernel, tm=tm, tiles_per_batch=tpb, ctx_len=ctx_len)
    layer_block = lambda width: pl.BlockSpec((1, d, width), lambda i: (layer, 0, 0),
                                             pipeline_mode=pl.Buffered(1))
    return pl.pallas_call(
        kern,
        grid=(m // tm,),
        in_specs=[pl.BlockSpec((tm, d), lambda i: (i, 0)),
                  pl.BlockSpec((1, 16, d), lambda i: (i // tpb, 0, 0)),
                  _resident((1, d)), layer_block(n), layer_block(LANES)],
        out_specs=[pl.BlockSpec((tm, n), lambda i: (i, 0)),
                   pl.BlockSpec((tm, LANES), lambda i: (i, 0))],
        out_shape=[jax.ShapeDtypeStruct((m, n), F32),
                   jax.ShapeDtypeStruct((m, LANES), F32)],
        scratch_shapes=[pltpu.VMEM((tm, d), BF16)],
        compiler_params=_cparams(1),
        name="inproj",
    )(xs, modl, gain, w_main, w_lr)


def _outproj_kernel(x_ref, ya_ref, yb_ref, yc_ref, yd_ref, mod_ref, g_ref, w_ref, o_ref, y_scr, *,
                    tm, tiles_per_batch, row_base, ctx_len, gw):
    i = pl.program_id(0)
    y_scr[:, 0 * gw:1 * gw] = ya_ref[...]
    y_scr[:, 1 * gw:2 * gw] = yb_ref[...]
    y_scr[:, 2 * gw:3 * gw] = yc_ref[...]
    y_scr[:, 3 * gw:4 * gw] = yd_ref[...]
    zn = _rms(_dot(y_scr[...], w_ref[...]), g_ref[...])
    row0 = row_base + (i % tiles_per_batch) * tm

    @pl.when(row0 >= ctx_len)
    def _():
        o_ref[...] = x_ref[...] + mod_ref[0, 10:11, :] * zn

    @pl.when(row0 < ctx_len)
    def _():
        is_ctx = row0 + lax.broadcasted_iota(jnp.int32, (tm, 1), 0) < ctx_len
        o_ref[...] = x_ref[...] + jnp.where(is_ctx, mod_ref[0, 2:3, :], mod_ref[0, 10:11, :]) * zn


def _outproj(xs, ys, modl, gain, w, *, seq_t, ctx_len, latent_only):
    m, d = xs.shape
    gw = ys[0].shape[1]
    if latent_only:
        seq = seq_t - ctx_len
        tm = _pick(seq, (512, 256, 128))
        tpb = seq // tm
        n_tiles = (m // seq_t) * tpb
        align = math.gcd(seq_t, ctx_len, tm)
        row_start = lambda i: pl.multiple_of((i // tpb) * seq_t + ctx_len + (i % tpb) * tm, align)
        xspec = pl.BlockSpec((pl.Element(tm), pl.Element(d)), lambda i: (row_start(i), 0))
        yspec = pl.BlockSpec((pl.Element(tm), pl.Element(gw)), lambda i: (row_start(i), 0))
        row_base = ctx_len
    else:
        tm = _pick(seq_t, (576, 384, 128))
        tpb = seq_t // tm
        n_tiles = m // tm
        xspec = pl.BlockSpec((tm, d), lambda i: (i, 0))
        yspec = pl.BlockSpec((tm, gw), lambda i: (i, 0))
        row_base = 0
    kern = functools.partial(_outproj_kernel, tm=tm, tiles_per_batch=tpb, row_base=row_base,
                             ctx_len=ctx_len, gw=gw)
    return pl.pallas_call(
        kern,
        grid=(n_tiles,),
        in_specs=[xspec, yspec, yspec, yspec, yspec,
                  pl.BlockSpec((1, 16, d), lambda i: (i // tpb, 0, 0)),
                  _resident((1, d)), _resident((4 * gw, d))],
        out_specs=pl.BlockSpec((tm, d), lambda i: (i, 0)),
        out_shape=jax.ShapeDtypeStruct((n_tiles * tm, d), F32),
        scratch_shapes=[pltpu.VMEM((tm, 4 * gw), BF16)],
        compiler_params=_cparams(1),
        name="outproj",
    )(xs, *ys, modl, gain, w)


def _ffn_kernel(x_ref, mod_ref, g2_ref, g3_ref, wg_ref, wu_ref, wo_ref, o_ref, h_scr, acc_scr, *,
                tm, tiles_per_batch, row_base, ctx_len, nf):
    i = pl.program_id(0)
    j = pl.program_id(1)
    row0 = row_base + (i % tiles_per_batch) * tm

    @pl.when(j == 0)
    def _():
        _norm_modulate_rows(x_ref, h_scr, g2_ref, mod_ref, 3, 4, row0, ctx_len)
        acc_scr[...] = jnp.zeros_like(acc_scr)

    h = h_scr[...]
    act = (_silu(_dot(h, wg_ref[...])) * _dot(h, wu_ref[...])).astype(BF16)
    acc_scr[...] += _dot(act, wo_ref[...])

    @pl.when(j == nf - 1)
    def _():
        _gated_residual_rows(x_ref, acc_scr, o_ref, g3_ref, mod_ref, 5, row0, ctx_len)


def _ffn(xs, modl, g2, g3, w_in, w_out, *, rows_per_batch, row_base, ctx_len):
    m, d = xs.shape
    dff = w_out.shape[0]
    tm = _pick(rows_per_batch, (768, 512, 384, 128))
    tf = _pick(dff, (512, 256, 128))
    nf = dff // tf
    tpb = rows_per_batch // tm
    kern = functools.partial(_ffn_kernel, tm=tm, tiles_per_batch=tpb, row_base=row_base,
                             ctx_len=ctx_len, nf=nf)
    return pl.pallas_call(
        kern,
        grid=(m // tm, nf),
        in_specs=[pl.BlockSpec((tm, d), lambda i, j: (i, 0)),
                  pl.BlockSpec((1, 16, d), lambda i, j: (i // tpb, 0, 0)),
                  _resident((1, d)), _resident((1, d)),
                  pl.BlockSpec((d, tf), lambda i, j: (0, j)),
                  pl.BlockSpec((d, tf), lambda i, j: (0, nf + j)),
                  pl.BlockSpec((tf, d), lambda i, j: (j, 0))],
        out_specs=pl.BlockSpec((tm, d), lambda i, j: (i, 0)),
        out_shape=jax.ShapeDtypeStruct((m, d), F32),
        scratch_shapes=[pltpu.VMEM((tm, d), BF16), pltpu.VMEM((tm, d), F32)],
        compiler_params=_cparams(2),
        name="ffn",
    )(xs, modl, g2, g3, w_in, w_in, w_out)


def _cmlp_tile(u_ref, v_ref, lng_ref, lnb_ref, ws_ref, bsb_ref, gc_ref, o_ref):
    tc, gw = u_ref.shape[1], u_ref.shape[2]
    cw = gw // CM_GROUPS
    u = _gelu_tanh(u_ref[0])
    v = _gelu_tanh(v_ref[0])
    vc = v - jnp.mean(v, axis=-1, keepdims=True)
    vn = vc * lax.rsqrt(jnp.mean(vc * vc, axis=-1, keepdims=True) + EPS) * lng_ref[...] + lnb_ref[...]
    vb = vn.astype(BF16)
    rows = []
    for ch in range(tc // CM_CHUNK):
        cols = []
        for g in range(CM_GROUPS):
            blk = vb[ch * CM_CHUNK:(ch + 1) * CM_CHUNK, g * cw:(g + 1) * cw]
            cols.append(_dot(ws_ref[g], blk) + bsb_ref[:, g * cw:(g + 1) * cw])
        rows.append(jnp.concatenate(cols, axis=1))
    s = jnp.concatenate(rows, axis=0)
    o_ref[0] = _rms(u * s, gc_ref[...]).astype(BF16)


def _group_mean_sq(x, ones_bd):
    sq = x * x
    hi = sq.astype(BF16)
    lo = (sq - hi.astype(F32)).astype(BF16)
    return (_dot(hi, ones_bd) + _dot(lo, ones_bd)) * (1.0 / HEAD_DIM)


def _rope(x, cos, sin_signed):
    w = x.shape[1]
    lane = lax.broadcasted_iota(jnp.int32, x.shape, 1)
    first = (lane % (HEAD_DIM // 2)) < (HEAD_DIM // 4)
    partner = jnp.where(first, pltpu.roll(x, w - HEAD_DIM // 4, 1), pltpu.roll(x, HEAD_DIM // 4, 1))
    return x * cos + partner * sin_signed


def _attn_pairs(q, k_scr, v_scr, sink_ref, pairs, *, mode, latent_tile, i, tq, seq_t, ctx_len):
    lo = lax.broadcasted_iota(jnp.int32, (tq, LANES), 1) < HEAD_DIM
    lane1 = lax.broadcasted_iota(jnp.int32, (1, LANES), 1)
    row2 = lax.broadcasted_iota(jnp.int32, (2 * tq, 1), 0)

    key_sets = [(0, ctx_len, None)]
    if latent_tile and mode == "window":
        wk = tq + 2 * WINDOW
        start = pl.multiple_of(jnp.clip(i * tq - WINDOW, ctx_len, seq_t - wk), LANES)
        r = lax.broadcasted_iota(jnp.int32, (2 * tq, wk), 0)
        r = jnp.where(r >= tq, r - tq, r)
        c = lax.broadcasted_iota(jnp.int32, (2 * tq, wk), 1)
        dist = (i * tq + r) - (start + c)
        key_sets.append((start, wk, (jnp.abs(dist) <= WINDOW)))
    elif latent_tile:
        key_sets = [(0, seq_t, None)]

    outs = []
    for j in pairs:
        qp = q[:, j * LANES:(j + 1) * LANES]
        q2 = jnp.concatenate([jnp.where(lo, qp, 0.0), jnp.where(lo, 0.0, qp)], axis=0).astype(BF16)
        scores = []
        for k0, kn, valid in key_sets:
            s = _dot_nt(q2, k_scr[pl.ds(k0, kn), :])
            scores.append(s if valid is None else jnp.where(valid, s, NEG_BIG))
        mx = functools.reduce(jnp.maximum, [jnp.max(s, axis=1, keepdims=True) for s in scores])
        if mode == "window":
            sv = sink_ref[:, j * LANES:(j + 1) * LANES]
            s_lo = jnp.max(jnp.where(lane1 < HEAD_DIM, sv, NEG_BIG), axis=1, keepdims=True)
            s_hi = jnp.max(jnp.where(lane1 < HEAD_DIM, NEG_BIG, sv), axis=1, keepdims=True)
            sk = jnp.where(row2 < tq, s_lo, s_hi)
            mx = jnp.maximum(mx, sk)
            den = jnp.exp(sk - mx)
        else:
            den = 0.0
        o2 = 0.0
        for (k0, kn, _), s in zip(key_sets, scores):
            p = jnp.exp(s - mx)
            den = den + jnp.sum(p, axis=1, keepdims=True)
            o2 = o2 + _dot(p.astype(BF16), v_scr[pl.ds(k0, kn), :])
        o2 = o2 / den
        outs.append(jnp.where(lo, o2[:tq], o2[tq:]))
    return outs


def _mixers_kernel(bq_ref, dq_ref, cu_ref, cv_ref, bkv_ref, dkv_ref, cq_ref, sq_ref, ck_ref, sk_ref,
                   qg_ref, kg_ref, onesq_ref, onesk_ref, sink_ref, gb_ref, gd_ref,
                   lng_ref, lnb_ref, ws_ref, bsb_ref, gc_ref,
                   yb_ref, yc_ref, yd_ref, kb_scr, vb_scr, kd_scr, vd_scr, *, tq, seq_t, ctx_len):
    i = pl.program_id(1)
    n_ctx_tiles = ctx_len // tq

    @pl.when(i == 0)
    def _():
        kv = bkv_ref[0]
        kb_scr[...] = _rope(kv[:, :LANES], ck_ref[...], sk_ref[...]).astype(BF16)
        vb_scr[...] = kv[:, LANES:].astype(BF16)
        kv = dkv_ref[0]
        k = kv[:, :LANES]
        k = k * lax.rsqrt(_group_mean_sq(k, onesk_ref[...]) + EPS) * kg_ref[...]
        kd_scr[...] = _rope(k, ck_ref[...], sk_ref[...]).astype(BF16)
        vd_scr[...] = kv[:, LANES:].astype(BF16)

    qb = _rope(bq_ref[0], cq_ref[...], sq_ref[...]) * ATTN_SCALE
    qd = dq_ref[0]
    qd = qd * lax.rsqrt(_group_mean_sq(qd, onesq_ref[...]) + EPS) * qg_ref[...]
    qd = _rope(qd, cq_ref[...], sq_ref[...]) * ATTN_SCALE

    n_pairs = qb.shape[1] // LANES
    half = n_pairs // 2

    def attend(latent_tile):
        common = dict(latent_tile=latent_tile, i=i, tq=tq, seq_t=seq_t, ctx_len=ctx_len)
        win = functools.partial(_attn_pairs, qb, kb_scr, vb_scr, sink_ref, mode="window", **common)
        ob = win(range(half))
        od = _attn_pairs(qd, kd_scr, vd_scr, None, range(n_pairs), mode="global", **common)
        ob = ob + win(range(half, n_pairs))
        yb_ref[0] = _rms(jnp.concatenate(ob, axis=1), gb_ref[...]).astype(BF16)
        yd_ref[0] = _rms(jnp.concatenate(od, axis=1), gd_ref[...]).astype(BF16)

    pl.when(i < n_ctx_tiles)(functools.partial(attend, False))
    pl.when(i >= n_ctx_tiles)(functools.partial(attend, True))
    _cmlp_tile(cu_ref, cv_ref, lng_ref, lnb_ref, ws_ref, bsb_ref, gc_ref, yc_ref)


def _mixers(p, tabs, qg, kg, sink, gb, gd, lng, lnb, ws, bsb, gc, *, blocks, ctx_len):
    b, t, _ = p.shape
    gw = gb.shape[1]
    tq = _pick(t, (256, 128))
    assert ctx_len % tq == 0 and tq % CM_CHUNK == 0
    cq, sq, ck, sk, ones_q, ones_k = tabs
    kern = functools.partial(_mixers_kernel, tq=tq, seq_t=t, ctx_len=ctx_len)
    const = lambda a: _resident(a.shape)
    qspec = lambda blk: pl.BlockSpec((1, tq, gw), lambda bi, i: (bi, i, blk))
    kvspec = lambda blk: pl.BlockSpec((1, t, 2 * LANES), lambda bi, i: (bi, 0, blk))
    tabspec = pl.BlockSpec((tq, gw), lambda bi, i: (i, 0))
    out = jax.ShapeDtypeStruct((b, t, gw), BF16)
    ospec = pl.BlockSpec((1, tq, gw), lambda bi, i: (bi, i, 0))
    kvscr = pltpu.VMEM((t, LANES), BF16)
    return pl.pallas_call(
        kern,
        grid=(b, t // tq),
        in_specs=[qspec(blocks["b_q"]), qspec(blocks["d_q"]), qspec(blocks["c_u"]), qspec(blocks["c_v"]),
                  kvspec(blocks["b_kv"]), kvspec(blocks["d_kv"]),
                  tabspec, tabspec, const(ck), const(sk), const(qg), const(kg),
                  const(ones_q), const(ones_k), const(sink), const(gb), const(gd),
                  const(lng), const(lnb), const(ws), const(bsb), const(gc)],
        out_specs=[ospec, ospec, ospec],
        out_shape=[out, out, out],
        scratch_shapes=[kvscr, kvscr, kvscr, kvscr],
        compiler_params=_cparams(2),
        name="mixers_bcd",
    )(p, p, p, p, p, p, cq, sq, ck, sk, qg, kg, ones_q, ones_k, sink, gb, gd, lng, lnb, ws, bsb, gc)


GLA_LEVELS = int(math.log2(GLA_CHUNK))
GLA_ROW_PRE = GLA_LEVELS * GLA_CHUNK
GLA_ROW_SUF = GLA_ROW_PRE + GLA_CHUNK
GLA_ROW_LAST = GLA_ROW_SUF + GLA_CHUNK
GLA_ROWS = GLA_ROW_LAST + 16
GLA_STEPS_PER_ITER = 2


def _gla_constants(dk):
    n = GLA_CHUNK
    csum = np.zeros((2, GLA_ROWS, n), np.float32)
    qmask = np.zeros((2, GLA_LEVELS + 1, n, 1), np.float32)
    kmask = np.zeros((2, GLA_LEVELS + 1, n, 1), np.float32)
    smask = np.zeros((2, GLA_LEVELS + 1, n, n), np.float32)
    for d in range(2):
        tau = np.arange(n) if d == 0 else n - 1 - np.arange(n)
        qmask[d, 0] = 1.0
        kmask[d, 0] = 1.0
        smask[d, 0] = np.eye(n)
        for lv in range(1, GLA_LEVELS + 1):
            h = 1 << (lv - 1)
            blk = tau // (2 * h)
            upper = (tau % (2 * h)) >= h
            same = blk[:, None] == blk[None, :]
            both_up = upper[:, None] & upper[None, :]
            both_lo = (~upper[:, None]) & (~upper[None, :])
            c = np.where(upper[:, None],
                         same & both_up & (tau[None, :] <= tau[:, None]),
                         same & both_lo & (tau[None, :] > tau[:, None]))
            csum[d, (lv - 1) * n:lv * n] = c
            qmask[d, lv, :, 0] = upper
            kmask[d, lv, :, 0] = ~upper
            smask[d, lv] = same & upper[:, None] & (~upper[None, :])
        csum[d, GLA_ROW_PRE:GLA_ROW_SUF] = tau[None, :] <= tau[:, None]
        csum[d, GLA_ROW_SUF:GLA_ROW_LAST] = tau[None, :] > tau[:, None]
        csum[d, GLA_ROW_LAST:] = 1.0
    hk = GLA_HEADS * dk
    qmask = np.broadcast_to(qmask, (2, GLA_LEVELS + 1, n, hk)).copy()
    kmask = np.broadcast_to(kmask, (2, GLA_LEVELS + 1, n, hk)).copy()
    smask = np.tile(smask, (1, 1, 1, GLA_HEADS))
    return csum, qmask, kmask, smask


def _gla_kernel(pa_ref, lr_ref, gw_ref, gb_ref, gain_ref, cs_ref, qm_ref, km_ref, sm_ref,
                o_ref, la_scr, oacc_scr, st_scr, *, seq_t, ctx_len, dk, dv):
    hk = GLA_HEADS * dk
    hv = GLA_HEADS * dv
    n = GLA_CHUNK
    nc = seq_t // n
    nc_ctx = ctx_len // n

    z = _dot(lr_ref[0].astype(BF16), gw_ref[...]) + gb_ref[...]
    la_scr[...] = (jnp.minimum(z, 0.0) - jnp.log1p(jnp.exp(-jnp.abs(z)))) * (1.0 / GLA_TAU)
    oacc_scr[...] = jnp.zeros_like(oacc_scr)
    st_scr[...] = jnp.zeros_like(st_scr)

    lane_head = lax.broadcasted_iota(jnp.int32, (n, hk), 1) // dk
    head_lanes = [lane_head == h for h in range(GLA_HEADS)]

    def per_head_rows(x):
        return jnp.concatenate([jnp.where(m, x, 0.0) for m in head_lanes], axis=0).astype(BF16)

    steps = GLA_STEPS_PER_ITER
    assert nc % steps == 0
    chains = [(d, u) for u in range(steps) for d in (0, 1)]

    def scan_iter(it, carry):
        r0 = []
        for d, u in chains:
            s = it * steps + u
            c = s if d == 0 else jnp.where(s < nc_ctx, nc_ctx - 1 - s, nc - 1 - (s - nc_ctx))
            r0.append(pl.multiple_of(c * n, n))
        q = [pa_ref[0, pl.ds(r, n), 0:hk] * (dk ** -0.5) for r in r0]
        k = [pa_ref[0, pl.ds(r, n), hk:2 * hk] for r in r0]
        v = [pa_ref[0, pl.ds(r, n), 2 * hk:2 * hk + hv] for r in r0]
        ex2 = []
        for (d, _), r in zip(chains, r0):
            a = la_scr[pl.ds(r, n), d * hk:(d + 1) * hk]
            a_hi = a.astype(BF16)
            a_lo = (a - a_hi.astype(F32)).astype(BF16)
            ex2.append(_dot(cs_ref[d], jnp.concatenate([a_hi, a_lo], axis=1)))
        e = [jnp.exp(x[:, :hk] + x[:, hk:]) for x in ex2]

        scores = [jnp.zeros((n, GLA_HEADS * n), F32) for _ in chains]
        for lv in range(GLA_LEVELS + 1):
            for ci, (d, _) in enumerate(chains):
                if lv == 0:
                    ql, kl = q[ci], k[ci]
                else:
                    el = e[ci][(lv - 1) * n:lv * n]
                    ql = q[ci] * el * qm_ref[d, lv]
                    kl = k[ci] * el * km_ref[d, lv]
                scores[ci] = scores[ci] + _dot_nt(ql.astype(BF16), per_head_rows(kl)) * sm_ref[d, lv]

        kt = []
        for ci in range(len(chains)):
            kh = k[ci] * e[ci][GLA_ROW_SUF:GLA_ROW_LAST]
            e_last = e[ci][GLA_ROW_LAST:GLA_ROW_LAST + 1]
            kt.append(jnp.concatenate([kh, jnp.broadcast_to(e_last, (n, hk))], axis=0).T)
        dstate = []
        o_intra = []
        for ci in range(len(chains)):
            vpad = jnp.concatenate([v[ci], jnp.zeros_like(v[ci])], axis=0).astype(BF16)
            full = _dot(kt[ci].astype(BF16), vpad)
            dstate.append(jnp.concatenate(
                [full[h * dk:(h + 1) * dk, h * dv:(h + 1) * dv] for h in range(GLA_HEADS)], axis=0))
            v_rows = jnp.concatenate([v[ci][:, h * dv:(h + 1) * dv] for h in range(GLA_HEADS)],
                                     axis=0).astype(BF16)
            o_intra.append(_dot(per_head_rows(scores[ci]), v_rows))

        st = [st_scr[0], st_scr[1]]
        for ci, (d, _) in enumerate(chains):
            qh = q[ci] * e[ci][GLA_ROW_PRE:GLA_ROW_SUF]
            o_rows = o_intra[ci] + _dot(per_head_rows(qh), st[d].astype(BF16))
            oacc_scr[pl.ds(r0[ci], n), :] += jnp.concatenate(
                [o_rows[h * n:(h + 1) * n, :] for h in range(GLA_HEADS)], axis=1)
            st[d] = st[d] * kt[ci][:, n:n + 1] + dstate[ci]
        st_scr[0] = st[0]
        st_scr[1] = st[1]
        return carry

    lax.fori_loop(0, nc // steps, scan_iter, 0)

    tfin = _pick(seq_t, (256, 128))
    for t0 in range(0, seq_t, tfin):
        o = oacc_scr[t0:t0 + tfin, :]
        parts = [_rms(o[:, h * dv:(h + 1) * dv], gain_ref[:, h * dv:(h + 1) * dv])
                 for h in range(GLA_HEADS)]
        gate = pa_ref[0, t0:t0 + tfin, 2 * hk + hv:2 * hk + 2 * hv]
        o_ref[0, t0:t0 + tfin, :] = (jnp.concatenate(parts, axis=1) * _silu(gate)).astype(BF16)


def _gla(p, lr, gw, gb, gain, consts, *, ctx_len, dk, dv):
    b, t, _ = p.shape
    hk = GLA_HEADS * dk
    hv = GLA_HEADS * dv
    pa_w = 2 * hk + 2 * hv
    cs, qm, km, sm = consts
    kern = functools.partial(_gla_kernel, seq_t=t, ctx_len=ctx_len, dk=dk, dv=dv)
    const = lambda a: _resident(a.shape)
    return pl.pallas_call(
        kern,
        grid=(b,),
        in_specs=[pl.BlockSpec((1, t, pa_w), lambda bi: (bi, 0, 0)),
                  pl.BlockSpec((1, t, LANES), lambda bi: (bi, 0, 0)),
                  const(gw), const(gb), const(gain), const(cs), const(qm), const(km), const(sm)],
        out_specs=pl.BlockSpec((1, t, hv), lambda bi: (bi, 0, 0)),
        out_shape=jax.ShapeDtypeStruct((b, t, hv), BF16),
        scratch_shapes=[pltpu.VMEM((t, 2 * hk), F32), pltpu.VMEM((t, hv), F32),
                        pltpu.VMEM((2, hk, dv), F32)],
        compiler_params=_cparams(1),
        name="gla",
    )(p, lr, gw, gb, gain, cs, qm, km, sm)


def _rope_tables(length, ctx_len, n_heads):
    rows = length // GRID_W
    row = jnp.repeat(jnp.arange(rows), GRID_W).astype(F32)
    col = jnp.tile(jnp.arange(GRID_W), rows).astype(F32)
    quarter = HEAD_DIM // 4
    inv = ROPE_THETA ** (-jnp.arange(quarter, dtype=F32) / quarter)
    ar, ac = row[:, None] * inv, col[:, None] * inv
    ang = jnp.concatenate([ar, ar, ac, ac], axis=-1)
    cos, sin = jnp.cos(ang), jnp.sin(ang)
    first = (jnp.arange(HEAD_DIM) % (HEAD_DIM // 2)) < quarter
    sin_signed = jnp.where(first[None, :], -sin, sin)
    cos = jnp.concatenate([jnp.ones((ctx_len, HEAD_DIM), F32), cos], axis=0)
    sin_signed = jnp.concatenate([jnp.zeros((ctx_len, HEAD_DIM), F32), sin_signed], axis=0)
    return jnp.tile(cos, (1, n_heads)), jnp.tile(sin_signed, (1, n_heads))


def _pair_heads(a, axis, n_kv):
    axis = axis % a.ndim
    g = a.shape[axis] // (n_kv * HEAD_DIM)
    shape = a.shape[:axis] + (n_kv, g, HEAD_DIM) + a.shape[axis + 1:]
    return jnp.swapaxes(a.reshape(shape), axis, axis + 1).reshape(a.shape)


def _block_diag_mask(rows, cols, rb, cb):
    r = np.arange(rows)[:, None] // rb
    c = np.arange(cols)[None, :] // cb
    return (r == c).astype(np.float32)


def kernel(x, c, ctx, c_ctx, w_mod, b_mod, norm_g, w_in, gla_gate_up, gla_gate_b, win_sink, cm_ln_g,
           cm_ln_b, cm_ws, cm_bs, qk_g, mix_g, w_out, w_ffn_in, w_ffn_out):
    bsz, seq, d = x.shape
    ctx_len = ctx.shape[1]
    seq_t = ctx_len + seq
    depth = w_mod.shape[0]
    gw = d // 4
    rank = gla_gate_up.shape[2]
    dk = gla_gate_up.shape[3] // GLA_HEADS
    dv = gw // GLA_HEADS
    n_heads = gw // HEAD_DIM
    kvw = WIN_KV * HEAD_DIM
    hk = GLA_HEADS * dk
    assert kvw == LANES and GA_KV == WIN_KV and gw % LANES == 0 and 2 * rank <= LANES
    assert dk == GLA_CHUNK and 2 * hk == gw
    assert ctx_len % CM_CHUNK == 0 and seq % CM_CHUNK == 0 and seq % GRID_W == 0
    assert ctx_len % ROW_CHUNK == 0

    sizes = [hk, hk, gw, gw, rank, rank, gw, kvw, kvw, gw, gw, gw, kvw, kvw]
    off = [int(v) for v in np.concatenate([[0], np.cumsum(sizes)])]
    blocks = {"c_u": 3, "c_v": 4, "b_q": 5, "d_q": 6,
              "b_kv": (7 * gw) // (2 * LANES), "d_kv": (7 * gw) // (2 * LANES) + 1}

    cos_q, sin_q = _rope_tables(seq, ctx_len, n_heads)
    cos_k, sin_k = cos_q[:, :LANES], sin_q[:, :LANES]
    ones_q = jnp.asarray(_block_diag_mask(gw, gw, HEAD_DIM, HEAD_DIM), BF16)
    ones_k = ones_q[:LANES, :LANES]
    tabs = (cos_q, sin_q, cos_k, sin_k, ones_q, ones_k)

    csum, qmask, kmask, smask = _gla_constants(dk)
    gla_consts = (jnp.asarray(csum, BF16), jnp.asarray(qmask), jnp.asarray(kmask), jnp.asarray(smask))

    rows = 16
    cond = jnp.concatenate([c, c_ctx[None, :], jnp.zeros((rows - bsz - 1, d), F32)], axis=0)
    mod_all = _modulation(cond, w_mod, b_mod)
    mod_lat = mod_all[:, :bsz].reshape(depth, bsz, 6, d)
    mod_ctx = jnp.broadcast_to(mod_all[:, bsz].reshape(depth, 1, 6, d), (depth, bsz, 6, d))
    pad2 = jnp.zeros((depth, bsz, 2, d), F32)
    mods = jnp.concatenate([mod_ctx, pad2, mod_lat, pad2], axis=2)

    xs = jnp.concatenate([ctx, x], axis=1).reshape(bsz * seq_t, d)

    w_main, w_lr = _prep_w_in(w_in, off, rank)
    zeros_k = jnp.zeros((depth, rank, hk), F32)
    gate_w = jnp.concatenate(
        [jnp.concatenate([gla_gate_up[:, 0], zeros_k], axis=2),
         jnp.concatenate([zeros_k, gla_gate_up[:, 1]], axis=2),
         jnp.zeros((depth, LANES - 2 * rank, 2 * hk), F32)], axis=1).astype(BF16)
    gate_b = gla_gate_b.reshape(depth, 1, 2 * hk)
    g_a, g_c = mix_g[:, None, 0:gw], mix_g[:, None, 2 * gw:3 * gw]
    g_b = _pair_heads(mix_g[:, None, gw:2 * gw], 2, WIN_KV)
    g_d = _pair_heads(mix_g[:, None, 3 * gw:], 2, GA_KV)
    w_o = jnp.concatenate([w_out[:, 0:gw], _pair_heads(w_out[:, gw:2 * gw], 1, WIN_KV),
                           w_out[:, 2 * gw:3 * gw], _pair_heads(w_out[:, 3 * gw:], 1, GA_KV)],
                          axis=1).astype(BF16)
    sink = _pair_heads(jnp.repeat(win_sink, HEAD_DIM, axis=1)[:, None, :], 2, WIN_KV)
    q_gain = jnp.tile(qk_g[:, 0:1], (1, 1, n_heads))
    k_gain = jnp.tile(qk_g[:, 1:2], (1, 1, WIN_KV))
    bsb = jnp.repeat(jnp.swapaxes(cm_bs, 1, 2), gw // CM_GROUPS, axis=2)
    ws = cm_ws.astype(BF16)
    w_up, w_down = w_ffn_in.astype(BF16), w_ffn_out.astype(BF16)
    ln_g, ln_b = cm_ln_g[:, None, :], cm_ln_b[:, None, :]

    for l in range(depth):
        modl = mods[l]
        p, lr = _inproj(xs, modl, norm_g[l, 0:1], w_main, w_lr, l, seq_t=seq_t, ctx_len=ctx_len)
        p3 = p.reshape(bsz, seq_t, -1)
        lr3 = lr.reshape(bsz, seq_t, LANES)
        y_a = _gla(p3, lr3, gate_w[l], gate_b[l], g_a[l], gla_consts, ctx_len=ctx_len, dk=dk, dv=dv)
        y_b, y_c, y_d = _mixers(p3, tabs, q_gain[l], k_gain[l], sink[l], g_b[l], g_d[l], ln_g[l], ln_b[l],
                                ws[l], bsb[l], g_c[l], blocks=blocks, ctx_len=ctx_len)
        ys = [y.reshape(bsz * seq_t, gw) for y in (y_a, y_b, y_c, y_d)]
        last = l == depth - 1
        xs = _outproj(xs, ys, modl, norm_g[l, 1:2], w_o[l], seq_t=seq_t, ctx_len=ctx_len,
                      latent_only=last)
        xs = _ffn(xs, modl, norm_g[l, 2:3], norm_g[l, 3:4], w_up[l], w_down[l],
                  rows_per_batch=seq if last else seq_t, row_base=ctx_len if last else 0,
                  ctx_len=ctx_len)

    return xs.reshape(bsz, seq, d)
```

```python
import functools
import math

import numpy as np
import jax
import jax.numpy as jnp
from jax import lax
from jax.experimental import pallas as pl
from jax.experimental.pallas import tpu as pltpu

F32 = jnp.float32
BF16 = jnp.bfloat16

EPS = 1e-6
HEAD_DIM = 64
ATTN_SCALE = HEAD_DIM ** -0.5
ROPE_THETA = 10000.0
GRID_W = 64
GLA_HEADS = 4
GLA_TAU = 16.0
GLA_CHUNK = 64
WIN_KV = 2
WINDOW = 128
CM_GROUPS = 4
CM_CHUNK = 128
GA_KV = 2

LANES = 128
NEG_BIG = -1e30
V7X_VMEM_LIMIT = 56 * 1024 * 1024
ROW_CHUNK = 64


def _cparams(n_axes, vmem=V7X_VMEM_LIMIT, flags=None):
    return pltpu.CompilerParams(dimension_semantics=("arbitrary",) * n_axes,
                                vmem_limit_bytes=vmem, flags=flags)


def _pick(total, candidates):
    for cand in candidates:
        if total % cand == 0:
            return cand
    raise ValueError(f"no tile in {candidates} divides {total}")


def _resident(shape):
    return pl.BlockSpec(shape, lambda *_: (0,) * len(shape), pipeline_mode=pl.Buffered(1))


def _dot(a, b):
    return jnp.dot(a, b, preferred_element_type=F32)


def _dot_nt(a, b):
    return lax.dot_general(a, b, (((1,), (1,)), ((), ())), preferred_element_type=F32)


def _rms(x, gain):
    return x * lax.rsqrt(jnp.mean(x * x, axis=-1, keepdims=True) + EPS) * gain


def _silu(x):
    return x * jax.nn.sigmoid(x)


def _gelu_tanh(x):
    return 0.5 * x * (1.0 + jnp.tanh(math.sqrt(2.0 / math.pi) * (x + 0.044715 * (x * x * x))))


def _mod_vec(mod_ref, k, chunk_is_ctx):
    return jnp.where(chunk_is_ctx, mod_ref[0, k:k + 1, :], mod_ref[0, 8 + k:9 + k, :])


def _row_loop(n_rows, step):
    n_chunks = n_rows // ROW_CHUNK
    lax.fori_loop(0, n_chunks, step, 0, unroll=3 if n_chunks % 3 == 0 else 2)


def _norm_modulate_rows(x_ref, h_ref, g_ref, mod_ref, k_shift, k_scale, row0, ctx_len):
    def step(c, carry):
        r = pl.multiple_of(c * ROW_CHUNK, ROW_CHUNK)
        is_ctx = row0 + r < ctx_len
        gain = g_ref[...] * (1.0 + _mod_vec(mod_ref, k_scale, is_ctx))
        x = x_ref[pl.ds(r, ROW_CHUNK), :]
        inv = lax.rsqrt(jnp.mean(x * x, axis=-1, keepdims=True) + EPS)
        h_ref[pl.ds(r, ROW_CHUNK), :] = (x * inv * gain + _mod_vec(mod_ref, k_shift, is_ctx)).astype(BF16)
        return carry

    _row_loop(x_ref.shape[0], step)


def _gated_residual_rows(x_ref, z_ref, o_ref, g_ref, mod_ref, k_gate, row0, ctx_len):
    def step(c, carry):
        r = pl.multiple_of(c * ROW_CHUNK, ROW_CHUNK)
        gain = g_ref[...] * _mod_vec(mod_ref, k_gate, row0 + r < ctx_len)
        z = z_ref[pl.ds(r, ROW_CHUNK), :]
        inv = lax.rsqrt(jnp.mean(z * z, axis=-1, keepdims=True) + EPS)
        o_ref[pl.ds(r, ROW_CHUNK), :] = x_ref[pl.ds(r, ROW_CHUNK), :] + z * inv * gain
        return carry

    _row_loop(x_ref.shape[0], step)


def _mod_kernel(c_ref, w_ref, b_ref, o_ref):
    s = _silu(c_ref[...]).astype(BF16)
    o_ref[0] = _dot(s, w_ref[0].astype(BF16)) + b_ref[0]


def _modulation(cond, w_mod, b_mod):
    depth, d, n = w_mod.shape
    rows = cond.shape[0]
    tn = _pick(n, (1024, 512, 256, 128))
    return pl.pallas_call(
        _mod_kernel,
        grid=(depth, n // tn),
        in_specs=[pl.BlockSpec((rows, d), lambda l, j: (0, 0)),
                  pl.BlockSpec((1, d, tn), lambda l, j: (l, 0, j)),
                  pl.BlockSpec((1, 1, tn), lambda l, j: (l, 0, j))],
        out_specs=pl.BlockSpec((1, rows, tn), lambda l, j: (l, 0, j)),
        out_shape=jax.ShapeDtypeStruct((depth, rows, n), F32),
        compiler_params=_cparams(2),
        name="modulation",
    )(cond, w_mod, b_mod.reshape(depth, 1, n))


def _pair_lanes(y, n_kv):
    g = y.shape[1] // (n_kv * HEAD_DIM)
    cols = [y[:, (kv * g + j) * HEAD_DIM:(kv * g + j + 1) * HEAD_DIM] for j in range(g) for kv in range(n_kv)]
    return jnp.concatenate(cols, axis=1)


def _w_in_kernel(w_ref, main_ref, lr_ref, *, off, rank):
    x = w_ref[0]
    main = jnp.concatenate(
        [x[:, off[0]:off[4]], x[:, off[9]:off[11]],
         _pair_lanes(x[:, off[6]:off[7]], WIN_KV), _pair_lanes(x[:, off[11]:off[12]], GA_KV),
         x[:, off[7]:off[9]], x[:, off[12]:off[14]]], axis=1)
    main_ref[0] = main.astype(BF16)
    pad = jnp.zeros((x.shape[0], LANES - 2 * rank), F32)
    lr_ref[0] = jnp.concatenate([x[:, off[4]:off[6]], pad], axis=1).astype(BF16)


def _prep_w_in(w_in, off, rank):
    depth, d, n_in = w_in.shape
    n_main = n_in - 2 * rank
    rb = _pick(d, (256, 128))
    kern = functools.partial(_w_in_kernel, off=tuple(off), rank=rank)
    return pl.pallas_call(
        kern,
        grid=(depth, d // rb),
        in_specs=[pl.BlockSpec((1, rb, n_in), lambda l, i: (l, i, 0))],
        out_specs=[pl.BlockSpec((1, rb, n_main), lambda l, i: (l, i, 0)),
                   pl.BlockSpec((1, rb, LANES), lambda l, i: (l, i, 0))],
        out_shape=[jax.ShapeDtypeStruct((depth, d, n_main), BF16),
                   jax.ShapeDtypeStruct((depth, d, LANES), BF16)],
        compiler_params=_cparams(2),
        name="prep_w_in",
    )(w_in)


def _inproj_kernel(x_ref, mod_ref, g_ref, w_ref, wlr_ref, p_ref, lr_ref, h_scr, *,
                   tm, tiles_per_batch, ctx_len):
    row0 = (pl.program_id(0) % tiles_per_batch) * tm
    _norm_modulate_rows(x_ref, h_scr, g_ref, mod_ref, 0, 1, row0, ctx_len)
    h = h_scr[...]
    lr_ref[...] = _dot(h, wlr_ref[0])
    p_ref[...] = _dot(h, w_ref[0])


def _inproj(xs, modl, gain, w_main, w_lr, layer, *, seq_t, ctx_len):
    m, d = xs.shape
    n = w_main.shape[2]
    tm = _pick(seq_t, (576, 384, 128))
    tpb = seq_t // tm
    kern = functools.partial(_inproj_kernel, tm=tm, tiles_per_batch=tpb, ctx_len=ctx_len)
    layer_block = lambda width: pl.BlockSpec((1, d, width), lambda i: (layer, 0, 0),
                                             pipeline_mode=pl.Buffered(1))
    return pl.pallas_call(
        kern,
        grid=(m // tm,),
        in_specs=[pl.BlockSpec((tm, d), lambda i: (i, 0)),
                  pl.BlockSpec((1, 16, d), lambda i: (i // tpb, 0, 0)),
                  _resident((1, d)), layer_block(n), layer_block(LANES)],
        out_specs=[pl.BlockSpec((tm, n), lambda i: (i, 0)),
                   pl.BlockSpec((tm, LANES), lambda i: (i, 0))],
        out_shape=[jax.ShapeDtypeStruct((m, n), F32),
                   jax.ShapeDtypeStruct((m, LANES), F32)],
        scratch_shapes=[pltpu.VMEM((tm, d), BF16)],
        compiler_params=_cparams(1),
        name="inproj",
    )(xs, modl, gain, w_main, w_lr)


def _outproj_kernel(x_ref, ya_ref, yb_ref, yc_ref, yd_ref, mod_ref, g_ref, w_ref, o_ref, y_scr, *,
                    tm, tiles_per_batch, row_base, ctx_len, gw):
    i = pl.program_id(0)
    y_scr[:, 0 * gw:1 * gw] = ya_ref[...]
    y_scr[:, 1 * gw:2 * gw] = yb_ref[...]
    y_scr[:, 2 * gw:3 * gw] = yc_ref[...]
    y_scr[:, 3 * gw:4 * gw] = yd_ref[...]
    zn = _rms(_dot(y_scr[...], w_ref[0]), g_ref[...])
    row0 = row_base + (i % tiles_per_batch) * tm

    @pl.when(row0 >= ctx_len)
    def _():
        o_ref[...] = x_ref[...] + mod_ref[0, 10:11, :] * zn

    @pl.when(row0 < ctx_len)
    def _():
        is_ctx = row0 + lax.broadcasted_iota(jnp.int32, (tm, 1), 0) < ctx_len
        o_ref[...] = x_ref[...] + jnp.where(is_ctx, mod_ref[0, 2:3, :], mod_ref[0, 10:11, :]) * zn


def _outproj(xs, ys, modl, gain, w, layer, *, seq_t, ctx_len, latent_only):
    m, d = xs.shape
    gw = ys[0].shape[1]
    if latent_only:
        seq = seq_t - ctx_len
        tm = _pick(seq, (512, 256, 128))
        tpb = seq // tm
        n_tiles = (m // seq_t) * tpb
        align = math.gcd(seq_t, ctx_len, tm)
        row_start = lambda i: pl.multiple_of((i // tpb) * seq_t + ctx_len + (i % tpb) * tm, align)
        xspec = pl.BlockSpec((pl.Element(tm), pl.Element(d)), lambda i: (row_start(i), 0))
        yspec = pl.BlockSpec((pl.Element(tm), pl.Element(gw)), lambda i: (row_start(i), 0))
        row_base = ctx_len
    else:
        tm = _pick(seq_t, (576, 384, 128))
        tpb = seq_t // tm
        n_tiles = m // tm
        xspec = pl.BlockSpec((tm, d), lambda i: (i, 0))
        yspec = pl.BlockSpec((tm, gw), lambda i: (i, 0))
        row_base = 0
    kern = functools.partial(_outproj_kernel, tm=tm, tiles_per_batch=tpb, row_base=row_base,
                             ctx_len=ctx_len, gw=gw)
    return pl.pallas_call(
        kern,
        grid=(n_tiles,),
        in_specs=[xspec, yspec, yspec, yspec, yspec,
                  pl.BlockSpec((1, 16, d), lambda i: (i // tpb, 0, 0)),
                  _resident((1, d)),
                  pl.BlockSpec((1, 4 * gw, d), lambda i: (layer, 0, 0), pipeline_mode=pl.Buffered(1))],
        out_specs=pl.BlockSpec((tm, d), lambda i: (i, 0)),
        out_shape=jax.ShapeDtypeStruct((n_tiles * tm, d), F32),
        scratch_shapes=[pltpu.VMEM((tm, 4 * gw), BF16)],
        compiler_params=_cparams(1),
        name="outproj",
    )(xs, *ys, modl, gain, w)


def _ffn_kernel(x_ref, mod_ref, g2_ref, g3_ref, wg_ref, wu_ref, wo_ref, o_ref, h_scr, acc_scr, *,
                tm, tiles_per_batch, row_base, ctx_len, nf):
    i = pl.program_id(0)
    j = pl.program_id(1)
    row0 = row_base + (i % tiles_per_batch) * tm

    @pl.when(j == 0)
    def _():
        _norm_modulate_rows(x_ref, h_scr, g2_ref, mod_ref, 3, 4, row0, ctx_len)
        acc_scr[...] = jnp.zeros_like(acc_scr)

    h = h_scr[...]
    act = (_silu(_dot(h, wg_ref[0])) * _dot(h, wu_ref[0])).astype(BF16)
    acc_scr[...] += _dot(act, wo_ref[0])

    @pl.when(j == nf - 1)
    def _():
        _gated_residual_rows(x_ref, acc_scr, o_ref, g3_ref, mod_ref, 5, row0, ctx_len)


def _ffn(xs, modl, g2, g3, w_in, w_out, layer, *, rows_per_batch, row_base, ctx_len):
    m, d = xs.shape
    dff = w_out.shape[1]
    tm = _pick(rows_per_batch, (768, 512, 384, 128))
    tf = _pick(dff, (512, 256, 128))
    nf = dff // tf
    tpb = rows_per_batch // tm
    kern = functools.partial(_ffn_kernel, tm=tm, tiles_per_batch=tpb, row_base=row_base,
                             ctx_len=ctx_len, nf=nf)
    return pl.pallas_call(
        kern,
        grid=(m // tm, nf),
        in_specs=[pl.BlockSpec((tm, d), lambda i, j: (i, 0)),
                  pl.BlockSpec((1, 16, d), lambda i, j: (i // tpb, 0, 0)),
                  _resident((1, d)), _resident((1, d)),
                  pl.BlockSpec((1, d, tf), lambda i, j: (layer, 0, j)),
                  pl.BlockSpec((1, d, tf), lambda i, j: (layer, 0, nf + j)),
                  pl.BlockSpec((1, tf, d), lambda i, j: (layer, j, 0))],
        out_specs=pl.BlockSpec((tm, d), lambda i, j: (i, 0)),
        out_shape=jax.ShapeDtypeStruct((m, d), F32),
        scratch_shapes=[pltpu.VMEM((tm, d), BF16), pltpu.VMEM((tm, d), F32)],
        compiler_params=_cparams(2),
        name="ffn",
    )(xs, modl, g2, g3, w_in, w_in, w_out)


def _cmlp_tile(u_ref, v_ref, lng_ref, lnb_ref, ws_ref, bsb_ref, gc_ref, o_ref):
    tc, gw = u_ref.shape[1], u_ref.shape[2]
    cw = gw // CM_GROUPS
    u = _gelu_tanh(u_ref[0])
    v = _gelu_tanh(v_ref[0])
    vc = v - jnp.mean(v, axis=-1, keepdims=True)
    vn = vc * lax.rsqrt(jnp.mean(vc * vc, axis=-1, keepdims=True) + EPS) * lng_ref[...] + lnb_ref[...]
    vb = vn.astype(BF16)
    rows = []
    for ch in range(tc // CM_CHUNK):
        cols = []
        for g in range(CM_GROUPS):
            blk = vb[ch * CM_CHUNK:(ch + 1) * CM_CHUNK, g * cw:(g + 1) * cw]
            cols.append(_dot(ws_ref[g], blk) + bsb_ref[:, g * cw:(g + 1) * cw])
        rows.append(jnp.concatenate(cols, axis=1))
    s = jnp.concatenate(rows, axis=0)
    o_ref[0] = _rms(u * s, gc_ref[...]).astype(BF16)


def _group_mean_sq(x, ones_bd):
    sq = x * x
    hi = sq.astype(BF16)
    lo = (sq - hi.astype(F32)).astype(BF16)
    return (_dot(hi, ones_bd) + _dot(lo, ones_bd)) * (1.0 / HEAD_DIM)


def _rope(x, cos, sin_signed):
    w = x.shape[1]
    lane = lax.broadcasted_iota(jnp.int32, x.shape, 1)
    first = (lane % (HEAD_DIM // 2)) < (HEAD_DIM // 4)
    partner = jnp.where(first, pltpu.roll(x, w - HEAD_DIM // 4, 1), pltpu.roll(x, HEAD_DIM // 4, 1))
    return x * cos + partner * sin_signed


def _attn_pairs(q, k_scr, v_scr, sink_ref, pairs, *, mode, latent_tile, i, tq, seq_t, ctx_len):
    lo = lax.broadcasted_iota(jnp.int32, (tq, LANES), 1) < HEAD_DIM
    lane1 = lax.broadcasted_iota(jnp.int32, (1, LANES), 1)
    row2 = lax.broadcasted_iota(jnp.int32, (2 * tq, 1), 0)

    key_sets = [(0, ctx_len, None)]
    if latent_tile and mode == "window":
        wk = tq + 2 * WINDOW
        start = pl.multiple_of(jnp.clip(i * tq - WINDOW, ctx_len, seq_t - wk), LANES)
        r = lax.broadcasted_iota(jnp.int32, (2 * tq, wk), 0)
        r = jnp.where(r >= tq, r - tq, r)
        c = lax.broadcasted_iota(jnp.int32, (2 * tq, wk), 1)
        dist = (i * tq + r) - (start + c)
        key_sets.append((start, wk, (jnp.abs(dist) <= WINDOW)))
    elif latent_tile:
        key_sets = [(0, seq_t, None)]

    outs = []
    for j in pairs:
        qp = q[:, j * LANES:(j + 1) * LANES]
        q2 = jnp.concatenate([jnp.where(lo, qp, 0.0), jnp.where(lo, 0.0, qp)], axis=0).astype(BF16)
        scores = []
        for k0, kn, valid in key_sets:
            s = _dot_nt(q2, k_scr[pl.ds(k0, kn), :])
            scores.append(s if valid is None else jnp.where(valid, s, NEG_BIG))
        mx = functools.reduce(jnp.maximum, [jnp.max(s, axis=1, keepdims=True) for s in scores])
        if mode == "window":
            sv = sink_ref[:, j * LANES:(j + 1) * LANES]
            s_lo = jnp.max(jnp.where(lane1 < HEAD_DIM, sv, NEG_BIG), axis=1, keepdims=True)
            s_hi = jnp.max(jnp.where(lane1 < HEAD_DIM, NEG_BIG, sv), axis=1, keepdims=True)
            sk = jnp.where(row2 < tq, s_lo, s_hi)
            mx = jnp.maximum(mx, sk)
            den = jnp.exp(sk - mx)
        else:
            den = 0.0
        o2 = 0.0
        for (k0, kn, _), s in zip(key_sets, scores):
            p = jnp.exp(s - mx)
            den = den + jnp.sum(p, axis=1, keepdims=True)
            o2 = o2 + _dot(p.astype(BF16), v_scr[pl.ds(k0, kn), :])
        o2 = o2 / den
        outs.append(jnp.where(lo, o2[:tq], o2[tq:]))
    return outs


def _mixers_kernel(bq_ref, dq_ref, cu_ref, cv_ref, bkv_ref, dkv_ref, cq_ref, sq_ref, ck_ref, sk_ref,
                   qg_ref, kg_ref, onesq_ref, onesk_ref, sink_ref, gb_ref, gd_ref,
                   lng_ref, lnb_ref, ws_ref, bsb_ref, gc_ref,
                   yb_ref, yc_ref, yd_ref, kb_scr, vb_scr, kd_scr, vd_scr, *, tq, seq_t, ctx_len):
    i = pl.program_id(1)
    n_ctx_tiles = ctx_len // tq

    @pl.when(i == 0)
    def _():
        kv = bkv_ref[0]
        kb_scr[...] = _rope(kv[:, :LANES], ck_ref[...], sk_ref[...]).astype(BF16)
        vb_scr[...] = kv[:, LANES:].astype(BF16)
        kv = dkv_ref[0]
        k = kv[:, :LANES]
        k = k * lax.rsqrt(_group_mean_sq(k, onesk_ref[...]) + EPS) * kg_ref[...]
        kd_scr[...] = _rope(k, ck_ref[...], sk_ref[...]).astype(BF16)
        vd_scr[...] = kv[:, LANES:].astype(BF16)

    qb = _rope(bq_ref[0], cq_ref[...], sq_ref[...]) * ATTN_SCALE
    qd = dq_ref[0]
    qd = qd * lax.rsqrt(_group_mean_sq(qd, onesq_ref[...]) + EPS) * qg_ref[...]
    qd = _rope(qd, cq_ref[...], sq_ref[...]) * ATTN_SCALE

    n_pairs = qb.shape[1] // LANES
    half = n_pairs // 2

    def attend(latent_tile):
        common = dict(latent_tile=latent_tile, i=i, tq=tq, seq_t=seq_t, ctx_len=ctx_len)
        win = functools.partial(_attn_pairs, qb, kb_scr, vb_scr, sink_ref, mode="window", **common)
        ob = win(range(half))
        od = _attn_pairs(qd, kd_scr, vd_scr, None, range(n_pairs), mode="global", **common)
        ob = ob + win(range(half, n_pairs))
        yb_ref[0] = _rms(jnp.concatenate(ob, axis=1), gb_ref[...]).astype(BF16)
        yd_ref[0] = _rms(jnp.concatenate(od, axis=1), gd_ref[...]).astype(BF16)

    pl.when(i < n_ctx_tiles)(functools.partial(attend, False))
    pl.when(i >= n_ctx_tiles)(functools.partial(attend, True))
    _cmlp_tile(cu_ref, cv_ref, lng_ref, lnb_ref, ws_ref, bsb_ref, gc_ref, yc_ref)


def _mixers(p, tabs, qg, kg, sink, gb, gd, lng, lnb, ws, bsb, gc, *, blocks, ctx_len):
    b, t, _ = p.shape
    gw = gb.shape[1]
    tq = _pick(t, (256, 128))
    assert ctx_len % tq == 0 and tq % CM_CHUNK == 0
    cq, sq, ck, sk, ones_q, ones_k = tabs
    kern = functools.partial(_mixers_kernel, tq=tq, seq_t=t, ctx_len=ctx_len)
    const = lambda a: _resident(a.shape)
    qspec = lambda blk: pl.BlockSpec((1, tq, gw), lambda bi, i: (bi, i, blk))
    kvspec = lambda blk: pl.BlockSpec((1, t, 2 * LANES), lambda bi, i: (bi, 0, blk))
    tabspec = pl.BlockSpec((tq, gw), lambda bi, i: (i, 0))
    out = jax.ShapeDtypeStruct((b, t, gw), BF16)
    ospec = pl.BlockSpec((1, tq, gw), lambda bi, i: (bi, i, 0))
    kvscr = pltpu.VMEM((t, LANES), BF16)
    return pl.pallas_call(
        kern,
        grid=(b, t // tq),
        in_specs=[qspec(blocks["b_q"]), qspec(blocks["d_q"]), qspec(blocks["c_u"]), qspec(blocks["c_v"]),
                  kvspec(blocks["b_kv"]), kvspec(blocks["d_kv"]),
                  tabspec, tabspec, const(ck), const(sk), const(qg), const(kg),
                  const(ones_q), const(ones_k), const(sink), const(gb), const(gd),
                  const(lng), const(lnb), const(ws), const(bsb), const(gc)],
        out_specs=[ospec, ospec, ospec],
        out_shape=[out, out, out],
        scratch_shapes=[kvscr, kvscr, kvscr, kvscr],
        compiler_params=_cparams(2),
        name="mixers_bcd",
    )(p, p, p, p, p, p, cq, sq, ck, sk, qg, kg, ones_q, ones_k, sink, gb, gd, lng, lnb, ws, bsb, gc)


GLA_LEVELS = int(math.log2(GLA_CHUNK))
GLA_ROW_PRE = GLA_LEVELS * GLA_CHUNK
GLA_ROW_SUF = GLA_ROW_PRE + GLA_CHUNK
GLA_ROW_LAST = GLA_ROW_SUF + GLA_CHUNK
GLA_ROWS = GLA_ROW_LAST + 16
GLA_STEPS_PER_ITER = 2


def _gla_constants(dk):
    n = GLA_CHUNK
    csum = np.zeros((2, GLA_ROWS, n), np.float32)
    qmask = np.zeros((2, GLA_LEVELS + 1, n, 1), np.float32)
    kmask = np.zeros((2, GLA_LEVELS + 1, n, 1), np.float32)
    smask = np.zeros((2, GLA_LEVELS + 1, n, n), np.float32)
    for d in range(2):
        tau = np.arange(n) if d == 0 else n - 1 - np.arange(n)
        qmask[d, 0] = 1.0
        kmask[d, 0] = 1.0
        smask[d, 0] = np.eye(n)
        for lv in range(1, GLA_LEVELS + 1):
            h = 1 << (lv - 1)
            blk = tau // (2 * h)
            upper = (tau % (2 * h)) >= h
            same = blk[:, None] == blk[None, :]
            both_up = upper[:, None] & upper[None, :]
            both_lo = (~upper[:, None]) & (~upper[None, :])
            c = np.where(upper[:, None],
                         same & both_up & (tau[None, :] <= tau[:, None]),
                         same & both_lo & (tau[None, :] > tau[:, None]))
            csum[d, (lv - 1) * n:lv * n] = c
            qmask[d, lv, :, 0] = upper
            kmask[d, lv, :, 0] = ~upper
            smask[d, lv] = same & upper[:, None] & (~upper[None, :])
        csum[d, GLA_ROW_PRE:GLA_ROW_SUF] = tau[None, :] <= tau[:, None]
        csum[d, GLA_ROW_SUF:GLA_ROW_LAST] = tau[None, :] > tau[:, None]
        csum[d, GLA_ROW_LAST:] = 1.0
    hk = GLA_HEADS * dk
    qmask = np.broadcast_to(qmask, (2, GLA_LEVELS + 1, n, hk)).copy()
    kmask = np.broadcast_to(kmask, (2, GLA_LEVELS + 1, n, hk)).copy()
    smask = np.tile(smask, (1, 1, 1, GLA_HEADS))
    return csum, qmask, kmask, smask


def _gla_kernel(pa_ref, lr_ref, gw_ref, gb_ref, gain_ref, cs_ref, qm_ref, km_ref, sm_ref,
                o_ref, la_scr, oacc_scr, st_scr, *, seq_t, ctx_len, dk, dv):
    hk = GLA_HEADS * dk
    hv = GLA_HEADS * dv
    n = GLA_CHUNK
    nc = seq_t // n
    nc_ctx = ctx_len // n

    z = _dot(lr_ref[0].astype(BF16), gw_ref[...]) + gb_ref[...]
    la_scr[...] = (jnp.minimum(z, 0.0) - jnp.log1p(jnp.exp(-jnp.abs(z)))) * (1.0 / GLA_TAU)
    oacc_scr[...] = jnp.zeros_like(oacc_scr)
    st_scr[...] = jnp.zeros_like(st_scr)

    lane_head = lax.broadcasted_iota(jnp.int32, (n, hk), 1) // dk
    head_lanes = [lane_head == h for h in range(GLA_HEADS)]

    def per_head_rows(x):
        return jnp.concatenate([jnp.where(m, x, 0.0) for m in head_lanes], axis=0).astype(BF16)

    steps = GLA_STEPS_PER_ITER
    assert nc % steps == 0
    chains = [(d, u) for u in range(steps) for d in (0, 1)]

    def scan_iter(it, carry):
        r0 = []
        for d, u in chains:
            s = it * steps + u
            c = s if d == 0 else jnp.where(s < nc_ctx, nc_ctx - 1 - s, nc - 1 - (s - nc_ctx))
            r0.append(pl.multiple_of(c * n, n))
        q = [pa_ref[0, pl.ds(r, n), 0:hk] * (dk ** -0.5) for r in r0]
        k = [pa_ref[0, pl.ds(r, n), hk:2 * hk] for r in r0]
        v = [pa_ref[0, pl.ds(r, n), 2 * hk:2 * hk + hv] for r in r0]
        ex2 = []
        for (d, _), r in zip(chains, r0):
            a = la_scr[pl.ds(r, n), d * hk:(d + 1) * hk]
            a_hi = a.astype(BF16)
            a_lo = (a - a_hi.astype(F32)).astype(BF16)
            ex2.append(_dot(cs_ref[d], jnp.concatenate([a_hi, a_lo], axis=1)))
        e = [jnp.exp(x[:, :hk] + x[:, hk:]) for x in ex2]

        scores = [jnp.zeros((n, GLA_HEADS * n), F32) for _ in chains]
        for lv in range(GLA_LEVELS + 1):
            for ci, (d, _) in enumerate(chains):
                if lv == 0:
                    ql, kl = q[ci], k[ci]
                else:
                    el = e[ci][(lv - 1) * n:lv * n]
                    ql = q[ci] * el * qm_ref[d, lv]
                    kl = k[ci] * el * km_ref[d, lv]
                scores[ci] = scores[ci] + _dot_nt(ql.astype(BF16), per_head_rows(kl)) * sm_ref[d, lv]

        kt = []
        for ci in range(len(chains)):
            kh = k[ci] * e[ci][GLA_ROW_SUF:GLA_ROW_LAST]
            e_last = e[ci][GLA_ROW_LAST:GLA_ROW_LAST + 1]
            kt.append(jnp.concatenate([kh, jnp.broadcast_to(e_last, (n, hk))], axis=0).T)
        dstate = []
        o_intra = []
        for ci in range(len(chains)):
            vpad = jnp.concatenate([v[ci], jnp.zeros_like(v[ci])], axis=0).astype(BF16)
            full = _dot(kt[ci].astype(BF16), vpad)
            dstate.append(jnp.concatenate(
                [full[h * dk:(h + 1) * dk, h * dv:(h + 1) * dv] for h in range(GLA_HEADS)], axis=0))
            v_rows = jnp.concatenate([v[ci][:, h * dv:(h + 1) * dv] for h in range(GLA_HEADS)],
                                     axis=0).astype(BF16)
            o_intra.append(_dot(per_head_rows(scores[ci]), v_rows))

        st = [st_scr[0], st_scr[1]]
        for ci, (d, _) in enumerate(chains):
            qh = q[ci] * e[ci][GLA_ROW_PRE:GLA_ROW_SUF]
            o_rows = o_intra[ci] + _dot(per_head_rows(qh), st[d].astype(BF16))
            oacc_scr[pl.ds(r0[ci], n), :] += jnp.concatenate(
                [o_rows[h * n:(h + 1) * n, :] for h in range(GLA_HEADS)], axis=1)
            st[d] = st[d] * kt[ci][:, n:n + 1] + dstate[ci]
        st_scr[0] = st[0]
        st_scr[1] = st[1]
        return carry

    lax.fori_loop(0, nc // steps, scan_iter, 0)

    tfin = _pick(seq_t, (256, 128))
    for t0 in range(0, seq_t, tfin):
        o = oacc_scr[t0:t0 + tfin, :]
        parts = [_rms(o[:, h * dv:(h + 1) * dv], gain_ref[:, h * dv:(h + 1) * dv])
                 for h in range(GLA_HEADS)]
        gate = pa_ref[0, t0:t0 + tfin, 2 * hk + hv:2 * hk + 2 * hv]
        o_ref[0, t0:t0 + tfin, :] = (jnp.concatenate(parts, axis=1) * _silu(gate)).astype(BF16)


def _gla(p, lr, gw, gb, gain, consts, *, ctx_len, dk, dv):
    b, t, _ = p.shape
    hk = GLA_HEADS * dk
    hv = GLA_HEADS * dv
    pa_w = 2 * hk + 2 * hv
    cs, qm, km, sm = consts
    kern = functools.partial(_gla_kernel, seq_t=t, ctx_len=ctx_len, dk=dk, dv=dv)
    const = lambda a: _resident(a.shape)
    return pl.pallas_call(
        kern,
        grid=(b,),
        in_specs=[pl.BlockSpec((1, t, pa_w), lambda bi: (bi, 0, 0)),
                  pl.BlockSpec((1, t, LANES), lambda bi: (bi, 0, 0)),
                  const(gw), const(gb), const(gain), const(cs), const(qm), const(km), const(sm)],
        out_specs=pl.BlockSpec((1, t, hv), lambda bi: (bi, 0, 0)),
        out_shape=jax.ShapeDtypeStruct((b, t, hv), BF16),
        scratch_shapes=[pltpu.VMEM((t, 2 * hk), F32), pltpu.VMEM((t, hv), F32),
                        pltpu.VMEM((2, hk, dv), F32)],
        compiler_params=_cparams(1),
        name="gla",
    )(p, lr, gw, gb, gain, cs, qm, km, sm)


def _rope_tables(length, ctx_len, n_heads):
    rows = length // GRID_W
    row = jnp.repeat(jnp.arange(rows), GRID_W).astype(F32)
    col = jnp.tile(jnp.arange(GRID_W), rows).astype(F32)
    quarter = HEAD_DIM // 4
    inv = ROPE_THETA ** (-jnp.arange(quarter, dtype=F32) / quarter)
    ar, ac = row[:, None] * inv, col[:, None] * inv
    ang = jnp.concatenate([ar, ar, ac, ac], axis=-1)
    cos, sin = jnp.cos(ang), jnp.sin(ang)
    first = (jnp.arange(HEAD_DIM) % (HEAD_DIM // 2)) < quarter
    sin_signed = jnp.where(first[None, :], -sin, sin)
    cos = jnp.concatenate([jnp.ones((ctx_len, HEAD_DIM), F32), cos], axis=0)
    sin_signed = jnp.concatenate([jnp.zeros((ctx_len, HEAD_DIM), F32), sin_signed], axis=0)
    return jnp.tile(cos, (1, n_heads)), jnp.tile(sin_signed, (1, n_heads))


def _pair_heads(a, axis, n_kv):
    axis = axis % a.ndim
    g = a.shape[axis] // (n_kv * HEAD_DIM)
    shape = a.shape[:axis] + (n_kv, g, HEAD_DIM) + a.shape[axis + 1:]
    return jnp.swapaxes(a.reshape(shape), axis, axis + 1).reshape(a.shape)


def _block_diag_mask(rows, cols, rb, cb):
    r = np.arange(rows)[:, None] // rb
    c = np.arange(cols)[None, :] // cb
    return (r == c).astype(np.float32)


def kernel(x, c, ctx, c_ctx, w_mod, b_mod, norm_g, w_in, gla_gate_up, gla_gate_b, win_sink, cm_ln_g,
           cm_ln_b, cm_ws, cm_bs, qk_g, mix_g, w_out, w_ffn_in, w_ffn_out):
    bsz, seq, d = x.shape
    ctx_len = ctx.shape[1]
    seq_t = ctx_len + seq
    depth = w_mod.shape[0]
    gw = d // 4
    rank = gla_gate_up.shape[2]
    dk = gla_gate_up.shape[3] // GLA_HEADS
    dv = gw // GLA_HEADS
    n_heads = gw // HEAD_DIM
    kvw = WIN_KV * HEAD_DIM
    hk = GLA_HEADS * dk
    assert kvw == LANES and GA_KV == WIN_KV and gw % LANES == 0 and 2 * rank <= LANES
    assert dk == GLA_CHUNK and 2 * hk == gw
    assert ctx_len % CM_CHUNK == 0 and seq % CM_CHUNK == 0 and seq % GRID_W == 0
    assert ctx_len % ROW_CHUNK == 0

    sizes = [hk, hk, gw, gw, rank, rank, gw, kvw, kvw, gw, gw, gw, kvw, kvw]
    off = [int(v) for v in np.concatenate([[0], np.cumsum(sizes)])]
    blocks = {"c_u": 3, "c_v": 4, "b_q": 5, "d_q": 6,
              "b_kv": (7 * gw) // (2 * LANES), "d_kv": (7 * gw) // (2 * LANES) + 1}

    cos_q, sin_q = _rope_tables(seq, ctx_len, n_heads)
    cos_k, sin_k = cos_q[:, :LANES], sin_q[:, :LANES]
    ones_q = jnp.asarray(_block_diag_mask(gw, gw, HEAD_DIM, HEAD_DIM), BF16)
    ones_k = ones_q[:LANES, :LANES]
    tabs = (cos_q, sin_q, cos_k, sin_k, ones_q, ones_k)

    csum, qmask, kmask, smask = _gla_constants(dk)
    gla_consts = (jnp.asarray(csum, BF16), jnp.asarray(qmask), jnp.asarray(kmask), jnp.asarray(smask))

    rows = 16
    cond = jnp.concatenate([c, c_ctx[None, :], jnp.zeros((rows - bsz - 1, d), F32)], axis=0)
    mod_all = _modulation(cond, w_mod, b_mod)
    mod_lat = mod_all[:, :bsz].reshape(depth, bsz, 6, d)
    mod_ctx = jnp.broadcast_to(mod_all[:, bsz].reshape(depth, 1, 6, d), (depth, bsz, 6, d))
    pad2 = jnp.zeros((depth, bsz, 2, d), F32)
    mods = jnp.concatenate([mod_ctx, pad2, mod_lat, pad2], axis=2)

    xs = jnp.concatenate([ctx, x], axis=1).reshape(bsz * seq_t, d)

    w_main, w_lr = _prep_w_in(w_in, off, rank)
    zeros_k = jnp.zeros((depth, rank, hk), F32)
    gate_w = jnp.concatenate(
        [jnp.concatenate([gla_gate_up[:, 0], zeros_k], axis=2),
         jnp.concatenate([zeros_k, gla_gate_up[:, 1]], axis=2),
         jnp.zeros((depth, LANES - 2 * rank, 2 * hk), F32)], axis=1).astype(BF16)
    gate_b = gla_gate_b.reshape(depth, 1, 2 * hk)
    g_a, g_c = mix_g[:, None, 0:gw], mix_g[:, None, 2 * gw:3 * gw]
    g_b = _pair_heads(mix_g[:, None, gw:2 * gw], 2, WIN_KV)
    g_d = _pair_heads(mix_g[:, None, 3 * gw:], 2, GA_KV)
    w_o = jnp.concatenate([w_out[:, 0:gw], _pair_heads(w_out[:, gw:2 * gw], 1, WIN_KV),
                           w_out[:, 2 * gw:3 * gw], _pair_heads(w_out[:, 3 * gw:], 1, GA_KV)],
                          axis=1).astype(BF16)
    sink = _pair_heads(jnp.repeat(win_sink, HEAD_DIM, axis=1)[:, None, :], 2, WIN_KV)
    q_gain = jnp.tile(qk_g[:, 0:1], (1, 1, n_heads))
    k_gain = jnp.tile(qk_g[:, 1:2], (1, 1, WIN_KV))
    bsb = jnp.repeat(jnp.swapaxes(cm_bs, 1, 2), gw // CM_GROUPS, axis=2)
    ws = cm_ws.astype(BF16)
    w_up, w_down = w_ffn_in.astype(BF16), w_ffn_out.astype(BF16)
    ln_g, ln_b = cm_ln_g[:, None, :], cm_ln_b[:, None, :]

    for l in range(depth):
        modl = mods[l]
        p, lr = _inproj(xs, modl, norm_g[l, 0:1], w_main, w_lr, l, seq_t=seq_t, ctx_len=ctx_len)
        p3 = p.reshape(bsz, seq_t, -1)
        lr3 = lr.reshape(bsz, seq_t, LANES)
        y_a = _gla(p3, lr3, gate_w[l], gate_b[l], g_a[l], gla_consts, ctx_len=ctx_len, dk=dk, dv=dv)
        y_b, y_c, y_d = _mixers(p3, tabs, q_gain[l], k_gain[l], sink[l], g_b[l], g_d[l], ln_g[l], ln_b[l],
                                ws[l], bsb[l], g_c[l], blocks=blocks, ctx_len=ctx_len)
        ys = [y.reshape(bsz * seq_t, gw) for y in (y_a, y_b, y_c, y_d)]
        last = l == depth - 1
        xs = _outproj(xs, ys, modl, norm_g[l, 1:2], w_o, l, seq_t=seq_t, ctx_len=ctx_len,
                      latent_only=last)
        xs = _ffn(xs, modl, norm_g[l, 2:3], norm_g[l, 3:4], w_up, w_down, l,
                  rows_per_batch=seq if last else seq_t, row_base=ctx_len if last else 0,
                  ctx_len=ctx_len)

    return xs.reshape(bsz, seq, d)
```

```python
import functools
import math

import numpy as np
import jax
import jax.numpy as jnp
from jax import lax
from jax.experimental import pallas as pl
from jax.experimental.pallas import tpu as pltpu

F32 = jnp.float32
BF16 = jnp.bfloat16

EPS = 1e-6
HEAD_DIM = 64
ATTN_SCALE = HEAD_DIM ** -0.5
ROPE_THETA = 10000.0
GRID_W = 64
GLA_HEADS = 4
GLA_TAU = 16.0
GLA_CHUNK = 64
WIN_KV = 2
WINDOW = 128
CM_GROUPS = 4
CM_CHUNK = 128
GA_KV = 2

LANES = 128
NEG_BIG = -1e30
V7X_VMEM_LIMIT = 56 * 1024 * 1024
ROW_CHUNK = 64


def _cparams(n_axes, vmem=V7X_VMEM_LIMIT, flags=None):
    return pltpu.CompilerParams(dimension_semantics=("arbitrary",) * n_axes,
                                vmem_limit_bytes=vmem, flags=flags)


def _pick(total, candidates):
    for cand in candidates:
        if total % cand == 0:
            return cand
    raise ValueError(f"no tile in {candidates} divides {total}")


def _resident(shape):
    return pl.BlockSpec(shape, lambda *_: (0,) * len(shape), pipeline_mode=pl.Buffered(1))


def _dot(a, b):
    return jnp.dot(a, b, preferred_element_type=F32)


def _dot_nt(a, b):
    return lax.dot_general(a, b, (((1,), (1,)), ((), ())), preferred_element_type=F32)


def _rms(x, gain):
    return x * lax.rsqrt(jnp.mean(x * x, axis=-1, keepdims=True) + EPS) * gain


def _silu(x):
    return x * jax.nn.sigmoid(x)


def _gelu_tanh(x):
    return 0.5 * x * (1.0 + jnp.tanh(math.sqrt(2.0 / math.pi) * (x + 0.044715 * (x * x * x))))


def _mod_vec(mod_ref, k, chunk_is_ctx):
    return jnp.where(chunk_is_ctx, mod_ref[0, k:k + 1, :], mod_ref[0, 8 + k:9 + k, :])


def _row_loop(n_rows, step):
    n_chunks = n_rows // ROW_CHUNK
    lax.fori_loop(0, n_chunks, step, 0, unroll=3 if n_chunks % 3 == 0 else 2)


def _norm_modulate_rows(x_ref, h_ref, g_ref, mod_ref, k_shift, k_scale, row0, ctx_len):
    def step(c, carry):
        r = pl.multiple_of(c * ROW_CHUNK, ROW_CHUNK)
        is_ctx = row0 + r < ctx_len
        gain = g_ref[...] * (1.0 + _mod_vec(mod_ref, k_scale, is_ctx))
        x = x_ref[pl.ds(r, ROW_CHUNK), :]
        inv = lax.rsqrt(jnp.mean(x * x, axis=-1, keepdims=True) + EPS)
        h_ref[pl.ds(r, ROW_CHUNK), :] = (x * inv * gain + _mod_vec(mod_ref, k_shift, is_ctx)).astype(BF16)
        return carry

    _row_loop(x_ref.shape[0], step)


def _gated_residual_rows(x_ref, z_ref, o_ref, g_ref, mod_ref, k_gate, row0, ctx_len):
    def step(c, carry):
        r = pl.multiple_of(c * ROW_CHUNK, ROW_CHUNK)
        gain = g_ref[...] * _mod_vec(mod_ref, k_gate, row0 + r < ctx_len)
        z = z_ref[pl.ds(r, ROW_CHUNK), :]
        inv = lax.rsqrt(jnp.mean(z * z, axis=-1, keepdims=True) + EPS)
        o_ref[pl.ds(r, ROW_CHUNK), :] = x_ref[pl.ds(r, ROW_CHUNK), :] + z * inv * gain
        return carry

    _row_loop(x_ref.shape[0], step)


def _mod_kernel(c_ref, w_ref, b_ref, o_ref):
    s = _silu(c_ref[...]).astype(BF16)
    o_ref[0] = _dot(s, w_ref[0].astype(BF16)) + b_ref[0]


def _modulation(cond, w_mod, b_mod):
    depth, d, n = w_mod.shape
    rows = cond.shape[0]
    tn = _pick(n, (1024, 512, 256, 128))
    return pl.pallas_call(
        _mod_kernel,
        grid=(depth, n // tn),
        in_specs=[pl.BlockSpec((rows, d), lambda l, j: (0, 0)),
                  pl.BlockSpec((1, d, tn), lambda l, j: (l, 0, j)),
                  pl.BlockSpec((1, 1, tn), lambda l, j: (l, 0, j))],
        out_specs=pl.BlockSpec((1, rows, tn), lambda l, j: (l, 0, j)),
        out_shape=jax.ShapeDtypeStruct((depth, rows, n), F32),
        compiler_params=_cparams(2),
        name="modulation",
    )(cond, w_mod, b_mod.reshape(depth, 1, n))


def _pair_lanes(y, n_kv):
    g = y.shape[1] // (n_kv * HEAD_DIM)
    cols = [y[:, (kv * g + j) * HEAD_DIM:(kv * g + j + 1) * HEAD_DIM] for j in range(g) for kv in range(n_kv)]
    return jnp.concatenate(cols, axis=1)


def _w_in_kernel(w_ref, main_ref, lr_ref, *, off, rank):
    x = w_ref[0]
    main = jnp.concatenate(
        [x[:, off[0]:off[4]], x[:, off[9]:off[11]],
         _pair_lanes(x[:, off[6]:off[7]], WIN_KV), _pair_lanes(x[:, off[11]:off[12]], GA_KV),
         x[:, off[7]:off[9]], x[:, off[12]:off[14]]], axis=1)
    main_ref[0] = main.astype(BF16)
    pad = jnp.zeros((x.shape[0], LANES - 2 * rank), F32)
    lr_ref[0] = jnp.concatenate([x[:, off[4]:off[6]], pad], axis=1).astype(BF16)


def _prep_w_in(w_in, off, rank):
    depth, d, n_in = w_in.shape
    n_main = n_in - 2 * rank
    rb = _pick(d, (256, 128))
    kern = functools.partial(_w_in_kernel, off=tuple(off), rank=rank)
    return pl.pallas_call(
        kern,
        grid=(depth, d // rb),
        in_specs=[pl.BlockSpec((1, rb, n_in), lambda l, i: (l, i, 0))],
        out_specs=[pl.BlockSpec((1, rb, n_main), lambda l, i: (l, i, 0)),
                   pl.BlockSpec((1, rb, LANES), lambda l, i: (l, i, 0))],
        out_shape=[jax.ShapeDtypeStruct((depth, d, n_main), BF16),
                   jax.ShapeDtypeStruct((depth, d, LANES), BF16)],
        compiler_params=_cparams(2),
        name="prep_w_in",
    )(w_in)


def _inproj_kernel(x_ref, mod_ref, g_ref, w_ref, wlr_ref, p_ref, lr_ref, h_scr, *,
                   tm, tiles_per_batch, ctx_len):
    row0 = (pl.program_id(0) % tiles_per_batch) * tm
    _norm_modulate_rows(x_ref, h_scr, g_ref, mod_ref, 0, 1, row0, ctx_len)
    h = h_scr[...]
    lr_ref[...] = _dot(h, wlr_ref[0])
    p_ref[...] = _dot(h, w_ref[0])


def _inproj(xs, modl, gain, w_main, w_lr, layer, *, seq_t, ctx_len):
    m, d = xs.shape
    n = w_main.shape[2]
    tm = _pick(seq_t, (576, 384, 128))
    tpb = seq_t // tm
    kern = functools.partial(_inproj_kernel, tm=tm, tiles_per_batch=tpb, ctx_len=ctx_len)
    layer_block = lambda width: pl.BlockSpec((1, d, width), lambda i: (layer, 0, 0),
                                             pipeline_mode=pl.Buffered(1))
    return pl.pallas_call(
        kern,
        grid=(m // tm,),
        in_specs=[pl.BlockSpec((tm, d), lambda i: (i, 0)),
                  pl.BlockSpec((1, 16, d), lambda i: (i // tpb, 0, 0)),
                  _resident((1, d)), layer_block(n), layer_block(LANES)],
        out_specs=[pl.BlockSpec((tm, n), lambda i: (i, 0)),
                   pl.BlockSpec((tm, LANES), lambda i: (i, 0))],
        out_shape=[jax.ShapeDtypeStruct((m, n), F32),
                   jax.ShapeDtypeStruct((m, LANES), F32)],
        scratch_shapes=[pltpu.VMEM((tm, d), BF16)],
        compiler_params=_cparams(1),
        name="inproj",
    )(xs, modl, gain, w_main, w_lr)


def _outproj_kernel(x_ref, ya_ref, yb_ref, yc_ref, yd_ref, mod_ref, g_ref, w_ref, o_ref, y_scr, *,
                    tm, tiles_per_batch, row_base, ctx_len, gw):
    i = pl.program_id(0)
    y_scr[:, 0 * gw:1 * gw] = ya_ref[...]
    y_scr[:, 1 * gw:2 * gw] = yb_ref[...]
    y_scr[:, 2 * gw:3 * gw] = yc_ref[...]
    y_scr[:, 3 * gw:4 * gw] = yd_ref[...]
    zn = _rms(_dot(y_scr[...], w_ref[0]), g_ref[...])
    row0 = row_base + (i % tiles_per_batch) * tm

    @pl.when(row0 >= ctx_len)
    def _():
        o_ref[...] = x_ref[...] + mod_ref[0, 10:11, :] * zn

    @pl.when(row0 < ctx_len)
    def _():
        is_ctx = row0 + lax.broadcasted_iota(jnp.int32, (tm, 1), 0) < ctx_len
        o_ref[...] = x_ref[...] + jnp.where(is_ctx, mod_ref[0, 2:3, :], mod_ref[0, 10:11, :]) * zn


def _outproj(xs, ys, modl, gain, w, layer, *, seq_t, ctx_len, latent_only):
    m, d = xs.shape
    gw = ys[0].shape[1]
    if latent_only:
        seq = seq_t - ctx_len
        tm = _pick(seq, (512, 256, 128))
        tpb = seq // tm
        n_tiles = (m // seq_t) * tpb
        align = math.gcd(seq_t, ctx_len, tm)
        row_start = lambda i: pl.multiple_of((i // tpb) * seq_t + ctx_len + (i % tpb) * tm, align)
        xspec = pl.BlockSpec((pl.Element(tm), pl.Element(d)), lambda i: (row_start(i), 0))
        yspec = pl.BlockSpec((pl.Element(tm), pl.Element(gw)), lambda i: (row_start(i), 0))
        row_base = ctx_len
    else:
        tm = _pick(seq_t, (576, 384, 128))
        tpb = seq_t // tm
        n_tiles = m // tm
        xspec = pl.BlockSpec((tm, d), lambda i: (i, 0))
        yspec = pl.BlockSpec((tm, gw), lambda i: (i, 0))
        row_base = 0
    kern = functools.partial(_outproj_kernel, tm=tm, tiles_per_batch=tpb, row_base=row_base,
                             ctx_len=ctx_len, gw=gw)
    return pl.pallas_call(
        kern,
        grid=(n_tiles,),
        in_specs=[xspec, yspec, yspec, yspec, yspec,
                  pl.BlockSpec((1, 16, d), lambda i: (i // tpb, 0, 0)),
                  _resident((1, d)),
                  pl.BlockSpec((1, 4 * gw, d), lambda i: (layer, 0, 0), pipeline_mode=pl.Buffered(1))],
        out_specs=pl.BlockSpec((tm, d), lambda i: (i, 0)),
        out_shape=jax.ShapeDtypeStruct((n_tiles * tm, d), F32),
        scratch_shapes=[pltpu.VMEM((tm, 4 * gw), BF16)],
        compiler_params=_cparams(1),
        name="outproj",
    )(xs, *ys, modl, gain, w)


def _ffn_kernel(x_ref, mod_ref, g2_ref, g3_ref, wg_ref, wu_ref, wo_ref, o_ref, h_scr, acc_scr, *,
                tm, tiles_per_batch, row_base, ctx_len, nf):
    i = pl.program_id(0)
    j = pl.program_id(1)
    row0 = row_base + (i % tiles_per_batch) * tm

    @pl.when(j == 0)
    def _():
        _norm_modulate_rows(x_ref, h_scr, g2_ref, mod_ref, 3, 4, row0, ctx_len)
        acc_scr[...] = jnp.zeros_like(acc_scr)

    h = h_scr[...]
    act = (_silu(_dot(h, wg_ref[0])) * _dot(h, wu_ref[0])).astype(BF16)
    acc_scr[...] += _dot(act, wo_ref[0])

    @pl.when(j == nf - 1)
    def _():
        _gated_residual_rows(x_ref, acc_scr, o_ref, g3_ref, mod_ref, 5, row0, ctx_len)


def _ffn(xs, modl, g2, g3, w_in, w_out, layer, *, rows_per_batch, row_base, ctx_len):
    m, d = xs.shape
    dff = w_out.shape[1]
    tm = _pick(rows_per_batch, (768, 512, 384, 128))
    tf = _pick(dff, (512, 256, 128))
    nf = dff // tf
    tpb = rows_per_batch // tm
    kern = functools.partial(_ffn_kernel, tm=tm, tiles_per_batch=tpb, row_base=row_base,
                             ctx_len=ctx_len, nf=nf)
    return pl.pallas_call(
        kern,
        grid=(m // tm, nf),
        in_specs=[pl.BlockSpec((tm, d), lambda i, j: (i, 0)),
                  pl.BlockSpec((1, 16, d), lambda i, j: (i // tpb, 0, 0)),
                  _resident((1, d)), _resident((1, d)),
                  pl.BlockSpec((1, d, tf), lambda i, j: (layer, 0, j)),
                  pl.BlockSpec((1, d, tf), lambda i, j: (layer, 0, nf + j)),
                  pl.BlockSpec((1, tf, d), lambda i, j: (layer, j, 0))],
        out_specs=pl.BlockSpec((tm, d), lambda i, j: (i, 0)),
        out_shape=jax.ShapeDtypeStruct((m, d), F32),
        scratch_shapes=[pltpu.VMEM((tm, d), BF16), pltpu.VMEM((tm, d), F32)],
        compiler_params=_cparams(2),
        name="ffn",
    )(xs, modl, g2, g3, w_in, w_in, w_out)


def _cmlp_tile(u_ref, v_ref, lng_ref, lnb_ref, ws_ref, bsb_ref, gc_ref, o_ref):
    tc, gw = u_ref.shape[1], u_ref.shape[2]
    cw = gw // CM_GROUPS
    u = _gelu_tanh(u_ref[0])
    v = _gelu_tanh(v_ref[0])
    vc = v - jnp.mean(v, axis=-1, keepdims=True)
    vn = vc * lax.rsqrt(jnp.mean(vc * vc, axis=-1, keepdims=True) + EPS) * lng_ref[...] + lnb_ref[...]
    vb = vn.astype(BF16)
    rows = []
    for ch in range(tc // CM_CHUNK):
        cols = []
        for g in range(CM_GROUPS):
            blk = vb[ch * CM_CHUNK:(ch + 1) * CM_CHUNK, g * cw:(g + 1) * cw]
            cols.append(_dot(ws_ref[g], blk) + bsb_ref[:, g * cw:(g + 1) * cw])
        rows.append(jnp.concatenate(cols, axis=1))
    s = jnp.concatenate(rows, axis=0)
    o_ref[0] = _rms(u * s, gc_ref[...]).astype(BF16)


def _group_mean_sq(x, ones_bd):
    sq = x * x
    hi = sq.astype(BF16)
    lo = (sq - hi.astype(F32)).astype(BF16)
    return (_dot(hi, ones_bd) + _dot(lo, ones_bd)) * (1.0 / HEAD_DIM)


def _rope(x, cos, sin_signed):
    w = x.shape[1]
    lane = lax.broadcasted_iota(jnp.int32, x.shape, 1)
    first = (lane % (HEAD_DIM // 2)) < (HEAD_DIM // 4)
    partner = jnp.where(first, pltpu.roll(x, w - HEAD_DIM // 4, 1), pltpu.roll(x, HEAD_DIM // 4, 1))
    return x * cos + partner * sin_signed


def _attn_pairs(q, k_scr, v_scr, sink_ref, pairs, *, mode, latent_tile, i, tq, seq_t, ctx_len):
    lo = lax.broadcasted_iota(jnp.int32, (tq, LANES), 1) < HEAD_DIM
    lane1 = lax.broadcasted_iota(jnp.int32, (1, LANES), 1)
    row2 = lax.broadcasted_iota(jnp.int32, (2 * tq, 1), 0)

    key_sets = [(0, ctx_len, None)]
    if latent_tile and mode == "window":
        wk = tq + 2 * WINDOW
        start = pl.multiple_of(jnp.clip(i * tq - WINDOW, ctx_len, seq_t - wk), LANES)
        r = lax.broadcasted_iota(jnp.int32, (2 * tq, wk), 0)
        r = jnp.where(r >= tq, r - tq, r)
        c = lax.broadcasted_iota(jnp.int32, (2 * tq, wk), 1)
        dist = (i * tq + r) - (start + c)
        key_sets.append((start, wk, (jnp.abs(dist) <= WINDOW)))
    elif latent_tile:
        key_sets = [(0, seq_t, None)]

    outs = []
    for j in pairs:
        qp = q[:, j * LANES:(j + 1) * LANES]
        q2 = jnp.concatenate([jnp.where(lo, qp, 0.0), jnp.where(lo, 0.0, qp)], axis=0).astype(BF16)
        scores = []
        for k0, kn, valid in key_sets:
            s = _dot_nt(q2, k_scr[pl.ds(k0, kn), :])
            scores.append(s if valid is None else jnp.where(valid, s, NEG_BIG))
        mx = functools.reduce(jnp.maximum, [jnp.max(s, axis=1, keepdims=True) for s in scores])
        if mode == "window":
            sv = sink_ref[:, j * LANES:(j + 1) * LANES]
            s_lo = jnp.max(jnp.where(lane1 < HEAD_DIM, sv, NEG_BIG), axis=1, keepdims=True)
            s_hi = jnp.max(jnp.where(lane1 < HEAD_DIM, NEG_BIG, sv), axis=1, keepdims=True)
            sk = jnp.where(row2 < tq, s_lo, s_hi)
            mx = jnp.maximum(mx, sk)
            den = jnp.exp(sk - mx)
        else:
            den = 0.0
        o2 = 0.0
        for (k0, kn, _), s in zip(key_sets, scores):
            p = jnp.exp(s - mx)
            den = den + jnp.sum(p, axis=1, keepdims=True)
            o2 = o2 + _dot(p.astype(BF16), v_scr[pl.ds(k0, kn), :])
        o2 = o2 / den
        outs.append(jnp.where(lo, o2[:tq], o2[tq:]))
    return outs


def _mixers_kernel(bq_ref, dq_ref, cu_ref, cv_ref, bkv_ref, dkv_ref, cq_ref, sq_ref, ck_ref, sk_ref,
                   qg_ref, kg_ref, onesq_ref, onesk_ref, sink_ref, gb_ref, gd_ref,
                   lng_ref, lnb_ref, ws_ref, bsb_ref, gc_ref,
                   yb_ref, yc_ref, yd_ref, kb_scr, vb_scr, kd_scr, vd_scr, *, tq, seq_t, ctx_len):
    i = pl.program_id(1)
    n_ctx_tiles = ctx_len // tq

    @pl.when(i == 0)
    def _():
        kv = bkv_ref[0]
        kb_scr[...] = _rope(kv[:, :LANES], ck_ref[...], sk_ref[...]).astype(BF16)
        vb_scr[...] = kv[:, LANES:].astype(BF16)
        kv = dkv_ref[0]
        k = kv[:, :LANES]
        k = k * lax.rsqrt(_group_mean_sq(k, onesk_ref[...]) + EPS) * kg_ref[...]
        kd_scr[...] = _rope(k, ck_ref[...], sk_ref[...]).astype(BF16)
        vd_scr[...] = kv[:, LANES:].astype(BF16)

    n_pairs = bq_ref.shape[2] // LANES
    half = n_pairs // 2

    def mix(latent_tile):
        qb = _rope(bq_ref[0], cq_ref[...], sq_ref[...]) * ATTN_SCALE
        qd = dq_ref[0]
        qd = qd * lax.rsqrt(_group_mean_sq(qd, onesq_ref[...]) + EPS) * qg_ref[...]
        qd = _rope(qd, cq_ref[...], sq_ref[...]) * ATTN_SCALE
        common = dict(latent_tile=latent_tile, i=i, tq=tq, seq_t=seq_t, ctx_len=ctx_len)
        win = functools.partial(_attn_pairs, qb, kb_scr, vb_scr, sink_ref, mode="window", **common)
        ob = win(range(half))
        od = _attn_pairs(qd, kd_scr, vd_scr, None, range(n_pairs), mode="global", **common)
        ob = ob + win(range(half, n_pairs))
        yb_ref[0] = _rms(jnp.concatenate(ob, axis=1), gb_ref[...]).astype(BF16)
        yd_ref[0] = _rms(jnp.concatenate(od, axis=1), gd_ref[...]).astype(BF16)
        _cmlp_tile(cu_ref, cv_ref, lng_ref, lnb_ref, ws_ref, bsb_ref, gc_ref, yc_ref)

    pl.when(i < n_ctx_tiles)(functools.partial(mix, False))
    pl.when(i >= n_ctx_tiles)(functools.partial(mix, True))


def _mixers(p, tabs, qg, kg, sink, gb, gd, lng, lnb, ws, bsb, gc, *, blocks, ctx_len):
    b, t, _ = p.shape
    gw = gb.shape[1]
    tq = _pick(t, (256, 128))
    assert ctx_len % tq == 0 and tq % CM_CHUNK == 0
    cq, sq, ck, sk, ones_q, ones_k = tabs
    kern = functools.partial(_mixers_kernel, tq=tq, seq_t=t, ctx_len=ctx_len)
    const = lambda a: _resident(a.shape)
    qspec = lambda blk: pl.BlockSpec((1, tq, gw), lambda bi, i: (bi, i, blk))
    kvspec = lambda blk: pl.BlockSpec((1, t, 2 * LANES), lambda bi, i: (bi, 0, blk))
    tabspec = pl.BlockSpec((tq, gw), lambda bi, i: (i, 0))
    out = jax.ShapeDtypeStruct((b, t, gw), BF16)
    ospec = pl.BlockSpec((1, tq, gw), lambda bi, i: (bi, i, 0))
    kvscr = pltpu.VMEM((t, LANES), BF16)
    return pl.pallas_call(
        kern,
        grid=(b, t // tq),
        in_specs=[qspec(blocks["b_q"]), qspec(blocks["d_q"]), qspec(blocks["c_u"]), qspec(blocks["c_v"]),
                  kvspec(blocks["b_kv"]), kvspec(blocks["d_kv"]),
                  tabspec, tabspec, const(ck), const(sk), const(qg), const(kg),
                  const(ones_q), const(ones_k), const(sink), const(gb), const(gd),
                  const(lng), const(lnb), const(ws), const(bsb), const(gc)],
        out_specs=[ospec, ospec, ospec],
        out_shape=[out, out, out],
        scratch_shapes=[kvscr, kvscr, kvscr, kvscr],
        compiler_params=_cparams(2),
        name="mixers_bcd",
    )(p, p, p, p, p, p, cq, sq, ck, sk, qg, kg, ones_q, ones_k, sink, gb, gd, lng, lnb, ws, bsb, gc)


GLA_LEVELS = int(math.log2(GLA_CHUNK))
GLA_ROW_PRE = GLA_LEVELS * GLA_CHUNK
GLA_ROW_SUF = GLA_ROW_PRE + GLA_CHUNK
GLA_ROW_LAST = GLA_ROW_SUF + GLA_CHUNK
GLA_ROWS = GLA_ROW_LAST + 16
GLA_STEPS_PER_ITER = 2


def _gla_constants(dk):
    n = GLA_CHUNK
    csum = np.zeros((2, GLA_ROWS, n), np.float32)
    qmask = np.zeros((2, GLA_LEVELS + 1, n, 1), np.float32)
    kmask = np.zeros((2, GLA_LEVELS + 1, n, 1), np.float32)
    smask = np.zeros((2, GLA_LEVELS + 1, n, n), np.float32)
    for d in range(2):
        tau = np.arange(n) if d == 0 else n - 1 - np.arange(n)
        qmask[d, 0] = 1.0
        kmask[d, 0] = 1.0
        smask[d, 0] = np.eye(n)
        for lv in range(1, GLA_LEVELS + 1):
            h = 1 << (lv - 1)
            blk = tau // (2 * h)
            upper = (tau % (2 * h)) >= h
            same = blk[:, None] == blk[None, :]
            both_up = upper[:, None] & upper[None, :]
            both_lo = (~upper[:, None]) & (~upper[None, :])
            c = np.where(upper[:, None],
                         same & both_up & (tau[None, :] <= tau[:, None]),
                         same & both_lo & (tau[None, :] > tau[:, None]))
            csum[d, (lv - 1) * n:lv * n] = c
            qmask[d, lv, :, 0] = upper
            kmask[d, lv, :, 0] = ~upper
            smask[d, lv] = same & upper[:, None] & (~upper[None, :])
        csum[d, GLA_ROW_PRE:GLA_ROW_SUF] = tau[None, :] <= tau[:, None]
        csum[d, GLA_ROW_SUF:GLA_ROW_LAST] = tau[None, :] > tau[:, None]
        csum[d, GLA_ROW_LAST:] = 1.0
    hk = GLA_HEADS * dk
    qmask = np.broadcast_to(qmask, (2, GLA_LEVELS + 1, n, hk)).copy()
    kmask = np.broadcast_to(kmask, (2, GLA_LEVELS + 1, n, hk)).copy()
    smask = np.tile(smask, (1, 1, 1, GLA_HEADS))
    return csum, qmask, kmask, smask


def _gla_kernel(pa_ref, lr_ref, gw_ref, gb_ref, gain_ref, cs_ref, qm_ref, km_ref, sm_ref,
                o_ref, la_scr, oacc_scr, st_scr, *, seq_t, ctx_len, dk, dv):
    hk = GLA_HEADS * dk
    hv = GLA_HEADS * dv
    n = GLA_CHUNK
    nc = seq_t // n
    nc_ctx = ctx_len // n

    z = _dot(lr_ref[0].astype(BF16), gw_ref[...]) + gb_ref[...]
    la_scr[...] = (jnp.minimum(z, 0.0) - jnp.log1p(jnp.exp(-jnp.abs(z)))) * (1.0 / GLA_TAU)
    oacc_scr[...] = jnp.zeros_like(oacc_scr)
    st_scr[...] = jnp.zeros_like(st_scr)

    lane_head = lax.broadcasted_iota(jnp.int32, (n, hk), 1) // dk
    head_lanes = [lane_head == h for h in range(GLA_HEADS)]

    def per_head_rows(x):
        return jnp.concatenate([jnp.where(m, x, 0.0) for m in head_lanes], axis=0).astype(BF16)

    steps = GLA_STEPS_PER_ITER if nc % GLA_STEPS_PER_ITER == 0 else 2
    assert nc % steps == 0
    chains = [(d, u) for u in range(steps) for d in (0, 1)]

    def scan_iter(it, carry):
        r0 = []
        for d, u in chains:
            s = it * steps + u
            c = s if d == 0 else jnp.where(s < nc_ctx, nc_ctx - 1 - s, nc - 1 - (s - nc_ctx))
            r0.append(pl.multiple_of(c * n, n))
        q = [pa_ref[0, pl.ds(r, n), 0:hk] * (dk ** -0.5) for r in r0]
        k = [pa_ref[0, pl.ds(r, n), hk:2 * hk] for r in r0]
        v = [pa_ref[0, pl.ds(r, n), 2 * hk:2 * hk + hv] for r in r0]
        ex2 = []
        for (d, _), r in zip(chains, r0):
            a = la_scr[pl.ds(r, n), d * hk:(d + 1) * hk]
            a_hi = a.astype(BF16)
            a_lo = (a - a_hi.astype(F32)).astype(BF16)
            ex2.append(_dot(cs_ref[d], jnp.concatenate([a_hi, a_lo], axis=1)))
        e = [jnp.exp(x[:, :hk] + x[:, hk:]) for x in ex2]

        scores = [jnp.zeros((n, GLA_HEADS * n), F32) for _ in chains]
        for lv in range(GLA_LEVELS + 1):
            for ci, (d, _) in enumerate(chains):
                if lv == 0:
                    ql, kl = q[ci], k[ci]
                else:
                    el = e[ci][(lv - 1) * n:lv * n]
                    ql = q[ci] * el * qm_ref[d, lv]
                    kl = k[ci] * el * km_ref[d, lv]
                scores[ci] = scores[ci] + _dot_nt(ql.astype(BF16), per_head_rows(kl)) * sm_ref[d, lv]

        kt = []
        for ci in range(len(chains)):
            kh = k[ci] * e[ci][GLA_ROW_SUF:GLA_ROW_LAST]
            e_last = e[ci][GLA_ROW_LAST:GLA_ROW_LAST + 1]
            kt.append(jnp.concatenate([kh, jnp.broadcast_to(e_last, (n, hk))], axis=0).T)
        dstate = []
        o_intra = []
        for ci in range(len(chains)):
            vpad = jnp.concatenate([v[ci], jnp.zeros_like(v[ci])], axis=0).astype(BF16)
            full = _dot(kt[ci].astype(BF16), vpad)
            dstate.append(jnp.concatenate(
                [full[h * dk:(h + 1) * dk, h * dv:(h + 1) * dv] for h in range(GLA_HEADS)], axis=0))
            v_rows = jnp.concatenate([v[ci][:, h * dv:(h + 1) * dv] for h in range(GLA_HEADS)],
                                     axis=0).astype(BF16)
            o_intra.append(_dot(per_head_rows(scores[ci]), v_rows))

        st = [st_scr[0], st_scr[1]]
        for ci, (d, _) in enumerate(chains):
            qh = q[ci] * e[ci][GLA_ROW_PRE:GLA_ROW_SUF]
            o_rows = o_intra[ci] + _dot(per_head_rows(qh), st[d].astype(BF16))
            oacc_scr[pl.ds(r0[ci], n), :] += jnp.concatenate(
                [o_rows[h * n:(h + 1) * n, :] for h in range(GLA_HEADS)], axis=1)
            st[d] = st[d] * kt[ci][:, n:n + 1] + dstate[ci]
        st_scr[0] = st[0]
        st_scr[1] = st[1]
        return carry

    lax.fori_loop(0, nc // steps, scan_iter, 0)

    tfin = _pick(seq_t, (256, 128))
    for t0 in range(0, seq_t, tfin):
        o = oacc_scr[t0:t0 + tfin, :]
        parts = [_rms(o[:, h * dv:(h + 1) * dv], gain_ref[:, h * dv:(h + 1) * dv])
                 for h in range(GLA_HEADS)]
        gate = pa_ref[0, t0:t0 + tfin, 2 * hk + hv:2 * hk + 2 * hv]
        o_ref[0, t0:t0 + tfin, :] = (jnp.concatenate(parts, axis=1) * _silu(gate)).astype(BF16)


def _gla(p, lr, gw, gb, gain, consts, *, ctx_len, dk, dv):
    b, t, _ = p.shape
    hk = GLA_HEADS * dk
    hv = GLA_HEADS * dv
    pa_w = 2 * hk + 2 * hv
    cs, qm, km, sm = consts
    kern = functools.partial(_gla_kernel, seq_t=t, ctx_len=ctx_len, dk=dk, dv=dv)
    const = lambda a: _resident(a.shape)
    return pl.pallas_call(
        kern,
        grid=(b,),
        in_specs=[pl.BlockSpec((1, t, pa_w), lambda bi: (bi, 0, 0)),
                  pl.BlockSpec((1, t, LANES), lambda bi: (bi, 0, 0)),
                  const(gw), const(gb), const(gain), const(cs), const(qm), const(km), const(sm)],
        out_specs=pl.BlockSpec((1, t, hv), lambda bi: (bi, 0, 0)),
        out_shape=jax.ShapeDtypeStruct((b, t, hv), BF16),
        scratch_shapes=[pltpu.VMEM((t, 2 * hk), F32), pltpu.VMEM((t, hv), F32),
                        pltpu.VMEM((2, hk, dv), F32)],
        compiler_params=_cparams(1),
        name="gla",
    )(p, lr, gw, gb, gain, cs, qm, km, sm)


def _rope_tables(length, ctx_len, n_heads):
    rows = length // GRID_W
    row = jnp.repeat(jnp.arange(rows), GRID_W).astype(F32)
    col = jnp.tile(jnp.arange(GRID_W), rows).astype(F32)
    quarter = HEAD_DIM // 4
    inv = ROPE_THETA ** (-jnp.arange(quarter, dtype=F32) / quarter)
    ar, ac = row[:, None] * inv, col[:, None] * inv
    ang = jnp.concatenate([ar, ar, ac, ac], axis=-1)
    cos, sin = jnp.cos(ang), jnp.sin(ang)
    first = (jnp.arange(HEAD_DIM) % (HEAD_DIM // 2)) < quarter
    sin_signed = jnp.where(first[None, :], -sin, sin)
    cos = jnp.concatenate([jnp.ones((ctx_len, HEAD_DIM), F32), cos], axis=0)
    sin_signed = jnp.concatenate([jnp.zeros((ctx_len, HEAD_DIM), F32), sin_signed], axis=0)
    return jnp.tile(cos, (1, n_heads)), jnp.tile(sin_signed, (1, n_heads))


def _pair_heads(a, axis, n_kv):
    axis = axis % a.ndim
    g = a.shape[axis] // (n_kv * HEAD_DIM)
    shape = a.shape[:axis] + (n_kv, g, HEAD_DIM) + a.shape[axis + 1:]
    return jnp.swapaxes(a.reshape(shape), axis, axis + 1).reshape(a.shape)


def _block_diag_mask(rows, cols, rb, cb):
    r = np.arange(rows)[:, None] // rb
    c = np.arange(cols)[None, :] // cb
    return (r == c).astype(np.float32)


def kernel(x, c, ctx, c_ctx, w_mod, b_mod, norm_g, w_in, gla_gate_up, gla_gate_b, win_sink, cm_ln_g,
           cm_ln_b, cm_ws, cm_bs, qk_g, mix_g, w_out, w_ffn_in, w_ffn_out):
    bsz, seq, d = x.shape
    ctx_len = ctx.shape[1]
    seq_t = ctx_len + seq
    depth = w_mod.shape[0]
    gw = d // 4
    rank = gla_gate_up.shape[2]
    dk = gla_gate_up.shape[3] // GLA_HEADS
    dv = gw // GLA_HEADS
    n_heads = gw // HEAD_DIM
    kvw = WIN_KV * HEAD_DIM
    hk = GLA_HEADS * dk
    assert kvw == LANES and GA_KV == WIN_KV and gw % LANES == 0 and 2 * rank <= LANES
    assert dk == GLA_CHUNK and 2 * hk == gw
    assert ctx_len % CM_CHUNK == 0 and seq % CM_CHUNK == 0 and seq % GRID_W == 0
    assert ctx_len % ROW_CHUNK == 0

    sizes = [hk, hk, gw, gw, rank, rank, gw, kvw, kvw, gw, gw, gw, kvw, kvw]
    off = [int(v) for v in np.concatenate([[0], np.cumsum(sizes)])]
    blocks = {"c_u": 3, "c_v": 4, "b_q": 5, "d_q": 6,
              "b_kv": (7 * gw) // (2 * LANES), "d_kv": (7 * gw) // (2 * LANES) + 1}

    cos_q, sin_q = _rope_tables(seq, ctx_len, n_heads)
    cos_k, sin_k = cos_q[:, :LANES], sin_q[:, :LANES]
    ones_q = jnp.asarray(_block_diag_mask(gw, gw, HEAD_DIM, HEAD_DIM), BF16)
    ones_k = ones_q[:LANES, :LANES]
    tabs = (cos_q, sin_q, cos_k, sin_k, ones_q, ones_k)

    csum, qmask, kmask, smask = _gla_constants(dk)
    gla_consts = (jnp.asarray(csum, BF16), jnp.asarray(qmask), jnp.asarray(kmask), jnp.asarray(smask))

    rows = 16
    cond = jnp.concatenate([c, c_ctx[None, :], jnp.zeros((rows - bsz - 1, d), F32)], axis=0)
    mod_all = _modulation(cond, w_mod, b_mod)
    mod_lat = mod_all[:, :bsz].reshape(depth, bsz, 6, d)
    mod_ctx = jnp.broadcast_to(mod_all[:, bsz].reshape(depth, 1, 6, d), (depth, bsz, 6, d))
    pad2 = jnp.zeros((depth, bsz, 2, d), F32)
    mods = jnp.concatenate([mod_ctx, pad2, mod_lat, pad2], axis=2)

    xs = jnp.concatenate([ctx, x], axis=1).reshape(bsz * seq_t, d)

    w_main, w_lr = _prep_w_in(w_in, off, rank)
    zeros_k = jnp.zeros((depth, rank, hk), F32)
    gate_w = jnp.concatenate(
        [jnp.concatenate([gla_gate_up[:, 0], zeros_k], axis=2),
         jnp.concatenate([zeros_k, gla_gate_up[:, 1]], axis=2),
         jnp.zeros((depth, LANES - 2 * rank, 2 * hk), F32)], axis=1).astype(BF16)
    gate_b = gla_gate_b.reshape(depth, 1, 2 * hk)
    g_a, g_c = mix_g[:, None, 0:gw], mix_g[:, None, 2 * gw:3 * gw]
    g_b = _pair_heads(mix_g[:, None, gw:2 * gw], 2, WIN_KV)
    g_d = _pair_heads(mix_g[:, None, 3 * gw:], 2, GA_KV)
    w_o = jnp.concatenate([w_out[:, 0:gw], _pair_heads(w_out[:, gw:2 * gw], 1, WIN_KV),
                           w_out[:, 2 * gw:3 * gw], _pair_heads(w_out[:, 3 * gw:], 1, GA_KV)],
                          axis=1).astype(BF16)
    sink = _pair_heads(jnp.repeat(win_sink, HEAD_DIM, axis=1)[:, None, :], 2, WIN_KV)
    q_gain = jnp.tile(qk_g[:, 0:1], (1, 1, n_heads))
    k_gain = jnp.tile(qk_g[:, 1:2], (1, 1, WIN_KV))
    bsb = jnp.repeat(jnp.swapaxes(cm_bs, 1, 2), gw // CM_GROUPS, axis=2)
    ws = cm_ws.astype(BF16)
    w_up, w_down = w_ffn_in.astype(BF16), w_ffn_out.astype(BF16)
    ln_g, ln_b = cm_ln_g[:, None, :], cm_ln_b[:, None, :]

    for l in range(depth):
        modl = mods[l]
        p, lr = _inproj(xs, modl, norm_g[l, 0:1], w_main, w_lr, l, seq_t=seq_t, ctx_len=ctx_len)
        p3 = p.reshape(bsz, seq_t, -1)
        lr3 = lr.reshape(bsz, seq_t, LANES)
        y_a = _gla(p3, lr3, gate_w[l], gate_b[l], g_a[l], gla_consts, ctx_len=ctx_len, dk=dk, dv=dv)
        y_b, y_c, y_d = _mixers(p3, tabs, q_gain[l], k_gain[l], sink[l], g_b[l], g_d[l], ln_g[l], ln_b[l],
                                ws[l], bsb[l], g_c[l], blocks=blocks, ctx_len=ctx_len)
        ys = [y.reshape(bsz * seq_t, gw) for y in (y_a, y_b, y_c, y_d)]
        last = l == depth - 1
        xs = _outproj(xs, ys, modl, norm_g[l, 1:2], w_o, l, seq_t=seq_t, ctx_len=ctx_len,
                      latent_only=last)
        xs = _ffn(xs, modl, norm_g[l, 2:3], norm_g[l, 3:4], w_up, w_down, l,
                  rows_per_batch=seq if last else seq_t, row_base=ctx_len if last else 0,
                  ctx_len=ctx_len)

    return xs.reshape(bsz, seq, d)
```

```python
import functools
import math

import numpy as np
import jax
import jax.numpy as jnp
from jax import lax
from jax.experimental import pallas as pl
from jax.experimental.pallas import tpu as pltpu

F32 = jnp.float32
BF16 = jnp.bfloat16

EPS = 1e-6
HEAD_DIM = 64
ATTN_SCALE = HEAD_DIM ** -0.5
ROPE_THETA = 10000.0
GRID_W = 64
GLA_HEADS = 4
GLA_TAU = 16.0
GLA_CHUNK = 64
WIN_KV = 2
WINDOW = 128
CM_GROUPS = 4
CM_CHUNK = 128
GA_KV = 2

LANES = 128
NEG_BIG = -1e30
V7X_VMEM_LIMIT = 56 * 1024 * 1024
ROW_CHUNK = 64


def _cparams(n_axes, vmem=V7X_VMEM_LIMIT, flags=None):
    return pltpu.CompilerParams(dimension_semantics=("arbitrary",) * n_axes,
                                vmem_limit_bytes=vmem, flags=flags)


def _pick(total, candidates):
    for cand in candidates:
        if total % cand == 0:
            return cand
    raise ValueError(f"no tile in {candidates} divides {total}")


def _resident(shape):
    return pl.BlockSpec(shape, lambda *_: (0,) * len(shape), pipeline_mode=pl.Buffered(1))


def _dot(a, b):
    return jnp.dot(a, b, preferred_element_type=F32)


def _dot_nt(a, b):
    return lax.dot_general(a, b, (((1,), (1,)), ((), ())), preferred_element_type=F32)


def _rms(x, gain):
    return x * lax.rsqrt(jnp.mean(x * x, axis=-1, keepdims=True) + EPS) * gain


def _silu(x):
    return x * jax.nn.sigmoid(x)


def _gelu_tanh(x):
    return 0.5 * x * (1.0 + jnp.tanh(math.sqrt(2.0 / math.pi) * (x + 0.044715 * (x * x * x))))


def _mod_vec(mod_ref, k, chunk_is_ctx):
    return jnp.where(chunk_is_ctx, mod_ref[0, k:k + 1, :], mod_ref[0, 8 + k:9 + k, :])


def _row_loop(n_rows, step):
    n_chunks = n_rows // ROW_CHUNK
    lax.fori_loop(0, n_chunks, step, 0, unroll=3 if n_chunks % 3 == 0 else 2)


def _norm_modulate_rows(x_ref, h_ref, g_ref, mod_ref, k_shift, k_scale, row0, ctx_len):
    def step(c, carry):
        r = pl.multiple_of(c * ROW_CHUNK, ROW_CHUNK)
        is_ctx = row0 + r < ctx_len
        gain = g_ref[...] * (1.0 + _mod_vec(mod_ref, k_scale, is_ctx))
        x = x_ref[pl.ds(r, ROW_CHUNK), :]
        inv = lax.rsqrt(jnp.mean(x * x, axis=-1, keepdims=True) + EPS)
        h_ref[pl.ds(r, ROW_CHUNK), :] = (x * inv * gain + _mod_vec(mod_ref, k_shift, is_ctx)).astype(BF16)
        return carry

    _row_loop(x_ref.shape[0], step)


def _gated_residual_rows(x_ref, z_ref, o_ref, g_ref, mod_ref, k_gate, row0, ctx_len):
    def step(c, carry):
        r = pl.multiple_of(c * ROW_CHUNK, ROW_CHUNK)
        gain = g_ref[...] * _mod_vec(mod_ref, k_gate, row0 + r < ctx_len)
        z = z_ref[pl.ds(r, ROW_CHUNK), :]
        inv = lax.rsqrt(jnp.mean(z * z, axis=-1, keepdims=True) + EPS)
        o_ref[pl.ds(r, ROW_CHUNK), :] = x_ref[pl.ds(r, ROW_CHUNK), :] + z * inv * gain
        return carry

    _row_loop(x_ref.shape[0], step)


def _mod_kernel(c_ref, w_ref, b_ref, o_ref):
    s = _silu(c_ref[...]).astype(BF16)
    o_ref[0] = _dot(s, w_ref[0].astype(BF16)) + b_ref[0]


def _modulation(cond, w_mod, b_mod):
    depth, d, n = w_mod.shape
    rows = cond.shape[0]
    tn = _pick(n, (1024, 512, 256, 128))
    return pl.pallas_call(
        _mod_kernel,
        grid=(depth, n // tn),
        in_specs=[pl.BlockSpec((rows, d), lambda l, j: (0, 0)),
                  pl.BlockSpec((1, d, tn), lambda l, j: (l, 0, j)),
                  pl.BlockSpec((1, 1, tn), lambda l, j: (l, 0, j))],
        out_specs=pl.BlockSpec((1, rows, tn), lambda l, j: (l, 0, j)),
        out_shape=jax.ShapeDtypeStruct((depth, rows, n), F32),
        compiler_params=_cparams(2),
        name="modulation",
    )(cond, w_mod, b_mod.reshape(depth, 1, n))


def _pair_lanes(y, n_kv):
    g = y.shape[1] // (n_kv * HEAD_DIM)
    cols = [y[:, (kv * g + j) * HEAD_DIM:(kv * g + j + 1) * HEAD_DIM] for j in range(g) for kv in range(n_kv)]
    return jnp.concatenate(cols, axis=1)


def _w_in_kernel(w_ref, o_ref, *, off, rank):
    x = w_ref[0]
    pad = jnp.zeros((x.shape[0], LANES - 2 * rank), F32)
    o_ref[0] = jnp.concatenate(
        [x[:, off[0]:off[4]], x[:, off[9]:off[11]],
         _pair_lanes(x[:, off[6]:off[7]], WIN_KV), _pair_lanes(x[:, off[11]:off[12]], GA_KV),
         x[:, off[7]:off[9]], x[:, off[12]:off[14]], x[:, off[4]:off[6]], pad], axis=1).astype(BF16)


def _prep_w_in(w_in, off, rank):
    depth, d, n_in = w_in.shape
    n_out = n_in - 2 * rank + LANES
    rb = _pick(d, (256, 128))
    kern = functools.partial(_w_in_kernel, off=tuple(off), rank=rank)
    return pl.pallas_call(
        kern,
        grid=(depth, d // rb),
        in_specs=[pl.BlockSpec((1, rb, n_in), lambda l, i: (l, i, 0))],
        out_specs=pl.BlockSpec((1, rb, n_out), lambda l, i: (l, i, 0)),
        out_shape=jax.ShapeDtypeStruct((depth, d, n_out), BF16),
        compiler_params=_cparams(2),
        name="prep_w_in",
    )(w_in)


def _inproj_kernel(x_ref, mod_ref, g_ref, w_ref, p_ref, h_scr, *, tm, tiles_per_batch, ctx_len):
    row0 = (pl.program_id(0) % tiles_per_batch) * tm
    _norm_modulate_rows(x_ref, h_scr, g_ref, mod_ref, 0, 1, row0, ctx_len)
    p_ref[...] = _dot(h_scr[...], w_ref[0])


def _inproj(xs, modl, gain, w_all, layer, *, seq_t, ctx_len):
    m, d = xs.shape
    n = w_all.shape[2]
    tm = _pick(seq_t, (576, 384, 128))
    tpb = seq_t // tm
    kern = functools.partial(_inproj_kernel, tm=tm, tiles_per_batch=tpb, ctx_len=ctx_len)
    return pl.pallas_call(
        kern,
        grid=(m // tm,),
        in_specs=[pl.BlockSpec((tm, d), lambda i: (i, 0)),
                  pl.BlockSpec((1, 16, d), lambda i: (i // tpb, 0, 0)),
                  _resident((1, d)),
                  pl.BlockSpec((1, d, n), lambda i: (layer, 0, 0), pipeline_mode=pl.Buffered(1))],
        out_specs=pl.BlockSpec((tm, n), lambda i: (i, 0)),
        out_shape=jax.ShapeDtypeStruct((m, n), F32),
        scratch_shapes=[pltpu.VMEM((tm, d), BF16)],
        compiler_params=_cparams(1),
        name="inproj",
    )(xs, modl, gain, w_all)


def _outproj_kernel(x_ref, ya_ref, yb_ref, yc_ref, yd_ref, mod_ref, g_ref, w_ref, o_ref, y_scr, *,
                    tm, tiles_per_batch, row_base, ctx_len, gw):
    i = pl.program_id(0)
    y_scr[:, 0 * gw:1 * gw] = ya_ref[...]
    y_scr[:, 1 * gw:2 * gw] = yb_ref[...]
    y_scr[:, 2 * gw:3 * gw] = yc_ref[...]
    y_scr[:, 3 * gw:4 * gw] = yd_ref[...]
    zn = _rms(_dot(y_scr[...], w_ref[0]), g_ref[...])
    row0 = row_base + (i % tiles_per_batch) * tm

    @pl.when(row0 >= ctx_len)
    def _():
        o_ref[...] = x_ref[...] + mod_ref[0, 10:11, :] * zn

    @pl.when(row0 < ctx_len)
    def _():
        is_ctx = row0 + lax.broadcasted_iota(jnp.int32, (tm, 1), 0) < ctx_len
        o_ref[...] = x_ref[...] + jnp.where(is_ctx, mod_ref[0, 2:3, :], mod_ref[0, 10:11, :]) * zn


def _outproj(xs, ys, modl, gain, w, layer, *, seq_t, ctx_len, latent_only):
    m, d = xs.shape
    gw = ys[0].shape[1]
    if latent_only:
        seq = seq_t - ctx_len
        tm = _pick(seq, (512, 256, 128))
        tpb = seq // tm
        n_tiles = (m // seq_t) * tpb
        align = math.gcd(seq_t, ctx_len, tm)
        row_start = lambda i: pl.multiple_of((i // tpb) * seq_t + ctx_len + (i % tpb) * tm, align)
        xspec = pl.BlockSpec((pl.Element(tm), pl.Element(d)), lambda i: (row_start(i), 0))
        yspec = pl.BlockSpec((pl.Element(tm), pl.Element(gw)), lambda i: (row_start(i), 0))
        row_base = ctx_len
    else:
        tm = _pick(seq_t, (576, 384, 128))
        tpb = seq_t // tm
        n_tiles = m // tm
        xspec = pl.BlockSpec((tm, d), lambda i: (i, 0))
        yspec = pl.BlockSpec((tm, gw), lambda i: (i, 0))
        row_base = 0
    kern = functools.partial(_outproj_kernel, tm=tm, tiles_per_batch=tpb, row_base=row_base,
                             ctx_len=ctx_len, gw=gw)
    return pl.pallas_call(
        kern,
        grid=(n_tiles,),
        in_specs=[xspec, yspec, yspec, yspec, yspec,
                  pl.BlockSpec((1, 16, d), lambda i: (i // tpb, 0, 0)),
                  _resident((1, d)),
                  pl.BlockSpec((1, 4 * gw, d), lambda i: (layer, 0, 0), pipeline_mode=pl.Buffered(1))],
        out_specs=pl.BlockSpec((tm, d), lambda i: (i, 0)),
        out_shape=jax.ShapeDtypeStruct((n_tiles * tm, d), F32),
        scratch_shapes=[pltpu.VMEM((tm, 4 * gw), BF16)],
        compiler_params=_cparams(1),
        name="outproj",
    )(xs, *ys, modl, gain, w)


def _ffn_kernel(x_ref, mod_ref, g2_ref, g3_ref, wg_ref, wu_ref, wo_ref, o_ref, h_scr, acc_scr, *,
                tm, tiles_per_batch, row_base, ctx_len, nf):
    i = pl.program_id(0)
    j = pl.program_id(1)
    row0 = row_base + (i % tiles_per_batch) * tm

    @pl.when(j == 0)
    def _():
        _norm_modulate_rows(x_ref, h_scr, g2_ref, mod_ref, 3, 4, row0, ctx_len)
        acc_scr[...] = jnp.zeros_like(acc_scr)

    h = h_scr[...]
    act = (_silu(_dot(h, wg_ref[0])) * _dot(h, wu_ref[0])).astype(BF16)
    acc_scr[...] += _dot(act, wo_ref[0])

    @pl.when(j == nf - 1)
    def _():
        _gated_residual_rows(x_ref, acc_scr, o_ref, g3_ref, mod_ref, 5, row0, ctx_len)


def _ffn(xs, modl, g2, g3, w_in, w_out, layer, *, rows_per_batch, row_base, ctx_len):
    m, d = xs.shape
    dff = w_out.shape[1]
    tm = _pick(rows_per_batch, (768, 512, 384, 128))
    tf = _pick(dff, (512, 256, 128))
    nf = dff // tf
    tpb = rows_per_batch // tm
    kern = functools.partial(_ffn_kernel, tm=tm, tiles_per_batch=tpb, row_base=row_base,
                             ctx_len=ctx_len, nf=nf)
    return pl.pallas_call(
        kern,
        grid=(m // tm, nf),
        in_specs=[pl.BlockSpec((tm, d), lambda i, j: (i, 0)),
                  pl.BlockSpec((1, 16, d), lambda i, j: (i // tpb, 0, 0)),
                  _resident((1, d)), _resident((1, d)),
                  pl.BlockSpec((1, d, tf), lambda i, j: (layer, 0, j)),
                  pl.BlockSpec((1, d, tf), lambda i, j: (layer, 0, nf + j)),
                  pl.BlockSpec((1, tf, d), lambda i, j: (layer, j, 0))],
        out_specs=pl.BlockSpec((tm, d), lambda i, j: (i, 0)),
        out_shape=jax.ShapeDtypeStruct((m, d), F32),
        scratch_shapes=[pltpu.VMEM((tm, d), BF16), pltpu.VMEM((tm, d), F32)],
        compiler_params=_cparams(2),
        name="ffn",
    )(xs, modl, g2, g3, w_in, w_in, w_out)


def _cmlp_tile(u_ref, v_ref, lng_ref, lnb_ref, ws_ref, bsb_ref, gc_ref, o_ref):
    tc, gw = u_ref.shape[1], u_ref.shape[2]
    cw = gw // CM_GROUPS
    u = _gelu_tanh(u_ref[0])
    v = _gelu_tanh(v_ref[0])
    vc = v - jnp.mean(v, axis=-1, keepdims=True)
    vn = vc * lax.rsqrt(jnp.mean(vc * vc, axis=-1, keepdims=True) + EPS) * lng_ref[...] + lnb_ref[...]
    vb = vn.astype(BF16)
    rows = []
    for ch in range(tc // CM_CHUNK):
        cols = []
        for g in range(CM_GROUPS):
            blk = vb[ch * CM_CHUNK:(ch + 1) * CM_CHUNK, g * cw:(g + 1) * cw]
            cols.append(_dot(ws_ref[g], blk) + bsb_ref[:, g * cw:(g + 1) * cw])
        rows.append(jnp.concatenate(cols, axis=1))
    s = jnp.concatenate(rows, axis=0)
    o_ref[0] = _rms(u * s, gc_ref[...]).astype(BF16)


def _group_mean_sq(x, ones_bd):
    sq = x * x
    hi = sq.astype(BF16)
    lo = (sq - hi.astype(F32)).astype(BF16)
    return (_dot(hi, ones_bd) + _dot(lo, ones_bd)) * (1.0 / HEAD_DIM)


def _rope(x, cos, sin_signed):
    w = x.shape[1]
    lane = lax.broadcasted_iota(jnp.int32, x.shape, 1)
    first = (lane % (HEAD_DIM // 2)) < (HEAD_DIM // 4)
    partner = jnp.where(first, pltpu.roll(x, w - HEAD_DIM // 4, 1), pltpu.roll(x, HEAD_DIM // 4, 1))
    return x * cos + partner * sin_signed


def _attn_pairs(q, k_scr, v_scr, sink_ref, pairs, *, mode, latent_tile, i, tq, seq_t, ctx_len):
    lo = lax.broadcasted_iota(jnp.int32, (tq, LANES), 1) < HEAD_DIM
    lane1 = lax.broadcasted_iota(jnp.int32, (1, LANES), 1)
    row2 = lax.broadcasted_iota(jnp.int32, (2 * tq, 1), 0)

    key_sets = [(0, ctx_len, None)]
    if latent_tile and mode == "window":
        wk = tq + 2 * WINDOW
        start = pl.multiple_of(jnp.clip(i * tq - WINDOW, ctx_len, seq_t - wk), LANES)
        r = lax.broadcasted_iota(jnp.int32, (2 * tq, wk), 0)
        r = jnp.where(r >= tq, r - tq, r)
        c = lax.broadcasted_iota(jnp.int32, (2 * tq, wk), 1)
        dist = (i * tq + r) - (start + c)
        key_sets.append((start, wk, (jnp.abs(dist) <= WINDOW)))
    elif latent_tile:
        key_sets = [(0, seq_t, None)]

    outs = []
    for j in pairs:
        qp = q[:, j * LANES:(j + 1) * LANES]
        q2 = jnp.concatenate([jnp.where(lo, qp, 0.0), jnp.where(lo, 0.0, qp)], axis=0).astype(BF16)
        scores = []
        for k0, kn, valid in key_sets:
            s = _dot_nt(q2, k_scr[pl.ds(k0, kn), :])
            scores.append(s if valid is None else jnp.where(valid, s, NEG_BIG))
        mx = functools.reduce(jnp.maximum, [jnp.max(s, axis=1, keepdims=True) for s in scores])
        if mode == "window":
            sv = sink_ref[:, j * LANES:(j + 1) * LANES]
            s_lo = jnp.max(jnp.where(lane1 < HEAD_DIM, sv, NEG_BIG), axis=1, keepdims=True)
            s_hi = jnp.max(jnp.where(lane1 < HEAD_DIM, NEG_BIG, sv), axis=1, keepdims=True)
            sk = jnp.where(row2 < tq, s_lo, s_hi)
            mx = jnp.maximum(mx, sk)
            den = jnp.exp(sk - mx)
        else:
            den = 0.0
        o2 = 0.0
        for (k0, kn, _), s in zip(key_sets, scores):
            p = jnp.exp(s - mx)
            den = den + jnp.sum(p, axis=1, keepdims=True)
            o2 = o2 + _dot(p.astype(BF16), v_scr[pl.ds(k0, kn), :])
        o2 = o2 / den
        outs.append(jnp.where(lo, o2[:tq], o2[tq:]))
    return outs


def _mixers_kernel(bq_ref, dq_ref, cu_ref, cv_ref, bkv_ref, dkv_ref, cq_ref, sq_ref, ck_ref, sk_ref,
                   qg_ref, kg_ref, onesq_ref, onesk_ref, sink_ref, gb_ref, gd_ref,
                   lng_ref, lnb_ref, ws_ref, bsb_ref, gc_ref,
                   yb_ref, yc_ref, yd_ref, kb_scr, vb_scr, kd_scr, vd_scr, *, tq, seq_t, ctx_len):
    i = pl.program_id(1)
    n_ctx_tiles = ctx_len // tq

    @pl.when(i == 0)
    def _():
        kv = bkv_ref[0]
        kb_scr[...] = _rope(kv[:, :LANES], ck_ref[...], sk_ref[...]).astype(BF16)
        vb_scr[...] = kv[:, LANES:].astype(BF16)
        kv = dkv_ref[0]
        k = kv[:, :LANES]
        k = k * lax.rsqrt(_group_mean_sq(k, onesk_ref[...]) + EPS) * kg_ref[...]
        kd_scr[...] = _rope(k, ck_ref[...], sk_ref[...]).astype(BF16)
        vd_scr[...] = kv[:, LANES:].astype(BF16)

    n_pairs = bq_ref.shape[2] // LANES
    half = n_pairs // 2

    def mix(latent_tile):
        qb = _rope(bq_ref[0], cq_ref[...], sq_ref[...]) * ATTN_SCALE
        qd = dq_ref[0]
        qd = qd * lax.rsqrt(_group_mean_sq(qd, onesq_ref[...]) + EPS) * qg_ref[...]
        qd = _rope(qd, cq_ref[...], sq_ref[...]) * ATTN_SCALE
        common = dict(latent_tile=latent_tile, i=i, tq=tq, seq_t=seq_t, ctx_len=ctx_len)
        win = functools.partial(_attn_pairs, qb, kb_scr, vb_scr, sink_ref, mode="window", **common)
        ob = win(range(half))
        od = _attn_pairs(qd, kd_scr, vd_scr, None, range(n_pairs), mode="global", **common)
        ob = ob + win(range(half, n_pairs))
        yb_ref[0] = _rms(jnp.concatenate(ob, axis=1), gb_ref[...]).astype(BF16)
        yd_ref[0] = _rms(jnp.concatenate(od, axis=1), gd_ref[...]).astype(BF16)
        _cmlp_tile(cu_ref, cv_ref, lng_ref, lnb_ref, ws_ref, bsb_ref, gc_ref, yc_ref)

    pl.when(i < n_ctx_tiles)(functools.partial(mix, False))
    pl.when(i >= n_ctx_tiles)(functools.partial(mix, True))


def _mixers(p, tabs, qg, kg, sink, gb, gd, lng, lnb, ws, bsb, gc, *, blocks, ctx_len):
    b, t, _ = p.shape
    gw = gb.shape[1]
    tq = _pick(t, (256, 128))
    assert ctx_len % tq == 0 and tq % CM_CHUNK == 0
    cq, sq, ck, sk, ones_q, ones_k = tabs
    kern = functools.partial(_mixers_kernel, tq=tq, seq_t=t, ctx_len=ctx_len)
    const = lambda a: _resident(a.shape)
    qspec = lambda blk: pl.BlockSpec((1, tq, gw), lambda bi, i: (bi, i, blk))
    kvspec = lambda blk: pl.BlockSpec((1, t, 2 * LANES), lambda bi, i: (bi, 0, blk))
    tabspec = pl.BlockSpec((tq, gw), lambda bi, i: (i, 0))
    out = jax.ShapeDtypeStruct((b, t, gw), BF16)
    ospec = pl.BlockSpec((1, tq, gw), lambda bi, i: (bi, i, 0))
    kvscr = pltpu.VMEM((t, LANES), BF16)
    return pl.pallas_call(
        kern,
        grid=(b, t // tq),
        in_specs=[qspec(blocks["b_q"]), qspec(blocks["d_q"]), qspec(blocks["c_u"]), qspec(blocks["c_v"]),
                  kvspec(blocks["b_kv"]), kvspec(blocks["d_kv"]),
                  tabspec, tabspec, const(ck), const(sk), const(qg), const(kg),
                  const(ones_q), const(ones_k), const(sink), const(gb), const(gd),
                  const(lng), const(lnb), const(ws), const(bsb), const(gc)],
        out_specs=[ospec, ospec, ospec],
        out_shape=[out, out, out],
        scratch_shapes=[kvscr, kvscr, kvscr, kvscr],
        compiler_params=_cparams(2),
        name="mixers_bcd",
    )(p, p, p, p, p, p, cq, sq, ck, sk, qg, kg, ones_q, ones_k, sink, gb, gd, lng, lnb, ws, bsb, gc)


GLA_LEVELS = int(math.log2(GLA_CHUNK))
GLA_ROWS = GLA_LEVELS * GLA_CHUNK
GLA_STEPS_PER_ITER = 2


def _gla_constants(dk):
    n = GLA_CHUNK
    csum = np.zeros((2, GLA_ROWS, n), np.float32)
    qmask = np.zeros((2, GLA_LEVELS + 1, n, 1), np.float32)
    kmask = np.zeros((2, GLA_LEVELS + 1, n, 1), np.float32)
    smask = np.zeros((2, GLA_LEVELS + 1, n, n), np.float32)
    for d in range(2):
        tau = np.arange(n) if d == 0 else n - 1 - np.arange(n)
        qmask[d, 0] = 1.0
        kmask[d, 0] = 1.0
        smask[d, 0] = np.eye(n)
        for lv in range(1, GLA_LEVELS + 1):
            h = 1 << (lv - 1)
            blk = tau // (2 * h)
            upper = (tau % (2 * h)) >= h
            same = blk[:, None] == blk[None, :]
            both_up = upper[:, None] & upper[None, :]
            both_lo = (~upper[:, None]) & (~upper[None, :])
            c = np.where(upper[:, None],
                         same & both_up & (tau[None, :] <= tau[:, None]),
                         same & both_lo & (tau[None, :] > tau[:, None]))
            csum[d, (lv - 1) * n:lv * n] = c
            qmask[d, lv, :, 0] = upper
            kmask[d, lv, :, 0] = ~upper
            smask[d, lv] = same & upper[:, None] & (~upper[None, :])
    hk = GLA_HEADS * dk
    qmask = np.broadcast_to(qmask, (2, GLA_LEVELS + 1, n, hk)).copy()
    kmask = np.broadcast_to(kmask, (2, GLA_LEVELS + 1, n, hk)).copy()
    smask = np.tile(smask, (1, 1, 1, GLA_HEADS))
    return csum, qmask, kmask, smask


def _gla_kernel(pa_ref, lr_ref, gw_ref, gb_ref, gain_ref, cs_ref, qm_ref, km_ref, sm_ref,
                o_ref, la_scr, oacc_scr, st_scr, *, seq_t, ctx_len, dk, dv):
    hk = GLA_HEADS * dk
    hv = GLA_HEADS * dv
    n = GLA_CHUNK
    nc = seq_t // n
    nc_ctx = ctx_len // n

    z = _dot(lr_ref[0].astype(BF16), gw_ref[...]) + gb_ref[...]
    la_scr[...] = (jnp.minimum(z, 0.0) - jnp.log1p(jnp.exp(-jnp.abs(z)))) * (1.0 / GLA_TAU)
    oacc_scr[...] = jnp.zeros_like(oacc_scr)
    st_scr[...] = jnp.zeros_like(st_scr)

    lane_head = lax.broadcasted_iota(jnp.int32, (n, hk), 1) // dk
    head_lanes = [lane_head == h for h in range(GLA_HEADS)]

    def per_head_rows(x):
        return jnp.concatenate([jnp.where(m, x, 0.0) for m in head_lanes], axis=0).astype(BF16)

    steps = GLA_STEPS_PER_ITER if nc % GLA_STEPS_PER_ITER == 0 else 2
    assert nc % steps == 0
    chains = [(d, u) for u in range(steps) for d in (0, 1)]

    def scan_iter(it, carry):
        r0 = []
        for d, u in chains:
            s = it * steps + u
            c = s if d == 0 else jnp.where(s < nc_ctx, nc_ctx - 1 - s, nc - 1 - (s - nc_ctx))
            r0.append(pl.multiple_of(c * n, n))
        q = [pa_ref[0, pl.ds(r, n), 0:hk] * (dk ** -0.5) for r in r0]
        k = [pa_ref[0, pl.ds(r, n), hk:2 * hk] for r in r0]
        v = [pa_ref[0, pl.ds(r, n), 2 * hk:2 * hk + hv] for r in r0]
        ex2 = []
        a_first = []
        for (d, _), r in zip(chains, r0):
            a = la_scr[pl.ds(r, n), d * hk:(d + 1) * hk]
            a_hi = a.astype(BF16)
            a_lo = (a - a_hi.astype(F32)).astype(BF16)
            ex2.append(_dot(cs_ref[d], jnp.concatenate([a_hi, a_lo], axis=1)))
            a_first.append(a[0:1] if d == 0 else a[n - 1:n])
        ex = [x[:, :hk] + x[:, hk:] for x in ex2]
        e = [jnp.exp(x) for x in ex]

        e_pre, e_suf, e_last = [], [], []
        for ci, (d, _) in enumerate(chains):
            top = ex[ci][(GLA_LEVELS - 1) * n:GLA_LEVELS * n]
            first, final = (0, n - 1) if d == 0 else (n - 1, 0)
            s_lo = top[first:first + 1] + a_first[ci]
            s_up = top[final:final + 1]
            signed = (2.0 * qm_ref[d, GLA_LEVELS] - 1.0) * top
            e_pre.append(jnp.exp(s_lo + signed))
            e_suf.append(jnp.exp(s_up - signed))
            e_last.append(jnp.exp(s_lo + s_up))

        scores = [jnp.zeros((n, GLA_HEADS * n), F32) for _ in chains]
        for lv in range(GLA_LEVELS + 1):
            for ci, (d, _) in enumerate(chains):
                if lv == 0:
                    ql, kl = q[ci], k[ci]
                else:
                    el = e[ci][(lv - 1) * n:lv * n]
                    ql = q[ci] * el * qm_ref[d, lv]
                    kl = k[ci] * el * km_ref[d, lv]
                scores[ci] = scores[ci] + _dot_nt(ql.astype(BF16), per_head_rows(kl)) * sm_ref[d, lv]

        kt = []
        for ci in range(len(chains)):
            kh = k[ci] * e_suf[ci]
            kt.append(jnp.concatenate([kh, jnp.broadcast_to(e_last[ci], (n, hk))], axis=0).T)
        dstate = []
        o_intra = []
        for ci in range(len(chains)):
            vpad = jnp.concatenate([v[ci], jnp.zeros_like(v[ci])], axis=0).astype(BF16)
            full = _dot(kt[ci].astype(BF16), vpad)
            dstate.append(jnp.concatenate(
                [full[h * dk:(h + 1) * dk, h * dv:(h + 1) * dv] for h in range(GLA_HEADS)], axis=0))
            v_rows = jnp.concatenate([v[ci][:, h * dv:(h + 1) * dv] for h in range(GLA_HEADS)],
                                     axis=0).astype(BF16)
            o_intra.append(_dot(per_head_rows(scores[ci]), v_rows))

        st = [st_scr[0], st_scr[1]]
        for ci, (d, _) in enumerate(chains):
            qh = q[ci] * e_pre[ci]
            o_rows = o_intra[ci] + _dot(per_head_rows(qh), st[d].astype(BF16))
            oacc_scr[pl.ds(r0[ci], n), :] += jnp.concatenate(
                [o_rows[h * n:(h + 1) * n, :] for h in range(GLA_HEADS)], axis=1)
            st[d] = st[d] * kt[ci][:, n:n + 1] + dstate[ci]
        st_scr[0] = st[0]
        st_scr[1] = st[1]
        return carry

    lax.fori_loop(0, nc // steps, scan_iter, 0)

    tfin = _pick(seq_t, (256, 128))
    for t0 in range(0, seq_t, tfin):
        o = oacc_scr[t0:t0 + tfin, :]
        parts = [_rms(o[:, h * dv:(h + 1) * dv], gain_ref[:, h * dv:(h + 1) * dv])
                 for h in range(GLA_HEADS)]
        gate = pa_ref[0, t0:t0 + tfin, 2 * hk + hv:2 * hk + 2 * hv]
        o_ref[0, t0:t0 + tfin, :] = (jnp.concatenate(parts, axis=1) * _silu(gate)).astype(BF16)


def _gla(p, gw, gb, gain, consts, *, ctx_len, dk, dv):
    b, t, n = p.shape
    hk = GLA_HEADS * dk
    hv = GLA_HEADS * dv
    pa_w = 2 * hk + 2 * hv
    lr_blk = n // LANES - 1
    cs, qm, km, sm = consts
    kern = functools.partial(_gla_kernel, seq_t=t, ctx_len=ctx_len, dk=dk, dv=dv)
    const = lambda a: _resident(a.shape)
    return pl.pallas_call(
        kern,
        grid=(b,),
        in_specs=[pl.BlockSpec((1, t, pa_w), lambda bi: (bi, 0, 0)),
                  pl.BlockSpec((1, t, LANES), lambda bi: (bi, 0, lr_blk)),
                  const(gw), const(gb), const(gain), const(cs), const(qm), const(km), const(sm)],
        out_specs=pl.BlockSpec((1, t, hv), lambda bi: (bi, 0, 0)),
        out_shape=jax.ShapeDtypeStruct((b, t, hv), BF16),
        scratch_shapes=[pltpu.VMEM((t, 2 * hk), F32), pltpu.VMEM((t, hv), F32),
                        pltpu.VMEM((2, hk, dv), F32)],
        compiler_params=_cparams(1),
        name="gla",
    )(p, p, gw, gb, gain, cs, qm, km, sm)


def _rope_tables(length, ctx_len, n_heads):
    rows = length // GRID_W
    row = jnp.repeat(jnp.arange(rows), GRID_W).astype(F32)
    col = jnp.tile(jnp.arange(GRID_W), rows).astype(F32)
    quarter = HEAD_DIM // 4
    inv = ROPE_THETA ** (-jnp.arange(quarter, dtype=F32) / quarter)
    ar, ac = row[:, None] * inv, col[:, None] * inv
    ang = jnp.concatenate([ar, ar, ac, ac], axis=-1)
    cos, sin = jnp.cos(ang), jnp.sin(ang)
    first = (jnp.arange(HEAD_DIM) % (HEAD_DIM // 2)) < quarter
    sin_signed = jnp.where(first[None, :], -sin, sin)
    cos = jnp.concatenate([jnp.ones((ctx_len, HEAD_DIM), F32), cos], axis=0)
    sin_signed = jnp.concatenate([jnp.zeros((ctx_len, HEAD_DIM), F32), sin_signed], axis=0)
    return jnp.tile(cos, (1, n_heads)), jnp.tile(sin_signed, (1, n_heads))


def _pair_heads(a, axis, n_kv):
    axis = axis % a.ndim
    g = a.shape[axis] // (n_kv * HEAD_DIM)
    shape = a.shape[:axis] + (n_kv, g, HEAD_DIM) + a.shape[axis + 1:]
    return jnp.swapaxes(a.reshape(shape), axis, axis + 1).reshape(a.shape)


def _block_diag_mask(rows, cols, rb, cb):
    r = np.arange(rows)[:, None] // rb
    c = np.arange(cols)[None, :] // cb
    return (r == c).astype(np.float32)


def kernel(x, c, ctx, c_ctx, w_mod, b_mod, norm_g, w_in, gla_gate_up, gla_gate_b, win_sink, cm_ln_g,
           cm_ln_b, cm_ws, cm_bs, qk_g, mix_g, w_out, w_ffn_in, w_ffn_out):
    bsz, seq, d = x.shape
    ctx_len = ctx.shape[1]
    seq_t = ctx_len + seq
    depth = w_mod.shape[0]
    gw = d // 4
    rank = gla_gate_up.shape[2]
    dk = gla_gate_up.shape[3] // GLA_HEADS
    dv = gw // GLA_HEADS
    n_heads = gw // HEAD_DIM
    kvw = WIN_KV * HEAD_DIM
    hk = GLA_HEADS * dk
    assert kvw == LANES and GA_KV == WIN_KV and gw % LANES == 0 and 2 * rank <= LANES
    assert dk == GLA_CHUNK and 2 * hk == gw
    assert ctx_len % CM_CHUNK == 0 and seq % CM_CHUNK == 0 and seq % GRID_W == 0
    assert ctx_len % ROW_CHUNK == 0

    sizes = [hk, hk, gw, gw, rank, rank, gw, kvw, kvw, gw, gw, gw, kvw, kvw]
    off = [int(v) for v in np.concatenate([[0], np.cumsum(sizes)])]
    blocks = {"c_u": 3, "c_v": 4, "b_q": 5, "d_q": 6,
              "b_kv": (7 * gw) // (2 * LANES), "d_kv": (7 * gw) // (2 * LANES) + 1}

    cos_q, sin_q = _rope_tables(seq, ctx_len, n_heads)
    cos_k, sin_k = cos_q[:, :LANES], sin_q[:, :LANES]
    ones_q = jnp.asarray(_block_diag_mask(gw, gw, HEAD_DIM, HEAD_DIM), BF16)
    ones_k = ones_q[:LANES, :LANES]
    tabs = (cos_q, sin_q, cos_k, sin_k, ones_q, ones_k)

    csum, qmask, kmask, smask = _gla_constants(dk)
    gla_consts = (jnp.asarray(csum, BF16), jnp.asarray(qmask), jnp.asarray(kmask), jnp.asarray(smask))

    rows = 16
    cond = jnp.concatenate([c, c_ctx[None, :], jnp.zeros((rows - bsz - 1, d), F32)], axis=0)
    mod_all = _modulation(cond, w_mod, b_mod)
    mod_lat = mod_all[:, :bsz].reshape(depth, bsz, 6, d)
    mod_ctx = jnp.broadcast_to(mod_all[:, bsz].reshape(depth, 1, 6, d), (depth, bsz, 6, d))
    pad2 = jnp.zeros((depth, bsz, 2, d), F32)
    mods = jnp.concatenate([mod_ctx, pad2, mod_lat, pad2], axis=2)

    xs = jnp.concatenate([ctx, x], axis=1).reshape(bsz * seq_t, d)

    w_proj = _prep_w_in(w_in, off, rank)
    zeros_k = jnp.zeros((depth, rank, hk), F32)
    gate_w = jnp.concatenate(
        [jnp.concatenate([gla_gate_up[:, 0], zeros_k], axis=2),
         jnp.concatenate([zeros_k, gla_gate_up[:, 1]], axis=2),
         jnp.zeros((depth, LANES - 2 * rank, 2 * hk), F32)], axis=1).astype(BF16)
    gate_b = gla_gate_b.reshape(depth, 1, 2 * hk)
    g_a, g_c = mix_g[:, None, 0:gw], mix_g[:, None, 2 * gw:3 * gw]
    g_b = _pair_heads(mix_g[:, None, gw:2 * gw], 2, WIN_KV)
    g_d = _pair_heads(mix_g[:, None, 3 * gw:], 2, GA_KV)
    w_o = jnp.concatenate([w_out[:, 0:gw], _pair_heads(w_out[:, gw:2 * gw], 1, WIN_KV),
                           w_out[:, 2 * gw:3 * gw], _pair_heads(w_out[:, 3 * gw:], 1, GA_KV)],
                          axis=1).astype(BF16)
    sink = _pair_heads(jnp.repeat(win_sink, HEAD_DIM, axis=1)[:, None, :], 2, WIN_KV)
    q_gain = jnp.tile(qk_g[:, 0:1], (1, 1, n_heads))
    k_gain = jnp.tile(qk_g[:, 1:2], (1, 1, WIN_KV))
    bsb = jnp.repeat(jnp.swapaxes(cm_bs, 1, 2), gw // CM_GROUPS, axis=2)
    ws = cm_ws.astype(BF16)
    w_up, w_down = w_ffn_in.astype(BF16), w_ffn_out.astype(BF16)
    ln_g, ln_b = cm_ln_g[:, None, :], cm_ln_b[:, None, :]

    for l in range(depth):
        modl = mods[l]
        p = _inproj(xs, modl, norm_g[l, 0:1], w_proj, l, seq_t=seq_t, ctx_len=ctx_len)
        p3 = p.reshape(bsz, seq_t, -1)
        y_a = _gla(p3, gate_w[l], gate_b[l], g_a[l], gla_consts, ctx_len=ctx_len, dk=dk, dv=dv)
        y_b, y_c, y_d = _mixers(p3, tabs, q_gain[l], k_gain[l], sink[l], g_b[l], g_d[l], ln_g[l], ln_b[l],
                                ws[l], bsb[l], g_c[l], blocks=blocks, ctx_len=ctx_len)
        ys = [y.reshape(bsz * seq_t, gw) for y in (y_a, y_b, y_c, y_d)]
        last = l == depth - 1
        xs = _outproj(xs, ys, modl, norm_g[l, 1:2], w_o, l, seq_t=seq_t, ctx_len=ctx_len,
                      latent_only=last)
        xs = _ffn(xs, modl, norm_g[l, 2:3], norm_g[l, 3:4], w_up, w_down, l,
                  rows_per_batch=seq if last else seq_t, row_base=ctx_len if last else 0,
                  ctx_len=ctx_len)

    return xs.reshape(bsz, seq, d)
```

```python
import functools
import math

import numpy as np
import jax
import jax.numpy as jnp
from jax import lax
from jax.experimental import pallas as pl
from jax.experimental.pallas import tpu as pltpu

F32 = jnp.float32
BF16 = jnp.bfloat16

EPS = 1e-6
HEAD_DIM = 64
ATTN_SCALE = HEAD_DIM ** -0.5
ROPE_THETA = 10000.0
GRID_W = 64
GLA_HEADS = 4
GLA_TAU = 16.0
GLA_CHUNK = 64
WIN_KV = 2
WINDOW = 128
CM_GROUPS = 4
CM_CHUNK = 128
GA_KV = 2

LANES = 128
NEG_BIG = -1e30
V7X_VMEM_LIMIT = 56 * 1024 * 1024
ROW_CHUNK = 64


def _cparams(n_axes, vmem=V7X_VMEM_LIMIT, flags=None):
    return pltpu.CompilerParams(dimension_semantics=("arbitrary",) * n_axes,
                                vmem_limit_bytes=vmem, flags=flags)


def _pick(total, candidates):
    for cand in candidates:
        if total % cand == 0:
            return cand
    raise ValueError(f"no tile in {candidates} divides {total}")


def _resident(shape):
    return pl.BlockSpec(shape, lambda *_: (0,) * len(shape), pipeline_mode=pl.Buffered(1))


def _dot(a, b):
    return jnp.dot(a, b, preferred_element_type=F32)


def _dot_nt(a, b):
    return lax.dot_general(a, b, (((1,), (1,)), ((), ())), preferred_element_type=F32)


def _rms(x, gain):
    return x * lax.rsqrt(jnp.mean(x * x, axis=-1, keepdims=True) + EPS) * gain


def _silu(x):
    return x * jax.nn.sigmoid(x)


def _gelu_tanh(x):
    return 0.5 * x * (1.0 + jnp.tanh(math.sqrt(2.0 / math.pi) * (x + 0.044715 * (x * x * x))))


def _mod_vec(mod_ref, k, chunk_is_ctx):
    return jnp.where(chunk_is_ctx, mod_ref[0, k:k + 1, :], mod_ref[0, 8 + k:9 + k, :])


def _row_loop(n_rows, step):
    n_chunks = n_rows // ROW_CHUNK
    lax.fori_loop(0, n_chunks, step, 0, unroll=3 if n_chunks % 3 == 0 else 2)


def _norm_modulate_rows(x_ref, h_ref, g_ref, mod_ref, k_shift, k_scale, row0, ctx_len):
    def step(c, carry):
        r = pl.multiple_of(c * ROW_CHUNK, ROW_CHUNK)
        is_ctx = row0 + r < ctx_len
        gain = g_ref[...] * (1.0 + _mod_vec(mod_ref, k_scale, is_ctx))
        x = x_ref[pl.ds(r, ROW_CHUNK), :]
        inv = lax.rsqrt(jnp.mean(x * x, axis=-1, keepdims=True) + EPS)
        h_ref[pl.ds(r, ROW_CHUNK), :] = (x * inv * gain + _mod_vec(mod_ref, k_shift, is_ctx)).astype(BF16)
        return carry

    _row_loop(x_ref.shape[0], step)


def _gated_residual_rows(x_ref, z_ref, o_ref, g_ref, mod_ref, k_gate, row0, ctx_len):
    def step(c, carry):
        r = pl.multiple_of(c * ROW_CHUNK, ROW_CHUNK)
        gain = g_ref[...] * _mod_vec(mod_ref, k_gate, row0 + r < ctx_len)
        z = z_ref[pl.ds(r, ROW_CHUNK), :]
        inv = lax.rsqrt(jnp.mean(z * z, axis=-1, keepdims=True) + EPS)
        o_ref[pl.ds(r, ROW_CHUNK), :] = x_ref[pl.ds(r, ROW_CHUNK), :] + z * inv * gain
        return carry

    _row_loop(x_ref.shape[0], step)


def _mod_kernel(c_ref, w_ref, b_ref, o_ref):
    s = _silu(c_ref[...]).astype(BF16)
    o_ref[0] = _dot(s, w_ref[0].astype(BF16)) + b_ref[0]


def _modulation(cond, w_mod, b_mod):
    depth, d, n = w_mod.shape
    rows = cond.shape[0]
    tn = _pick(n, (1024, 512, 256, 128))
    return pl.pallas_call(
        _mod_kernel,
        grid=(depth, n // tn),
        in_specs=[pl.BlockSpec((rows, d), lambda l, j: (0, 0)),
                  pl.BlockSpec((1, d, tn), lambda l, j: (l, 0, j)),
                  pl.BlockSpec((1, 1, tn), lambda l, j: (l, 0, j))],
        out_specs=pl.BlockSpec((1, rows, tn), lambda l, j: (l, 0, j)),
        out_shape=jax.ShapeDtypeStruct((depth, rows, n), F32),
        compiler_params=_cparams(2),
        name="modulation",
    )(cond, w_mod, b_mod.reshape(depth, 1, n))


def _pair_lanes(y, n_kv):
    g = y.shape[1] // (n_kv * HEAD_DIM)
    cols = [y[:, (kv * g + j) * HEAD_DIM:(kv * g + j + 1) * HEAD_DIM] for j in range(g) for kv in range(n_kv)]
    return jnp.concatenate(cols, axis=1)


def _w_in_kernel(w_ref, o_ref, *, off, rank):
    x = w_ref[...]
    pad = jnp.zeros((x.shape[0], LANES - 2 * rank), F32)
    o_ref[0] = jnp.concatenate(
        [x[:, off[0]:off[4]], x[:, off[9]:off[11]],
         _pair_lanes(x[:, off[6]:off[7]], WIN_KV), _pair_lanes(x[:, off[11]:off[12]], GA_KV),
         x[:, off[7]:off[9]], x[:, off[12]:off[14]], x[:, off[4]:off[6]], pad], axis=1).astype(BF16)


def _prep_w_in(w_in, off, rank):
    depth, d, n_in = w_in.shape
    n_out = n_in - 2 * rank + LANES
    rb = _pick(d, (256, 128))
    kern = functools.partial(_w_in_kernel, off=tuple(off), rank=rank)
    return pl.pallas_call(
        kern,
        grid=(depth, d // rb),
        in_specs=[pl.BlockSpec((rb, n_in), lambda l, i: (l * (d // rb) + i, 0))],
        out_specs=pl.BlockSpec((1, rb, n_out), lambda l, i: (l, i, 0)),
        out_shape=jax.ShapeDtypeStruct((depth, d, n_out), BF16),
        compiler_params=_cparams(2),
        name="prep_w_in",
    )(w_in.reshape(depth * d, n_in))


def _inproj_kernel(x_ref, mod_ref, g_ref, w_ref, p_ref, h_scr, *, tm, tiles_per_batch, ctx_len):
    row0 = (pl.program_id(0) % tiles_per_batch) * tm
    _norm_modulate_rows(x_ref, h_scr, g_ref, mod_ref, 0, 1, row0, ctx_len)
    p_ref[...] = _dot(h_scr[...], w_ref[0])


def _inproj(xs, modl, gain, w_all, layer, *, seq_t, ctx_len):
    m, d = xs.shape
    n = w_all.shape[2]
    tm = _pick(seq_t, (576, 384, 128))
    tpb = seq_t // tm
    kern = functools.partial(_inproj_kernel, tm=tm, tiles_per_batch=tpb, ctx_len=ctx_len)
    return pl.pallas_call(
        kern,
        grid=(m // tm,),
        in_specs=[pl.BlockSpec((tm, d), lambda i: (i, 0)),
                  pl.BlockSpec((1, 16, d), lambda i: (i // tpb, 0, 0)),
                  _resident((1, d)),
                  pl.BlockSpec((1, d, n), lambda i: (layer, 0, 0), pipeline_mode=pl.Buffered(1))],
        out_specs=pl.BlockSpec((tm, n), lambda i: (i, 0)),
        out_shape=jax.ShapeDtypeStruct((m, n), F32),
        scratch_shapes=[pltpu.VMEM((tm, d), BF16)],
        compiler_params=_cparams(1),
        name="inproj",
    )(xs, modl, gain, w_all)


def _outproj_kernel(x_ref, ya_ref, yb_ref, yc_ref, yd_ref, mod_ref, g_ref, w_ref, o_ref, y_scr, *,
                    tm, tiles_per_batch, row_base, ctx_len, gw):
    i = pl.program_id(0)
    y_scr[:, 0 * gw:1 * gw] = ya_ref[...]
    y_scr[:, 1 * gw:2 * gw] = yb_ref[...]
    y_scr[:, 2 * gw:3 * gw] = yc_ref[...]
    y_scr[:, 3 * gw:4 * gw] = yd_ref[...]
    zn = _rms(_dot(y_scr[...], w_ref[0]), g_ref[...])
    row0 = row_base + (i % tiles_per_batch) * tm

    @pl.when(row0 >= ctx_len)
    def _():
        o_ref[...] = x_ref[...] + mod_ref[0, 10:11, :] * zn

    @pl.when(row0 < ctx_len)
    def _():
        is_ctx = row0 + lax.broadcasted_iota(jnp.int32, (tm, 1), 0) < ctx_len
        o_ref[...] = x_ref[...] + jnp.where(is_ctx, mod_ref[0, 2:3, :], mod_ref[0, 10:11, :]) * zn


def _outproj(xs, ys, modl, gain, w, layer, *, seq_t, ctx_len, latent_only):
    m, d = xs.shape
    gw = ys[0].shape[1]
    if latent_only:
        seq = seq_t - ctx_len
        tm = _pick(seq, (512, 256, 128))
        tpb = seq // tm
        n_tiles = (m // seq_t) * tpb
        align = math.gcd(seq_t, ctx_len, tm)
        row_start = lambda i: pl.multiple_of((i // tpb) * seq_t + ctx_len + (i % tpb) * tm, align)
        xspec = pl.BlockSpec((pl.Element(tm), pl.Element(d)), lambda i: (row_start(i), 0))
        yspec = pl.BlockSpec((pl.Element(tm), pl.Element(gw)), lambda i: (row_start(i), 0))
        row_base = ctx_len
    else:
        tm = _pick(seq_t, (576, 384, 128))
        tpb = seq_t // tm
        n_tiles = m // tm
        xspec = pl.BlockSpec((tm, d), lambda i: (i, 0))
        yspec = pl.BlockSpec((tm, gw), lambda i: (i, 0))
        row_base = 0
    kern = functools.partial(_outproj_kernel, tm=tm, tiles_per_batch=tpb, row_base=row_base,
                             ctx_len=ctx_len, gw=gw)
    return pl.pallas_call(
        kern,
        grid=(n_tiles,),
        in_specs=[xspec, yspec, yspec, yspec, yspec,
                  pl.BlockSpec((1, 16, d), lambda i: (i // tpb, 0, 0)),
                  _resident((1, d)),
                  pl.BlockSpec((1, 4 * gw, d), lambda i: (layer, 0, 0), pipeline_mode=pl.Buffered(1))],
        out_specs=pl.BlockSpec((tm, d), lambda i: (i, 0)),
        out_shape=jax.ShapeDtypeStruct((n_tiles * tm, d), F32),
        scratch_shapes=[pltpu.VMEM((tm, 4 * gw), BF16)],
        compiler_params=_cparams(1),
        name="outproj",
    )(xs, *ys, modl, gain, w)


def _ffn_kernel(x_ref, mod_ref, g2_ref, g3_ref, wg_ref, wu_ref, wo_ref, o_ref, h_scr, acc_scr, *,
                tm, tiles_per_batch, row_base, ctx_len, nf):
    i = pl.program_id(0)
    j = pl.program_id(1)
    row0 = row_base + (i % tiles_per_batch) * tm

    @pl.when(j == 0)
    def _():
        _norm_modulate_rows(x_ref, h_scr, g2_ref, mod_ref, 3, 4, row0, ctx_len)
        acc_scr[...] = jnp.zeros_like(acc_scr)

    h = h_scr[...]
    act = (_silu(_dot(h, wg_ref[0])) * _dot(h, wu_ref[0])).astype(BF16)
    acc_scr[...] += _dot(act, wo_ref[0])

    @pl.when(j == nf - 1)
    def _():
        _gated_residual_rows(x_ref, acc_scr, o_ref, g3_ref, mod_ref, 5, row0, ctx_len)


def _ffn(xs, modl, g2, g3, w_in, w_out, layer, *, rows_per_batch, row_base, ctx_len):
    m, d = xs.shape
    dff = w_out.shape[1]
    tm = _pick(rows_per_batch, (768, 512, 384, 128))
    tf = _pick(dff, (512, 256, 128))
    nf = dff // tf
    tpb = rows_per_batch // tm
    kern = functools.partial(_ffn_kernel, tm=tm, tiles_per_batch=tpb, row_base=row_base,
                             ctx_len=ctx_len, nf=nf)
    return pl.pallas_call(
        kern,
        grid=(m // tm, nf),
        in_specs=[pl.BlockSpec((tm, d), lambda i, j: (i, 0)),
                  pl.BlockSpec((1, 16, d), lambda i, j: (i // tpb, 0, 0)),
                  _resident((1, d)), _resident((1, d)),
                  pl.BlockSpec((1, d, tf), lambda i, j: (layer, 0, j)),
                  pl.BlockSpec((1, d, tf), lambda i, j: (layer, 0, nf + j)),
                  pl.BlockSpec((1, tf, d), lambda i, j: (layer, j, 0))],
        out_specs=pl.BlockSpec((tm, d), lambda i, j: (i, 0)),
        out_shape=jax.ShapeDtypeStruct((m, d), F32),
        scratch_shapes=[pltpu.VMEM((tm, d), BF16), pltpu.VMEM((tm, d), F32)],
        compiler_params=_cparams(2),
        name="ffn",
    )(xs, modl, g2, g3, w_in, w_in, w_out)


def _cmlp_tile(u_ref, v_ref, lng_ref, lnb_ref, ws_ref, bsb_ref, gc_ref, o_ref):
    tc, gw = u_ref.shape[1], u_ref.shape[2]
    cw = gw // CM_GROUPS
    u = _gelu_tanh(u_ref[0])
    v = _gelu_tanh(v_ref[0])
    vc = v - jnp.mean(v, axis=-1, keepdims=True)
    vn = vc * lax.rsqrt(jnp.mean(vc * vc, axis=-1, keepdims=True) + EPS) * lng_ref[...] + lnb_ref[...]
    vb = vn.astype(BF16)
    rows = []
    for ch in range(tc // CM_CHUNK):
        cols = []
        for g in range(CM_GROUPS):
            blk = vb[ch * CM_CHUNK:(ch + 1) * CM_CHUNK, g * cw:(g + 1) * cw]
            cols.append(_dot(ws_ref[g], blk) + bsb_ref[:, g * cw:(g + 1) * cw])
        rows.append(jnp.concatenate(cols, axis=1))
    s = jnp.concatenate(rows, axis=0)
    o_ref[0] = _rms(u * s, gc_ref[...]).astype(BF16)


def _group_mean_sq(x, ones_bd):
    sq = x * x
    hi = sq.astype(BF16)
    lo = (sq - hi.astype(F32)).astype(BF16)
    return (_dot(hi, ones_bd) + _dot(lo, ones_bd)) * (1.0 / HEAD_DIM)


def _rope(x, cos, sin_signed):
    w = x.shape[1]
    lane = lax.broadcasted_iota(jnp.int32, x.shape, 1)
    first = (lane % (HEAD_DIM // 2)) < (HEAD_DIM // 4)
    partner = jnp.where(first, pltpu.roll(x, w - HEAD_DIM // 4, 1), pltpu.roll(x, HEAD_DIM // 4, 1))
    return x * cos + partner * sin_signed


def _attn_pairs(q, k_scr, v_scr, sink_ref, pairs, *, mode, latent_tile, i, tq, seq_t, ctx_len):
    lo = lax.broadcasted_iota(jnp.int32, (tq, LANES), 1) < HEAD_DIM
    lane1 = lax.broadcasted_iota(jnp.int32, (1, LANES), 1)
    row2 = lax.broadcasted_iota(jnp.int32, (2 * tq, 1), 0)

    key_sets = [(0, ctx_len, None)]
    if latent_tile and mode == "window":
        wk = tq + 2 * WINDOW
        start = pl.multiple_of(jnp.clip(i * tq - WINDOW, ctx_len, seq_t - wk), LANES)
        r = lax.broadcasted_iota(jnp.int32, (2 * tq, wk), 0)
        r = jnp.where(r >= tq, r - tq, r)
        c = lax.broadcasted_iota(jnp.int32, (2 * tq, wk), 1)
        dist = (i * tq + r) - (start + c)
        key_sets.append((start, wk, (jnp.abs(dist) <= WINDOW)))
    elif latent_tile:
        key_sets = [(0, seq_t, None)]

    outs = []
    for j in pairs:
        qp = q[:, j * LANES:(j + 1) * LANES]
        q2 = jnp.concatenate([jnp.where(lo, qp, 0.0), jnp.where(lo, 0.0, qp)], axis=0).astype(BF16)
        scores = []
        for k0, kn, valid in key_sets:
            s = _dot_nt(q2, k_scr[pl.ds(k0, kn), :])
            scores.append(s if valid is None else jnp.where(valid, s, NEG_BIG))
        mx = functools.reduce(jnp.maximum, [jnp.max(s, axis=1, keepdims=True) for s in scores])
        if mode == "window":
            sv = sink_ref[:, j * LANES:(j + 1) * LANES]
            s_lo = jnp.max(jnp.where(lane1 < HEAD_DIM, sv, NEG_BIG), axis=1, keepdims=True)
            s_hi = jnp.max(jnp.where(lane1 < HEAD_DIM, NEG_BIG, sv), axis=1, keepdims=True)
            sk = jnp.where(row2 < tq, s_lo, s_hi)
            mx = jnp.maximum(mx, sk)
            den = jnp.exp(sk - mx)
        else:
            den = 0.0
        o2 = 0.0
        for (k0, kn, _), s in zip(key_sets, scores):
            p = jnp.exp(s - mx)
            den = den + jnp.sum(p, axis=1, keepdims=True)
            o2 = o2 + _dot(p.astype(BF16), v_scr[pl.ds(k0, kn), :])
        o2 = o2 / den
        outs.append(jnp.where(lo, o2[:tq], o2[tq:]))
    return outs


def _mixers_kernel(bq_ref, dq_ref, cu_ref, cv_ref, bkv_ref, dkv_ref, cq_ref, sq_ref, ck_ref, sk_ref,
                   qg_ref, kg_ref, onesq_ref, onesk_ref, sink_ref, gb_ref, gd_ref,
                   lng_ref, lnb_ref, ws_ref, bsb_ref, gc_ref,
                   yb_ref, yc_ref, yd_ref, kb_scr, vb_scr, kd_scr, vd_scr, *, tq, seq_t, ctx_len):
    i = pl.program_id(1)
    n_ctx_tiles = ctx_len // tq

    @pl.when(i == 0)
    def _():
        kv = bkv_ref[0]
        kb_scr[...] = _rope(kv[:, :LANES], ck_ref[...], sk_ref[...]).astype(BF16)
        vb_scr[...] = kv[:, LANES:].astype(BF16)
        kv = dkv_ref[0]
        k = kv[:, :LANES]
        k = k * lax.rsqrt(_group_mean_sq(k, onesk_ref[...]) + EPS) * kg_ref[...]
        kd_scr[...] = _rope(k, ck_ref[...], sk_ref[...]).astype(BF16)
        vd_scr[...] = kv[:, LANES:].astype(BF16)

    n_pairs = bq_ref.shape[2] // LANES
    half = n_pairs // 2

    def mix(latent_tile):
        qb = _rope(bq_ref[0], cq_ref[...], sq_ref[...]) * ATTN_SCALE
        qd = dq_ref[0]
        qd = qd * lax.rsqrt(_group_mean_sq(qd, onesq_ref[...]) + EPS) * qg_ref[...]
        qd = _rope(qd, cq_ref[...], sq_ref[...]) * ATTN_SCALE
        common = dict(latent_tile=latent_tile, i=i, tq=tq, seq_t=seq_t, ctx_len=ctx_len)
        win = functools.partial(_attn_pairs, qb, kb_scr, vb_scr, sink_ref, mode="window", **common)
        ob = win(range(half))
        od = _attn_pairs(qd, kd_scr, vd_scr, None, range(n_pairs), mode="global", **common)
        ob = ob + win(range(half, n_pairs))
        yb_ref[0] = _rms(jnp.concatenate(ob, axis=1), gb_ref[...]).astype(BF16)
        yd_ref[0] = _rms(jnp.concatenate(od, axis=1), gd_ref[...]).astype(BF16)
        _cmlp_tile(cu_ref, cv_ref, lng_ref, lnb_ref, ws_ref, bsb_ref, gc_ref, yc_ref)

    pl.when(i < n_ctx_tiles)(functools.partial(mix, False))
    pl.when(i >= n_ctx_tiles)(functools.partial(mix, True))


def _mixers(p, tabs, qg, kg, sink, gb, gd, lng, lnb, ws, bsb, gc, *, blocks, ctx_len):
    b, t, _ = p.shape
    gw = gb.shape[1]
    tq = _pick(t, (256, 128))
    assert ctx_len % tq == 0 and tq % CM_CHUNK == 0
    cq, sq, ck, sk, ones_q, ones_k = tabs
    kern = functools.partial(_mixers_kernel, tq=tq, seq_t=t, ctx_len=ctx_len)
    const = lambda a: _resident(a.shape)
    qspec = lambda blk: pl.BlockSpec((1, tq, gw), lambda bi, i: (bi, i, blk))
    kvspec = lambda blk: pl.BlockSpec((1, t, 2 * LANES), lambda bi, i: (bi, 0, blk))
    tabspec = pl.BlockSpec((tq, gw), lambda bi, i: (i, 0))
    out = jax.ShapeDtypeStruct((b, t, gw), BF16)
    ospec = pl.BlockSpec((1, tq, gw), lambda bi, i: (bi, i, 0))
    kvscr = pltpu.VMEM((t, LANES), BF16)
    return pl.pallas_call(
        kern,
        grid=(b, t // tq),
        in_specs=[qspec(blocks["b_q"]), qspec(blocks["d_q"]), qspec(blocks["c_u"]), qspec(blocks["c_v"]),
                  kvspec(blocks["b_kv"]), kvspec(blocks["d_kv"]),
                  tabspec, tabspec, const(ck), const(sk), const(qg), const(kg),
                  const(ones_q), const(ones_k), const(sink), const(gb), const(gd),
                  const(lng), const(lnb), const(ws), const(bsb), const(gc)],
        out_specs=[ospec, ospec, ospec],
        out_shape=[out, out, out],
        scratch_shapes=[kvscr, kvscr, kvscr, kvscr],
        compiler_params=_cparams(2),
        name="mixers_bcd",
    )(p, p, p, p, p, p, cq, sq, ck, sk, qg, kg, ones_q, ones_k, sink, gb, gd, lng, lnb, ws, bsb, gc)


GLA_LEVELS = int(math.log2(GLA_CHUNK))
GLA_ROWS = GLA_LEVELS * GLA_CHUNK
GLA_STEPS_PER_ITER = 2


def _gla_constants(dk):
    n = GLA_CHUNK
    csum = np.zeros((2, GLA_ROWS, n), np.float32)
    qmask = np.zeros((2, GLA_LEVELS + 1, n, 1), np.float32)
    kmask = np.zeros((2, GLA_LEVELS + 1, n, 1), np.float32)
    smask = np.zeros((2, GLA_LEVELS + 1, n, n), np.float32)
    for d in range(2):
        tau = np.arange(n) if d == 0 else n - 1 - np.arange(n)
        qmask[d, 0] = 1.0
        kmask[d, 0] = 1.0
        smask[d, 0] = np.eye(n)
        for lv in range(1, GLA_LEVELS + 1):
            h = 1 << (lv - 1)
            blk = tau // (2 * h)
            upper = (tau % (2 * h)) >= h
            same = blk[:, None] == blk[None, :]
            both_up = upper[:, None] & upper[None, :]
            both_lo = (~upper[:, None]) & (~upper[None, :])
            c = np.where(upper[:, None],
                         same & both_up & (tau[None, :] <= tau[:, None]),
                         same & both_lo & (tau[None, :] > tau[:, None]))
            csum[d, (lv - 1) * n:lv * n] = c
            qmask[d, lv, :, 0] = upper
            kmask[d, lv, :, 0] = ~upper
            smask[d, lv] = same & upper[:, None] & (~upper[None, :])
    hk = GLA_HEADS * dk
    qmask = np.broadcast_to(qmask, (2, GLA_LEVELS + 1, n, hk)).copy()
    kmask = np.broadcast_to(kmask, (2, GLA_LEVELS + 1, n, hk)).copy()
    smask = np.tile(smask, (1, 1, 1, GLA_HEADS))
    return csum, qmask, kmask, smask


def _gla_kernel(pa_ref, lr_ref, gw_ref, gb_ref, gain_ref, cs_ref, qm_ref, km_ref, sm_ref,
                o_ref, la_scr, oacc_scr, st_scr, *, seq_t, ctx_len, dk, dv):
    hk = GLA_HEADS * dk
    hv = GLA_HEADS * dv
    n = GLA_CHUNK
    nc = seq_t // n
    nc_ctx = ctx_len // n

    z = _dot(lr_ref[0].astype(BF16), gw_ref[...]) + gb_ref[...]
    la_scr[...] = (jnp.minimum(z, 0.0) - jnp.log1p(jnp.exp(-jnp.abs(z)))) * (1.0 / GLA_TAU)
    oacc_scr[...] = jnp.zeros_like(oacc_scr)
    st_scr[...] = jnp.zeros_like(st_scr)

    lane_head = lax.broadcasted_iota(jnp.int32, (n, hk), 1) // dk
    head_lanes = [lane_head == h for h in range(GLA_HEADS)]

    def per_head_rows(x):
        return jnp.concatenate([jnp.where(m, x, 0.0) for m in head_lanes], axis=0).astype(BF16)

    steps = GLA_STEPS_PER_ITER if nc % GLA_STEPS_PER_ITER == 0 else 2
    assert nc % steps == 0
    chains = [(d, u) for u in range(steps) for d in (0, 1)]

    def scan_iter(it, carry):
        r0 = []
        for d, u in chains:
            s = it * steps + u
            c = s if d == 0 else jnp.where(s < nc_ctx, nc_ctx - 1 - s, nc - 1 - (s - nc_ctx))
            r0.append(pl.multiple_of(c * n, n))
        q = [pa_ref[0, pl.ds(r, n), 0:hk] * (dk ** -0.5) for r in r0]
        k = [pa_ref[0, pl.ds(r, n), hk:2 * hk] for r in r0]
        v = [pa_ref[0, pl.ds(r, n), 2 * hk:2 * hk + hv] for r in r0]
        ex2 = []
        a_first = []
        for (d, _), r in zip(chains, r0):
            a = la_scr[pl.ds(r, n), d * hk:(d + 1) * hk]
            a_hi = a.astype(BF16)
            a_lo = (a - a_hi.astype(F32)).astype(BF16)
            ex2.append(_dot(cs_ref[d], jnp.concatenate([a_hi, a_lo], axis=1)))
            a_first.append(a[0:1] if d == 0 else a[n - 1:n])
        ex = [x[:, :hk] + x[:, hk:] for x in ex2]
        e = [jnp.exp(x) for x in ex]

        e_pre, e_suf, e_last = [], [], []
        for ci, (d, _) in enumerate(chains):
            top = ex[ci][(GLA_LEVELS - 1) * n:GLA_LEVELS * n]
            first, final = (0, n - 1) if d == 0 else (n - 1, 0)
            s_lo = top[first:first + 1] + a_first[ci]
            s_up = top[final:final + 1]
            signed = (2.0 * qm_ref[d, GLA_LEVELS] - 1.0) * top
            e_pre.append(jnp.exp(s_lo + signed))
            e_suf.append(jnp.exp(s_up - signed))
            e_last.append(jnp.exp(s_lo + s_up))

        scores = [jnp.zeros((n, GLA_HEADS * n), F32) for _ in chains]
        for lv in range(GLA_LEVELS + 1):
            for ci, (d, _) in enumerate(chains):
                if lv == 0:
                    ql, kl = q[ci], k[ci]
                else:
                    el = e[ci][(lv - 1) * n:lv * n]
                    ql = q[ci] * el * qm_ref[d, lv]
                    kl = k[ci] * el * km_ref[d, lv]
                scores[ci] = scores[ci] + _dot_nt(ql.astype(BF16), per_head_rows(kl)) * sm_ref[d, lv]

        kt = []
        for ci in range(len(chains)):
            kh = k[ci] * e_suf[ci]
            kt.append(jnp.concatenate([kh, jnp.broadcast_to(e_last[ci], (n, hk))], axis=0).T)
        dstate = []
        o_intra = []
        for ci in range(len(chains)):
            vpad = jnp.concatenate([v[ci], jnp.zeros_like(v[ci])], axis=0).astype(BF16)
            full = _dot(kt[ci].astype(BF16), vpad)
            dstate.append(jnp.concatenate(
                [full[h * dk:(h + 1) * dk, h * dv:(h + 1) * dv] for h in range(GLA_HEADS)], axis=0))
            v_rows = jnp.concatenate([v[ci][:, h * dv:(h + 1) * dv] for h in range(GLA_HEADS)],
                                     axis=0).astype(BF16)
            o_intra.append(_dot(per_head_rows(scores[ci]), v_rows))

        st = [st_scr[0], st_scr[1]]
        for ci, (d, _) in enumerate(chains):
            qh = q[ci] * e_pre[ci]
            o_rows = o_intra[ci] + _dot(per_head_rows(qh), st[d].astype(BF16))
            oacc_scr[pl.ds(r0[ci], n), :] += jnp.concatenate(
                [o_rows[h * n:(h + 1) * n, :] for h in range(GLA_HEADS)], axis=1)
            st[d] = st[d] * kt[ci][:, n:n + 1] + dstate[ci]
        st_scr[0] = st[0]
        st_scr[1] = st[1]
        return carry

    lax.fori_loop(0, nc // steps, scan_iter, 0)

    tfin = _pick(seq_t, (256, 128))
    for t0 in range(0, seq_t, tfin):
        o = oacc_scr[t0:t0 + tfin, :]
        parts = [_rms(o[:, h * dv:(h + 1) * dv], gain_ref[:, h * dv:(h + 1) * dv])
                 for h in range(GLA_HEADS)]
        gate = pa_ref[0, t0:t0 + tfin, 2 * hk + hv:2 * hk + 2 * hv]
        o_ref[0, t0:t0 + tfin, :] = (jnp.concatenate(parts, axis=1) * _silu(gate)).astype(BF16)


def _gla(p, gw, gb, gain, consts, *, ctx_len, dk, dv):
    b, t, n = p.shape
    hk = GLA_HEADS * dk
    hv = GLA_HEADS * dv
    pa_w = 2 * hk + 2 * hv
    lr_blk = n // LANES - 1
    cs, qm, km, sm = consts
    kern = functools.partial(_gla_kernel, seq_t=t, ctx_len=ctx_len, dk=dk, dv=dv)
    const = lambda a: _resident(a.shape)
    return pl.pallas_call(
        kern,
        grid=(b,),
        in_specs=[pl.BlockSpec((1, t, pa_w), lambda bi: (bi, 0, 0)),
                  pl.BlockSpec((1, t, LANES), lambda bi: (bi, 0, lr_blk)),
                  const(gw), const(gb), const(gain), const(cs), const(qm), const(km), const(sm)],
        out_specs=pl.BlockSpec((1, t, hv), lambda bi: (bi, 0, 0)),
        out_shape=jax.ShapeDtypeStruct((b, t, hv), BF16),
        scratch_shapes=[pltpu.VMEM((t, 2 * hk), F32), pltpu.VMEM((t, hv), F32),
                        pltpu.VMEM((2, hk, dv), F32)],
        compiler_params=_cparams(1),
        name="gla",
    )(p, p, gw, gb, gain, cs, qm, km, sm)


def _rope_tables(length, ctx_len, n_heads):
    rows = length // GRID_W
    row = jnp.repeat(jnp.arange(rows), GRID_W).astype(F32)
    col = jnp.tile(jnp.arange(GRID_W), rows).astype(F32)
    quarter = HEAD_DIM // 4
    inv = ROPE_THETA ** (-jnp.arange(quarter, dtype=F32) / quarter)
    ar, ac = row[:, None] * inv, col[:, None] * inv
    ang = jnp.concatenate([ar, ar, ac, ac], axis=-1)
    cos, sin = jnp.cos(ang), jnp.sin(ang)
    first = (jnp.arange(HEAD_DIM) % (HEAD_DIM // 2)) < quarter
    sin_signed = jnp.where(first[None, :], -sin, sin)
    cos = jnp.concatenate([jnp.ones((ctx_len, HEAD_DIM), F32), cos], axis=0)
    sin_signed = jnp.concatenate([jnp.zeros((ctx_len, HEAD_DIM), F32), sin_signed], axis=0)
    return jnp.tile(cos, (1, n_heads)), jnp.tile(sin_signed, (1, n_heads))


def _pair_heads(a, axis, n_kv):
    axis = axis % a.ndim
    g = a.shape[axis] // (n_kv * HEAD_DIM)
    shape = a.shape[:axis] + (n_kv, g, HEAD_DIM) + a.shape[axis + 1:]
    return jnp.swapaxes(a.reshape(shape), axis, axis + 1).reshape(a.shape)


def _block_diag_mask(rows, cols, rb, cb):
    r = np.arange(rows)[:, None] // rb
    c = np.arange(cols)[None, :] // cb
    return (r == c).astype(np.float32)


def kernel(x, c, ctx, c_ctx, w_mod, b_mod, norm_g, w_in, gla_gate_up, gla_gate_b, win_sink, cm_ln_g,
           cm_ln_b, cm_ws, cm_bs, qk_g, mix_g, w_out, w_ffn_in, w_ffn_out):
    bsz, seq, d = x.shape
    ctx_len = ctx.shape[1]
    seq_t = ctx_len + seq
    depth = w_mod.shape[0]
    gw = d // 4
    rank = gla_gate_up.shape[2]
    dk = gla_gate_up.shape[3] // GLA_HEADS
    dv = gw // GLA_HEADS
    n_heads = gw // HEAD_DIM
    kvw = WIN_KV * HEAD_DIM
    hk = GLA_HEADS * dk
    assert kvw == LANES and GA_KV == WIN_KV and gw % LANES == 0 and 2 * rank <= LANES
    assert dk == GLA_CHUNK and 2 * hk == gw
    assert ctx_len % CM_CHUNK == 0 and seq % CM_CHUNK == 0 and seq % GRID_W == 0
    assert ctx_len % ROW_CHUNK == 0

    sizes = [hk, hk, gw, gw, rank, rank, gw, kvw, kvw, gw, gw, gw, kvw, kvw]
    off = [int(v) for v in np.concatenate([[0], np.cumsum(sizes)])]
    blocks = {"c_u": 3, "c_v": 4, "b_q": 5, "d_q": 6,
              "b_kv": (7 * gw) // (2 * LANES), "d_kv": (7 * gw) // (2 * LANES) + 1}

    cos_q, sin_q = _rope_tables(seq, ctx_len, n_heads)
    cos_k, sin_k = cos_q[:, :LANES], sin_q[:, :LANES]
    ones_q = jnp.asarray(_block_diag_mask(gw, gw, HEAD_DIM, HEAD_DIM), BF16)
    ones_k = ones_q[:LANES, :LANES]
    tabs = (cos_q, sin_q, cos_k, sin_k, ones_q, ones_k)

    csum, qmask, kmask, smask = _gla_constants(dk)
    gla_consts = (jnp.asarray(csum, BF16), jnp.asarray(qmask), jnp.asarray(kmask), jnp.asarray(smask))

    rows = 16
    cond = jnp.concatenate([c, c_ctx[None, :], jnp.zeros((rows - bsz - 1, d), F32)], axis=0)
    mod_all = _modulation(cond, w_mod, b_mod)
    mod_lat = mod_all[:, :bsz].reshape(depth, bsz, 6, d)
    mod_ctx = jnp.broadcast_to(mod_all[:, bsz].reshape(depth, 1, 6, d), (depth, bsz, 6, d))
    pad2 = jnp.zeros((depth, bsz, 2, d), F32)
    mods = jnp.concatenate([mod_ctx, pad2, mod_lat, pad2], axis=2)

    xs = jnp.concatenate([ctx, x], axis=1).reshape(bsz * seq_t, d)

    w_proj = _prep_w_in(w_in, off, rank)
    zeros_k = jnp.zeros((depth, rank, hk), F32)
    gate_w = jnp.concatenate(
        [jnp.concatenate([gla_gate_up[:, 0], zeros_k], axis=2),
         jnp.concatenate([zeros_k, gla_gate_up[:, 1]], axis=2),
         jnp.zeros((depth, LANES - 2 * rank, 2 * hk), F32)], axis=1).astype(BF16)
    gate_b = gla_gate_b.reshape(depth, 1, 2 * hk)
    g_a, g_c = mix_g[:, None, 0:gw], mix_g[:, None, 2 * gw:3 * gw]
    g_b = _pair_heads(mix_g[:, None, gw:2 * gw], 2, WIN_KV)
    g_d = _pair_heads(mix_g[:, None, 3 * gw:], 2, GA_KV)
    w_o = jnp.concatenate([w_out[:, 0:gw], _pair_heads(w_out[:, gw:2 * gw], 1, WIN_KV),
                           w_out[:, 2 * gw:3 * gw], _pair_heads(w_out[:, 3 * gw:], 1, GA_KV)],
                          axis=1).astype(BF16)
    sink = _pair_heads(jnp.repeat(win_sink, HEAD_DIM, axis=1)[:, None, :], 2, WIN_KV)
    q_gain = jnp.tile(qk_g[:, 0:1], (1, 1, n_heads))
    k_gain = jnp.tile(qk_g[:, 1:2], (1, 1, WIN_KV))
    bsb = jnp.repeat(jnp.swapaxes(cm_bs, 1, 2), gw // CM_GROUPS, axis=2)
    ws = cm_ws.astype(BF16)
    w_up, w_down = w_ffn_in.astype(BF16), w_ffn_out.astype(BF16)
    ln_g, ln_b = cm_ln_g[:, None, :], cm_ln_b[:, None, :]

    for l in range(depth):
        modl = mods[l]
        p = _inproj(xs, modl, norm_g[l, 0:1], w_proj, l, seq_t=seq_t, ctx_len=ctx_len)
        p3 = p.reshape(bsz, seq_t, -1)
        y_a = _gla(p3, gate_w[l], gate_b[l], g_a[l], gla_consts, ctx_len=ctx_len, dk=dk, dv=dv)
        y_b, y_c, y_d = _mixers(p3, tabs, q_gain[l], k_gain[l], sink[l], g_b[l], g_d[l], ln_g[l], ln_b[l],
                                ws[l], bsb[l], g_c[l], blocks=blocks, ctx_len=ctx_len)
        ys = [y.reshape(bsz * seq_t, gw) for y in (y_a, y_b, y_c, y_d)]
        last = l == depth - 1
        xs = _outproj(xs, ys, modl, norm_g[l, 1:2], w_o, l, seq_t=seq_t, ctx_len=ctx_len,
                      latent_only=last)
        xs = _ffn(xs, modl, norm_g[l, 2:3], norm_g[l, 3:4], w_up, w_down, l,
                  rows_per_batch=seq if last else seq_t, row_base=ctx_len if last else 0,
                  ctx_len=ctx_len)

    return xs.reshape(bsz, seq, d)
```

```python
import functools
import math

import numpy as np
import jax
import jax.numpy as jnp
from jax import lax
from jax.experimental import pallas as pl
from jax.experimental.pallas import tpu as pltpu

F32 = jnp.float32
BF16 = jnp.bfloat16

EPS = 1e-6
HEAD_DIM = 64
ATTN_SCALE = HEAD_DIM ** -0.5
LOG2_E = math.log2(math.e)
ROPE_THETA = 10000.0
GRID_W = 64
GLA_HEADS = 4
GLA_TAU = 16.0
GLA_CHUNK = 64
WIN_KV = 2
WINDOW = 128
CM_GROUPS = 4
CM_CHUNK = 128
GA_KV = 2

LANES = 128
NEG_BIG = -1e30
V7X_VMEM_LIMIT = 56 * 1024 * 1024
ROW_CHUNK = 64


def _cparams(n_axes, vmem=V7X_VMEM_LIMIT, flags=None):
    return pltpu.CompilerParams(dimension_semantics=("arbitrary",) * n_axes,
                                vmem_limit_bytes=vmem, flags=flags)


def _pick(total, candidates):
    for cand in candidates:
        if total % cand == 0:
            return cand
    raise ValueError(f"no tile in {candidates} divides {total}")


def _resident(shape):
    return pl.BlockSpec(shape, lambda *_: (0,) * len(shape), pipeline_mode=pl.Buffered(1))


def _dot(a, b):
    return jnp.dot(a, b, preferred_element_type=F32)


def _dot_nt(a, b):
    return lax.dot_general(a, b, (((1,), (1,)), ((), ())), preferred_element_type=F32)


def _rms(x, gain):
    return x * lax.rsqrt(jnp.mean(x * x, axis=-1, keepdims=True) + EPS) * gain


def _silu(x):
    return x * jax.nn.sigmoid(x)


def _gelu_tanh(x):
    return 0.5 * x * (1.0 + jnp.tanh(math.sqrt(2.0 / math.pi) * (x + 0.044715 * (x * x * x))))


def _mod_vec(mod_ref, k, chunk_is_ctx):
    return jnp.where(chunk_is_ctx, mod_ref[0, k:k + 1, :], mod_ref[0, 8 + k:9 + k, :])


def _row_loop(n_rows, step):
    n_chunks = n_rows // ROW_CHUNK
    lax.fori_loop(0, n_chunks, step, 0, unroll=3 if n_chunks % 3 == 0 else 2)


def _norm_modulate_rows(x_ref, h_ref, g_ref, mod_ref, k_shift, k_scale, row0, ctx_len):
    def step(c, carry):
        r = pl.multiple_of(c * ROW_CHUNK, ROW_CHUNK)
        is_ctx = row0 + r < ctx_len
        gain = g_ref[...] * (1.0 + _mod_vec(mod_ref, k_scale, is_ctx))
        x = x_ref[pl.ds(r, ROW_CHUNK), :]
        inv = lax.rsqrt(jnp.mean(x * x, axis=-1, keepdims=True) + EPS)
        h_ref[pl.ds(r, ROW_CHUNK), :] = (x * inv * gain + _mod_vec(mod_ref, k_shift, is_ctx)).astype(BF16)
        return carry

    _row_loop(x_ref.shape[0], step)


def _gated_residual_rows(x_ref, z_ref, o_ref, g_ref, mod_ref, k_gate, row0, ctx_len):
    def step(c, carry):
        r = pl.multiple_of(c * ROW_CHUNK, ROW_CHUNK)
        gain = g_ref[...] * _mod_vec(mod_ref, k_gate, row0 + r < ctx_len)
        z = z_ref[pl.ds(r, ROW_CHUNK), :]
        inv = lax.rsqrt(jnp.mean(z * z, axis=-1, keepdims=True) + EPS)
        o_ref[pl.ds(r, ROW_CHUNK), :] = x_ref[pl.ds(r, ROW_CHUNK), :] + z * inv * gain
        return carry

    _row_loop(x_ref.shape[0], step)


def _mod_kernel(c_ref, w_ref, b_ref, o_ref):
    s = _silu(c_ref[...]).astype(BF16)
    o_ref[0] = _dot(s, w_ref[0].astype(BF16)) + b_ref[0]


def _modulation(cond, w_mod, b_mod):
    depth, d, n = w_mod.shape
    rows = cond.shape[0]
    tn = _pick(n, (1024, 512, 256, 128))
    return pl.pallas_call(
        _mod_kernel,
        grid=(depth, n // tn),
        in_specs=[pl.BlockSpec((rows, d), lambda l, j: (0, 0)),
                  pl.BlockSpec((1, d, tn), lambda l, j: (l, 0, j)),
                  pl.BlockSpec((1, 1, tn), lambda l, j: (l, 0, j))],
        out_specs=pl.BlockSpec((1, rows, tn), lambda l, j: (l, 0, j)),
        out_shape=jax.ShapeDtypeStruct((depth, rows, n), F32),
        compiler_params=_cparams(2),
        name="modulation",
    )(cond, w_mod, b_mod.reshape(depth, 1, n))


def _pair_lanes(y, n_kv):
    g = y.shape[1] // (n_kv * HEAD_DIM)
    cols = [y[:, (kv * g + j) * HEAD_DIM:(kv * g + j + 1) * HEAD_DIM] for j in range(g) for kv in range(n_kv)]
    return jnp.concatenate(cols, axis=1)


def _w_in_kernel(w_ref, o_ref, *, off, rank):
    x = w_ref[...]
    pad = jnp.zeros((x.shape[0], LANES - 2 * rank), F32)
    o_ref[0] = jnp.concatenate(
        [x[:, off[0]:off[4]], x[:, off[9]:off[11]],
         _pair_lanes(x[:, off[6]:off[7]], WIN_KV), _pair_lanes(x[:, off[11]:off[12]], GA_KV),
         x[:, off[7]:off[9]], x[:, off[12]:off[14]], x[:, off[4]:off[6]], pad], axis=1).astype(BF16)


def _prep_w_in(w_in, off, rank):
    depth, d, n_in = w_in.shape
    n_out = n_in - 2 * rank + LANES
    rb = _pick(d, (256, 128))
    kern = functools.partial(_w_in_kernel, off=tuple(off), rank=rank)
    return pl.pallas_call(
        kern,
        grid=(depth, d // rb),
        in_specs=[pl.BlockSpec((rb, n_in), lambda l, i: (l * (d // rb) + i, 0))],
        out_specs=pl.BlockSpec((1, rb, n_out), lambda l, i: (l, i, 0)),
        out_shape=jax.ShapeDtypeStruct((depth, d, n_out), BF16),
        compiler_params=_cparams(2),
        name="prep_w_in",
    )(w_in.reshape(depth * d, n_in))


def _inproj_kernel(x_ref, mod_ref, g_ref, w_ref, p_ref, h_scr, *, tm, tiles_per_batch, ctx_len):
    row0 = (pl.program_id(0) % tiles_per_batch) * tm
    _norm_modulate_rows(x_ref, h_scr, g_ref, mod_ref, 0, 1, row0, ctx_len)
    p_ref[...] = _dot(h_scr[...], w_ref[0])


def _inproj(xs, modl, gain, w_all, layer, *, seq_t, ctx_len):
    m, d = xs.shape
    n = w_all.shape[2]
    tm = _pick(seq_t, (576, 384, 128))
    tpb = seq_t // tm
    kern = functools.partial(_inproj_kernel, tm=tm, tiles_per_batch=tpb, ctx_len=ctx_len)
    return pl.pallas_call(
        kern,
        grid=(m // tm,),
        in_specs=[pl.BlockSpec((tm, d), lambda i: (i, 0)),
                  pl.BlockSpec((1, 16, d), lambda i: (i // tpb, 0, 0)),
                  _resident((1, d)),
                  pl.BlockSpec((1, d, n), lambda i: (layer, 0, 0), pipeline_mode=pl.Buffered(1))],
        out_specs=pl.BlockSpec((tm, n), lambda i: (i, 0)),
        out_shape=jax.ShapeDtypeStruct((m, n), F32),
        scratch_shapes=[pltpu.VMEM((tm, d), BF16)],
        compiler_params=_cparams(1),
        name="inproj",
    )(xs, modl, gain, w_all)


def _outproj_kernel(x_ref, ya_ref, yb_ref, yc_ref, yd_ref, mod_ref, g_ref, w_ref, o_ref, y_scr, *,
                    tm, tiles_per_batch, row_base, ctx_len, gw):
    i = pl.program_id(0)
    y_scr[:, 0 * gw:1 * gw] = ya_ref[...]
    y_scr[:, 1 * gw:2 * gw] = yb_ref[...]
    y_scr[:, 2 * gw:3 * gw] = yc_ref[...]
    y_scr[:, 3 * gw:4 * gw] = yd_ref[...]
    zn = _rms(_dot(y_scr[...], w_ref[0]), g_ref[...])
    row0 = row_base + (i % tiles_per_batch) * tm

    @pl.when(row0 >= ctx_len)
    def _():
        o_ref[...] = x_ref[...] + mod_ref[0, 10:11, :] * zn

    @pl.when(row0 < ctx_len)
    def _():
        is_ctx = row0 + lax.broadcasted_iota(jnp.int32, (tm, 1), 0) < ctx_len
        o_ref[...] = x_ref[...] + jnp.where(is_ctx, mod_ref[0, 2:3, :], mod_ref[0, 10:11, :]) * zn


def _outproj(xs, ys, modl, gain, w, layer, *, seq_t, ctx_len, latent_only):
    m, d = xs.shape
    gw = ys[0].shape[1]
    if latent_only:
        seq = seq_t - ctx_len
        tm = _pick(seq, (512, 256, 128))
        tpb = seq // tm
        n_tiles = (m // seq_t) * tpb
        align = math.gcd(seq_t, ctx_len, tm)
        row_start = lambda i: pl.multiple_of((i // tpb) * seq_t + ctx_len + (i % tpb) * tm, align)
        xspec = pl.BlockSpec((pl.Element(tm), pl.Element(d)), lambda i: (row_start(i), 0))
        yspec = pl.BlockSpec((pl.Element(tm), pl.Element(gw)), lambda i: (row_start(i), 0))
        row_base = ctx_len
    else:
        tm = _pick(seq_t, (576, 384, 128))
        tpb = seq_t // tm
        n_tiles = m // tm
        xspec = pl.BlockSpec((tm, d), lambda i: (i, 0))
        yspec = pl.BlockSpec((tm, gw), lambda i: (i, 0))
        row_base = 0
    kern = functools.partial(_outproj_kernel, tm=tm, tiles_per_batch=tpb, row_base=row_base,
                             ctx_len=ctx_len, gw=gw)
    return pl.pallas_call(
        kern,
        grid=(n_tiles,),
        in_specs=[xspec, yspec, yspec, yspec, yspec,
                  pl.BlockSpec((1, 16, d), lambda i: (i // tpb, 0, 0)),
                  _resident((1, d)),
                  pl.BlockSpec((1, 4 * gw, d), lambda i: (layer, 0, 0), pipeline_mode=pl.Buffered(1))],
        out_specs=pl.BlockSpec((tm, d), lambda i: (i, 0)),
        out_shape=jax.ShapeDtypeStruct((n_tiles * tm, d), F32),
        scratch_shapes=[pltpu.VMEM((tm, 4 * gw), BF16)],
        compiler_params=_cparams(1),
        name="outproj",
    )(xs, *ys, modl, gain, w)


def _ffn_kernel(x_ref, mod_ref, g2_ref, g3_ref, wg_ref, wu_ref, wo_ref, o_ref, h_scr, acc_scr, *,
                tm, tiles_per_batch, row_base, ctx_len, nf):
    i = pl.program_id(0)
    j = pl.program_id(1)
    row0 = row_base + (i % tiles_per_batch) * tm

    @pl.when(j == 0)
    def _():
        _norm_modulate_rows(x_ref, h_scr, g2_ref, mod_ref, 3, 4, row0, ctx_len)
        acc_scr[...] = jnp.zeros_like(acc_scr)

    h = h_scr[...]
    act = (_silu(_dot(h, wg_ref[0])) * _dot(h, wu_ref[0])).astype(BF16)
    acc_scr[...] += _dot(act, wo_ref[0])

    @pl.when(j == nf - 1)
    def _():
        _gated_residual_rows(x_ref, acc_scr, o_ref, g3_ref, mod_ref, 5, row0, ctx_len)


def _ffn(xs, modl, g2, g3, w_in, w_out, layer, *, rows_per_batch, row_base, ctx_len):
    m, d = xs.shape
    dff = w_out.shape[1]
    tm = _pick(rows_per_batch, (768, 512, 384, 128))
    tf = _pick(dff, (512, 256, 128))
    nf = dff // tf
    tpb = rows_per_batch // tm
    kern = functools.partial(_ffn_kernel, tm=tm, tiles_per_batch=tpb, row_base=row_base,
                             ctx_len=ctx_len, nf=nf)
    return pl.pallas_call(
        kern,
        grid=(m // tm, nf),
        in_specs=[pl.BlockSpec((tm, d), lambda i, j: (i, 0)),
                  pl.BlockSpec((1, 16, d), lambda i, j: (i // tpb, 0, 0)),
                  _resident((1, d)), _resident((1, d)),
                  pl.BlockSpec((1, d, tf), lambda i, j: (layer, 0, j)),
                  pl.BlockSpec((1, d, tf), lambda i, j: (layer, 0, nf + j)),
                  pl.BlockSpec((1, tf, d), lambda i, j: (layer, j, 0))],
        out_specs=pl.BlockSpec((tm, d), lambda i, j: (i, 0)),
        out_shape=jax.ShapeDtypeStruct((m, d), F32),
        scratch_shapes=[pltpu.VMEM((tm, d), BF16), pltpu.VMEM((tm, d), F32)],
        compiler_params=_cparams(2),
        name="ffn",
    )(xs, modl, g2, g3, w_in, w_in, w_out)


def _cmlp_tile(u_ref, v_ref, lng_ref, lnb_ref, ws_ref, bsb_ref, gc_ref, o_ref):
    tc, gw = u_ref.shape[1], u_ref.shape[2]
    cw = gw // CM_GROUPS
    u = _gelu_tanh(u_ref[0])
    v = _gelu_tanh(v_ref[0])
    vc = v - jnp.mean(v, axis=-1, keepdims=True)
    vn = vc * lax.rsqrt(jnp.mean(vc * vc, axis=-1, keepdims=True) + EPS) * lng_ref[...] + lnb_ref[...]
    vb = vn.astype(BF16)
    rows = []
    for ch in range(tc // CM_CHUNK):
        cols = []
        for g in range(CM_GROUPS):
            blk = vb[ch * CM_CHUNK:(ch + 1) * CM_CHUNK, g * cw:(g + 1) * cw]
            cols.append(_dot(ws_ref[g], blk) + bsb_ref[:, g * cw:(g + 1) * cw])
        rows.append(jnp.concatenate(cols, axis=1))
    s = jnp.concatenate(rows, axis=0)
    o_ref[0] = _rms(u * s, gc_ref[...]).astype(BF16)


def _group_mean_sq(x, ones_bd):
    sq = x * x
    hi = sq.astype(BF16)
    lo = (sq - hi.astype(F32)).astype(BF16)
    return (_dot(hi, ones_bd) + _dot(lo, ones_bd)) * (1.0 / HEAD_DIM)


def _rope(x, cos, sin_signed):
    w = x.shape[1]
    lane = lax.broadcasted_iota(jnp.int32, x.shape, 1)
    first = (lane % (HEAD_DIM // 2)) < (HEAD_DIM // 4)
    partner = jnp.where(first, pltpu.roll(x, w - HEAD_DIM // 4, 1), pltpu.roll(x, HEAD_DIM // 4, 1))
    return x * cos + partner * sin_signed


def _attn_pairs(q, k_scr, v_scr, sink_ref, pairs, *, mode, latent_tile, i, tq, seq_t, ctx_len):
    lo = lax.broadcasted_iota(jnp.int32, (tq, LANES), 1) < HEAD_DIM
    lane1 = lax.broadcasted_iota(jnp.int32, (1, LANES), 1)
    row2 = lax.broadcasted_iota(jnp.int32, (2 * tq, 1), 0)

    key_sets = [(0, ctx_len, None)]
    if latent_tile and mode == "window":
        wk = tq + 2 * WINDOW
        start = pl.multiple_of(jnp.clip(i * tq - WINDOW, ctx_len, seq_t - wk), LANES)
        r = lax.broadcasted_iota(jnp.int32, (2 * tq, wk), 0)
        r = jnp.where(r >= tq, r - tq, r)
        c = lax.broadcasted_iota(jnp.int32, (2 * tq, wk), 1)
        dist = (i * tq + r) - (start + c)
        key_sets.append((start, wk, (jnp.abs(dist) <= WINDOW)))
    elif latent_tile:
        key_sets = [(0, seq_t, None)]

    outs = []
    for j in pairs:
        qp = q[:, j * LANES:(j + 1) * LANES]
        q2 = jnp.concatenate([jnp.where(lo, qp, 0.0), jnp.where(lo, 0.0, qp)], axis=0).astype(BF16)
        scores = []
        for k0, kn, valid in key_sets:
            s = _dot_nt(q2, k_scr[pl.ds(k0, kn), :])
            scores.append(s if valid is None else jnp.where(valid, s, NEG_BIG))
        mx = functools.reduce(jnp.maximum, [jnp.max(s, axis=1, keepdims=True) for s in scores])
        if mode == "window":
            sv = sink_ref[:, j * LANES:(j + 1) * LANES]
            s_lo = jnp.max(jnp.where(lane1 < HEAD_DIM, sv, NEG_BIG), axis=1, keepdims=True)
            s_hi = jnp.max(jnp.where(lane1 < HEAD_DIM, NEG_BIG, sv), axis=1, keepdims=True)
            sk = jnp.where(row2 < tq, s_lo, s_hi) * LOG2_E
            mx = jnp.maximum(mx, sk)
            den = jnp.exp2(sk - mx)
        else:
            den = 0.0
        o2 = 0.0
        for (k0, kn, _), s in zip(key_sets, scores):
            p = jnp.exp2(s - mx)
            den = den + jnp.sum(p, axis=1, keepdims=True)
            o2 = o2 + _dot(p.astype(BF16), v_scr[pl.ds(k0, kn), :])
        o2 = o2 / den
        outs.append(jnp.where(lo, o2[:tq], o2[tq:]))
    return outs


def _mixers_kernel(bq_ref, dq_ref, cu_ref, cv_ref, bkv_ref, dkv_ref, cq_ref, sq_ref, ck_ref, sk_ref,
                   qg_ref, kg_ref, onesq_ref, onesk_ref, sink_ref, gb_ref, gd_ref,
                   lng_ref, lnb_ref, ws_ref, bsb_ref, gc_ref,
                   yb_ref, yc_ref, yd_ref, kb_scr, vb_scr, kd_scr, vd_scr, *, tq, seq_t, ctx_len):
    i = pl.program_id(1)
    n_ctx_tiles = ctx_len // tq

    @pl.when(i == 0)
    def _():
        kv = bkv_ref[0]
        kb_scr[...] = _rope(kv[:, :LANES], ck_ref[...], sk_ref[...]).astype(BF16)
        vb_scr[...] = kv[:, LANES:].astype(BF16)
        kv = dkv_ref[0]
        k = kv[:, :LANES]
        k = k * lax.rsqrt(_group_mean_sq(k, onesk_ref[...]) + EPS) * kg_ref[...]
        kd_scr[...] = _rope(k, ck_ref[...], sk_ref[...]).astype(BF16)
        vd_scr[...] = kv[:, LANES:].astype(BF16)

    n_pairs = bq_ref.shape[2] // LANES
    half = n_pairs // 2

    def mix(latent_tile):
        qb = _rope(bq_ref[0], cq_ref[...], sq_ref[...]) * (ATTN_SCALE * LOG2_E)
        qd = dq_ref[0]
        qd = qd * lax.rsqrt(_group_mean_sq(qd, onesq_ref[...]) + EPS) * qg_ref[...]
        qd = _rope(qd, cq_ref[...], sq_ref[...]) * (ATTN_SCALE * LOG2_E)
        common = dict(latent_tile=latent_tile, i=i, tq=tq, seq_t=seq_t, ctx_len=ctx_len)
        win = functools.partial(_attn_pairs, qb, kb_scr, vb_scr, sink_ref, mode="window", **common)
        ob = win(range(half))
        od = _attn_pairs(qd, kd_scr, vd_scr, None, range(n_pairs), mode="global", **common)
        ob = ob + win(range(half, n_pairs))
        yb_ref[0] = _rms(jnp.concatenate(ob, axis=1), gb_ref[...]).astype(BF16)
        yd_ref[0] = _rms(jnp.concatenate(od, axis=1), gd_ref[...]).astype(BF16)
        _cmlp_tile(cu_ref, cv_ref, lng_ref, lnb_ref, ws_ref, bsb_ref, gc_ref, yc_ref)

    pl.when(i < n_ctx_tiles)(functools.partial(mix, False))
    pl.when(i >= n_ctx_tiles)(functools.partial(mix, True))


def _mixers(p, tabs, qg, kg, sink, gb, gd, lng, lnb, ws, bsb, gc, *, blocks, ctx_len):
    b, t, _ = p.shape
    gw = gb.shape[1]
    tq = _pick(t, (256, 128))
    assert ctx_len % tq == 0 and tq % CM_CHUNK == 0
    cq, sq, ck, sk, ones_q, ones_k = tabs
    kern = functools.partial(_mixers_kernel, tq=tq, seq_t=t, ctx_len=ctx_len)
    const = lambda a: _resident(a.shape)
    qspec = lambda blk: pl.BlockSpec((1, tq, gw), lambda bi, i: (bi, i, blk))
    kvspec = lambda blk: pl.BlockSpec((1, t, 2 * LANES), lambda bi, i: (bi, 0, blk))
    tabspec = pl.BlockSpec((tq, gw), lambda bi, i: (i, 0))
    out = jax.ShapeDtypeStruct((b, t, gw), BF16)
    ospec = pl.BlockSpec((1, tq, gw), lambda bi, i: (bi, i, 0))
    kvscr = pltpu.VMEM((t, LANES), BF16)
    return pl.pallas_call(
        kern,
        grid=(b, t // tq),
        in_specs=[qspec(blocks["b_q"]), qspec(blocks["d_q"]), qspec(blocks["c_u"]), qspec(blocks["c_v"]),
                  kvspec(blocks["b_kv"]), kvspec(blocks["d_kv"]),
                  tabspec, tabspec, const(ck), const(sk), const(qg), const(kg),
                  const(ones_q), const(ones_k), const(sink), const(gb), const(gd),
                  const(lng), const(lnb), const(ws), const(bsb), const(gc)],
        out_specs=[ospec, ospec, ospec],
        out_shape=[out, out, out],
        scratch_shapes=[kvscr, kvscr, kvscr, kvscr],
        compiler_params=_cparams(2),
        name="mixers_bcd",
    )(p, p, p, p, p, p, cq, sq, ck, sk, qg, kg, ones_q, ones_k, sink, gb, gd, lng, lnb, ws, bsb, gc)


GLA_LEVELS = int(math.log2(GLA_CHUNK))
GLA_ROWS = GLA_LEVELS * GLA_CHUNK
GLA_STEPS_PER_ITER = 4


def _gla_constants(dk):
    n = GLA_CHUNK
    csum = np.zeros((2, GLA_ROWS, n), np.float32)
    qmask = np.zeros((2, GLA_LEVELS + 1, n, 1), np.float32)
    kmask = np.zeros((2, GLA_LEVELS + 1, n, 1), np.float32)
    smask = np.zeros((2, GLA_LEVELS + 1, n, n), np.float32)
    for d in range(2):
        tau = np.arange(n) if d == 0 else n - 1 - np.arange(n)
        qmask[d, 0] = 1.0
        kmask[d, 0] = 1.0
        smask[d, 0] = np.eye(n)
        for lv in range(1, GLA_LEVELS + 1):
            h = 1 << (lv - 1)
            blk = tau // (2 * h)
            upper = (tau % (2 * h)) >= h
            same = blk[:, None] == blk[None, :]
            both_up = upper[:, None] & upper[None, :]
            both_lo = (~upper[:, None]) & (~upper[None, :])
            c = np.where(upper[:, None],
                         same & both_up & (tau[None, :] <= tau[:, None]),
                         same & both_lo & (tau[None, :] > tau[:, None]))
            csum[d, (lv - 1) * n:lv * n] = c
            qmask[d, lv, :, 0] = upper
            kmask[d, lv, :, 0] = ~upper
            smask[d, lv] = same & upper[:, None] & (~upper[None, :])
    hk = GLA_HEADS * dk
    qmask = np.broadcast_to(qmask, (2, GLA_LEVELS + 1, n, hk)).copy()
    kmask = np.broadcast_to(kmask, (2, GLA_LEVELS + 1, n, hk)).copy()
    smask = np.tile(smask, (1, 1, 1, GLA_HEADS))
    return csum, qmask, kmask, smask


def _gla_kernel(pa_ref, lr_ref, gw_ref, gb_ref, gain_ref, cs_ref, qm_ref, km_ref, sm_ref,
                o_ref, la_scr, oacc_scr, st_scr, *, seq_t, ctx_len, dk, dv):
    hk = GLA_HEADS * dk
    hv = GLA_HEADS * dv
    n = GLA_CHUNK
    nc = seq_t // n
    nc_ctx = ctx_len // n

    z = _dot(lr_ref[0].astype(BF16), gw_ref[...]) + gb_ref[...]
    la_scr[...] = (jnp.minimum(z, 0.0) - jnp.log1p(jnp.exp(-jnp.abs(z)))) * (1.0 / GLA_TAU)
    oacc_scr[...] = jnp.zeros_like(oacc_scr)
    st_scr[...] = jnp.zeros_like(st_scr)

    lane_head = lax.broadcasted_iota(jnp.int32, (n, hk), 1) // dk
    head_lanes = [lane_head == h for h in range(GLA_HEADS)]

    def per_head_rows(x):
        return jnp.concatenate([jnp.where(m, x, 0.0) for m in head_lanes], axis=0).astype(BF16)

    steps = GLA_STEPS_PER_ITER if nc % GLA_STEPS_PER_ITER == 0 else 2
    assert nc % steps == 0
    chains = [(d, u) for u in range(steps) for d in (0, 1)]

    def scan_iter(it, carry):
        r0 = []
        for d, u in chains:
            s = it * steps + u
            c = s if d == 0 else jnp.where(s < nc_ctx, nc_ctx - 1 - s, nc - 1 - (s - nc_ctx))
            r0.append(pl.multiple_of(c * n, n))
        q = [pa_ref[0, pl.ds(r, n), 0:hk] * (dk ** -0.5) for r in r0]
        k = [pa_ref[0, pl.ds(r, n), hk:2 * hk] for r in r0]
        v = [pa_ref[0, pl.ds(r, n), 2 * hk:2 * hk + hv] for r in r0]
        ex2 = []
        a_first = []
        for (d, _), r in zip(chains, r0):
            a = la_scr[pl.ds(r, n), d * hk:(d + 1) * hk]
            a_hi = a.astype(BF16)
            a_lo = (a - a_hi.astype(F32)).astype(BF16)
            ex2.append(_dot(cs_ref[d], jnp.concatenate([a_hi, a_lo], axis=1)))
            a_first.append(a[0:1] if d == 0 else a[n - 1:n])
        ex = [x[:, :hk] + x[:, hk:] for x in ex2]
        e = [jnp.exp(x) for x in ex]

        e_pre, e_suf, e_last = [], [], []
        for ci, (d, _) in enumerate(chains):
            top = ex[ci][(GLA_LEVELS - 1) * n:GLA_LEVELS * n]
            first, final = (0, n - 1) if d == 0 else (n - 1, 0)
            s_lo = top[first:first + 1] + a_first[ci]
            s_up = top[final:final + 1]
            signed = (2.0 * qm_ref[d, GLA_LEVELS] - 1.0) * top
            e_pre.append(jnp.exp(s_lo + signed))
            e_suf.append(jnp.exp(s_up - signed))
            e_last.append(jnp.exp(s_lo + s_up))

        scores = [jnp.zeros((n, GLA_HEADS * n), F32) for _ in chains]
        for lv in range(GLA_LEVELS + 1):
            for ci, (d, _) in enumerate(chains):
                if lv == 0:
                    ql, kl = q[ci], k[ci]
                else:
                    el = e[ci][(lv - 1) * n:lv * n]
                    ql = q[ci] * el * qm_ref[d, lv]
                    kl = k[ci] * el * km_ref[d, lv]
                scores[ci] = scores[ci] + _dot_nt(ql.astype(BF16), per_head_rows(kl)) * sm_ref[d, lv]

        kt = []
        for ci in range(len(chains)):
            kh = k[ci] * e_suf[ci]
            kt.append(jnp.concatenate([kh, jnp.broadcast_to(e_last[ci], (n, hk))], axis=0).T)
        dstate = []
        o_intra = []
        for ci in range(len(chains)):
            vpad = jnp.concatenate([v[ci], jnp.zeros_like(v[ci])], axis=0).astype(BF16)
            full = _dot(kt[ci].astype(BF16), vpad)
            dstate.append(jnp.concatenate(
                [full[h * dk:(h + 1) * dk, h * dv:(h + 1) * dv] for h in range(GLA_HEADS)], axis=0))
            v_rows = jnp.concatenate([v[ci][:, h * dv:(h + 1) * dv] for h in range(GLA_HEADS)],
                                     axis=0).astype(BF16)
            o_intra.append(_dot(per_head_rows(scores[ci]), v_rows))

        st = [st_scr[0], st_scr[1]]
        for ci, (d, _) in enumerate(chains):
            qh = q[ci] * e_pre[ci]
            o_rows = o_intra[ci] + _dot(per_head_rows(qh), st[d].astype(BF16))
            oacc_scr[pl.ds(r0[ci], n), :] += jnp.concatenate(
                [o_rows[h * n:(h + 1) * n, :] for h in range(GLA_HEADS)], axis=1)
            st[d] = st[d] * kt[ci][:, n:n + 1] + dstate[ci]
        st_scr[0] = st[0]
        st_scr[1] = st[1]
        return carry

    lax.fori_loop(0, nc // steps, scan_iter, 0)

    tfin = _pick(seq_t, (256, 128))
    for t0 in range(0, seq_t, tfin):
        o = oacc_scr[t0:t0 + tfin, :]
        parts = [_rms(o[:, h * dv:(h + 1) * dv], gain_ref[:, h * dv:(h + 1) * dv])
                 for h in range(GLA_HEADS)]
        gate = pa_ref[0, t0:t0 + tfin, 2 * hk + hv:2 * hk + 2 * hv]
        o_ref[0, t0:t0 + tfin, :] = (jnp.concatenate(parts, axis=1) * _silu(gate)).astype(BF16)


def _gla(p, gw, gb, gain, consts, *, ctx_len, dk, dv):
    b, t, n = p.shape
    hk = GLA_HEADS * dk
    hv = GLA_HEADS * dv
    pa_w = 2 * hk + 2 * hv
    lr_blk = n // LANES - 1
    cs, qm, km, sm = consts
    kern = functools.partial(_gla_kernel, seq_t=t, ctx_len=ctx_len, dk=dk, dv=dv)
    const = lambda a: _resident(a.shape)
    return pl.pallas_call(
        kern,
        grid=(b,),
        in_specs=[pl.BlockSpec((1, t, pa_w), lambda bi: (bi, 0, 0)),
                  pl.BlockSpec((1, t, LANES), lambda bi: (bi, 0, lr_blk)),
                  const(gw), const(gb), const(gain), const(cs), const(qm), const(km), const(sm)],
        out_specs=pl.BlockSpec((1, t, hv), lambda bi: (bi, 0, 0)),
        out_shape=jax.ShapeDtypeStruct((b, t, hv), BF16),
        scratch_shapes=[pltpu.VMEM((t, 2 * hk), F32), pltpu.VMEM((t, hv), F32),
                        pltpu.VMEM((2, hk, dv), F32)],
        compiler_params=_cparams(1),
        name="gla",
    )(p, p, gw, gb, gain, cs, qm, km, sm)


def _rope_tables(length, ctx_len, n_heads):
    rows = length // GRID_W
    row = jnp.repeat(jnp.arange(rows), GRID_W).astype(F32)
    col = jnp.tile(jnp.arange(GRID_W), rows).astype(F32)
    quarter = HEAD_DIM // 4
    inv = ROPE_THETA ** (-jnp.arange(quarter, dtype=F32) / quarter)
    ar, ac = row[:, None] * inv, col[:, None] * inv
    ang = jnp.concatenate([ar, ar, ac, ac], axis=-1)
    cos, sin = jnp.cos(ang), jnp.sin(ang)
    first = (jnp.arange(HEAD_DIM) % (HEAD_DIM // 2)) < quarter
    sin_signed = jnp.where(first[None, :], -sin, sin)
    cos = jnp.concatenate([jnp.ones((ctx_len, HEAD_DIM), F32), cos], axis=0)
    sin_signed = jnp.concatenate([jnp.zeros((ctx_len, HEAD_DIM), F32), sin_signed], axis=0)
    return jnp.tile(cos, (1, n_heads)), jnp.tile(sin_signed, (1, n_heads))


def _pair_heads(a, axis, n_kv):
    axis = axis % a.ndim
    g = a.shape[axis] // (n_kv * HEAD_DIM)
    shape = a.shape[:axis] + (n_kv, g, HEAD_DIM) + a.shape[axis + 1:]
    return jnp.swapaxes(a.reshape(shape), axis, axis + 1).reshape(a.shape)


def _block_diag_mask(rows, cols, rb, cb):
    r = np.arange(rows)[:, None] // rb
    c = np.arange(cols)[None, :] // cb
    return (r == c).astype(np.float32)


def kernel(x, c, ctx, c_ctx, w_mod, b_mod, norm_g, w_in, gla_gate_up, gla_gate_b, win_sink, cm_ln_g,
           cm_ln_b, cm_ws, cm_bs, qk_g, mix_g, w_out, w_ffn_in, w_ffn_out):
    bsz, seq, d = x.shape
    ctx_len = ctx.shape[1]
    seq_t = ctx_len + seq
    depth = w_mod.shape[0]
    gw = d // 4
    rank = gla_gate_up.shape[2]
    dk = gla_gate_up.shape[3] // GLA_HEADS
    dv = gw // GLA_HEADS
    n_heads = gw // HEAD_DIM
    kvw = WIN_KV * HEAD_DIM
    hk = GLA_HEADS * dk
    assert kvw == LANES and GA_KV == WIN_KV and gw % LANES == 0 and 2 * rank <= LANES
    assert dk == GLA_CHUNK and 2 * hk == gw
    assert ctx_len % CM_CHUNK == 0 and seq % CM_CHUNK == 0 and seq % GRID_W == 0
    assert ctx_len % ROW_CHUNK == 0

    sizes = [hk, hk, gw, gw, rank, rank, gw, kvw, kvw, gw, gw, gw, kvw, kvw]
    off = [int(v) for v in np.concatenate([[0], np.cumsum(sizes)])]
    blocks = {"c_u": 3, "c_v": 4, "b_q": 5, "d_q": 6,
              "b_kv": (7 * gw) // (2 * LANES), "d_kv": (7 * gw) // (2 * LANES) + 1}

    cos_q, sin_q = _rope_tables(seq, ctx_len, n_heads)
    cos_k, sin_k = cos_q[:, :LANES], sin_q[:, :LANES]
    ones_q = jnp.asarray(_block_diag_mask(gw, gw, HEAD_DIM, HEAD_DIM), BF16)
    ones_k = ones_q[:LANES, :LANES]
    tabs = (cos_q, sin_q, cos_k, sin_k, ones_q, ones_k)

    csum, qmask, kmask, smask = _gla_constants(dk)
    gla_consts = (jnp.asarray(csum, BF16), jnp.asarray(qmask), jnp.asarray(kmask), jnp.asarray(smask))

    rows = 16
    cond = jnp.concatenate([c, c_ctx[None, :], jnp.zeros((rows - bsz - 1, d), F32)], axis=0)
    mod_all = _modulation(cond, w_mod, b_mod)
    mod_lat = mod_all[:, :bsz].reshape(depth, bsz, 6, d)
    mod_ctx = jnp.broadcast_to(mod_all[:, bsz].reshape(depth, 1, 6, d), (depth, bsz, 6, d))
    pad2 = jnp.zeros((depth, bsz, 2, d), F32)
    mods = jnp.concatenate([mod_ctx, pad2, mod_lat, pad2], axis=2)

    xs = jnp.concatenate([ctx, x], axis=1).reshape(bsz * seq_t, d)

    w_proj = _prep_w_in(w_in, off, rank)
    zeros_k = jnp.zeros((depth, rank, hk), F32)
    gate_w = jnp.concatenate(
        [jnp.concatenate([gla_gate_up[:, 0], zeros_k], axis=2),
         jnp.concatenate([zeros_k, gla_gate_up[:, 1]], axis=2),
         jnp.zeros((depth, LANES - 2 * rank, 2 * hk), F32)], axis=1).astype(BF16)
    gate_b = gla_gate_b.reshape(depth, 1, 2 * hk)
    g_a, g_c = mix_g[:, None, 0:gw], mix_g[:, None, 2 * gw:3 * gw]
    g_b = _pair_heads(mix_g[:, None, gw:2 * gw], 2, WIN_KV)
    g_d = _pair_heads(mix_g[:, None, 3 * gw:], 2, GA_KV)
    w_o = jnp.concatenate([w_out[:, 0:gw], _pair_heads(w_out[:, gw:2 * gw], 1, WIN_KV),
                           w_out[:, 2 * gw:3 * gw], _pair_heads(w_out[:, 3 * gw:], 1, GA_KV)],
                          axis=1).astype(BF16)
    sink = _pair_heads(jnp.repeat(win_sink, HEAD_DIM, axis=1)[:, None, :], 2, WIN_KV)
    q_gain = jnp.tile(qk_g[:, 0:1], (1, 1, n_heads))
    k_gain = jnp.tile(qk_g[:, 1:2], (1, 1, WIN_KV))
    bsb = jnp.repeat(jnp.swapaxes(cm_bs, 1, 2), gw // CM_GROUPS, axis=2)
    ws = cm_ws.astype(BF16)
    w_up, w_down = w_ffn_in.astype(BF16), w_ffn_out.astype(BF16)
    ln_g, ln_b = cm_ln_g[:, None, :], cm_ln_b[:, None, :]

    for l in range(depth):
        modl = mods[l]
        p = _inproj(xs, modl, norm_g[l, 0:1], w_proj, l, seq_t=seq_t, ctx_len=ctx_len)
        p3 = p.reshape(bsz, seq_t, -1)
        y_a = _gla(p3, gate_w[l], gate_b[l], g_a[l], gla_consts, ctx_len=ctx_len, dk=dk, dv=dv)
        y_b, y_c, y_d = _mixers(p3, tabs, q_gain[l], k_gain[l], sink[l], g_b[l], g_d[l], ln_g[l], ln_b[l],
                                ws[l], bsb[l], g_c[l], blocks=blocks, ctx_len=ctx_len)
        ys = [y.reshape(bsz * seq_t, gw) for y in (y_a, y_b, y_c, y_d)]
        last = l == depth - 1
        xs = _outproj(xs, ys, modl, norm_g[l, 1:2], w_o, l, seq_t=seq_t, ctx_len=ctx_len,
                      latent_only=last)
        xs = _ffn(xs, modl, norm_g[l, 2:3], norm_g[l, 3:4], w_up, w_down, l,
                  rows_per_batch=seq if last else seq_t, row_base=ctx_len if last else 0,
                  ctx_len=ctx_len)

    return xs.reshape(bsz, seq, d)
```

```python
import functools
import math

import numpy as np
import jax
import jax.numpy as jnp
from jax import lax
from jax.experimental import pallas as pl
from jax.experimental.pallas import tpu as pltpu

F32 = jnp.float32
BF16 = jnp.bfloat16

EPS = 1e-6
HEAD_DIM = 64
ATTN_SCALE = HEAD_DIM ** -0.5
LOG2_E = math.log2(math.e)
ROPE_THETA = 10000.0
GRID_W = 64
GLA_HEADS = 4
GLA_TAU = 16.0
GLA_CHUNK = 64
WIN_KV = 2
WINDOW = 128
CM_GROUPS = 4
CM_CHUNK = 128
GA_KV = 2

LANES = 128
NEG_BIG = -1e30
V7X_VMEM_LIMIT = 56 * 1024 * 1024
ROW_CHUNK = 64


def _cparams(n_axes, vmem=V7X_VMEM_LIMIT, flags=None):
    return pltpu.CompilerParams(dimension_semantics=("arbitrary",) * n_axes,
                                vmem_limit_bytes=vmem, flags=flags)


def _pick(total, candidates):
    for cand in candidates:
        if total % cand == 0:
            return cand
    raise ValueError(f"no tile in {candidates} divides {total}")


def _resident(shape):
    return pl.BlockSpec(shape, lambda *_: (0,) * len(shape), pipeline_mode=pl.Buffered(1))


def _dot(a, b):
    return jnp.dot(a, b, preferred_element_type=F32)


def _dot_nt(a, b):
    return lax.dot_general(a, b, (((1,), (1,)), ((), ())), preferred_element_type=F32)


def _rms(x, gain):
    return x * lax.rsqrt(jnp.mean(x * x, axis=-1, keepdims=True) + EPS) * gain


def _silu(x):
    return x * jax.nn.sigmoid(x)


def _gelu_tanh(x):
    return 0.5 * x * (1.0 + jnp.tanh(math.sqrt(2.0 / math.pi) * (x + 0.044715 * (x * x * x))))


def _mod_vec(mod_ref, k, chunk_is_ctx):
    return jnp.where(chunk_is_ctx, mod_ref[0, k:k + 1, :], mod_ref[0, 8 + k:9 + k, :])


def _row_loop(n_rows, step):
    n_chunks = n_rows // ROW_CHUNK
    lax.fori_loop(0, n_chunks, step, 0, unroll=3 if n_chunks % 3 == 0 else 2)


def _norm_modulate_rows(x_ref, h_ref, g_ref, mod_ref, k_shift, k_scale, row0, ctx_len):
    def step(c, carry):
        r = pl.multiple_of(c * ROW_CHUNK, ROW_CHUNK)
        is_ctx = row0 + r < ctx_len
        gain = g_ref[...] * (1.0 + _mod_vec(mod_ref, k_scale, is_ctx))
        x = x_ref[pl.ds(r, ROW_CHUNK), :]
        inv = lax.rsqrt(jnp.mean(x * x, axis=-1, keepdims=True) + EPS)
        h_ref[pl.ds(r, ROW_CHUNK), :] = (x * inv * gain + _mod_vec(mod_ref, k_shift, is_ctx)).astype(BF16)
        return carry

    _row_loop(x_ref.shape[0], step)


def _gated_residual_rows(x_ref, z_ref, o_ref, g_ref, mod_ref, k_gate, row0, ctx_len):
    def step(c, carry):
        r = pl.multiple_of(c * ROW_CHUNK, ROW_CHUNK)
        gain = g_ref[...] * _mod_vec(mod_ref, k_gate, row0 + r < ctx_len)
        z = z_ref[pl.ds(r, ROW_CHUNK), :]
        inv = lax.rsqrt(jnp.mean(z * z, axis=-1, keepdims=True) + EPS)
        o_ref[pl.ds(r, ROW_CHUNK), :] = x_ref[pl.ds(r, ROW_CHUNK), :] + z * inv * gain
        return carry

    _row_loop(x_ref.shape[0], step)


def _mod_kernel(c_ref, w_ref, b_ref, o_ref):
    s = _silu(c_ref[...]).astype(BF16)
    o_ref[0] = _dot(s, w_ref[0].astype(BF16)) + b_ref[0]


def _modulation(cond, w_mod, b_mod):
    depth, d, n = w_mod.shape
    rows = cond.shape[0]
    tn = _pick(n, (1024, 512, 256, 128))
    return pl.pallas_call(
        _mod_kernel,
        grid=(depth, n // tn),
        in_specs=[pl.BlockSpec((rows, d), lambda l, j: (0, 0)),
                  pl.BlockSpec((1, d, tn), lambda l, j: (l, 0, j)),
                  pl.BlockSpec((1, 1, tn), lambda l, j: (l, 0, j))],
        out_specs=pl.BlockSpec((1, rows, tn), lambda l, j: (l, 0, j)),
        out_shape=jax.ShapeDtypeStruct((depth, rows, n), F32),
        compiler_params=_cparams(2),
        name="modulation",
    )(cond, w_mod, b_mod.reshape(depth, 1, n))


def _pair_lanes(y, n_kv):
    g = y.shape[1] // (n_kv * HEAD_DIM)
    cols = [y[:, (kv * g + j) * HEAD_DIM:(kv * g + j + 1) * HEAD_DIM] for j in range(g) for kv in range(n_kv)]
    return jnp.concatenate(cols, axis=1)


def _w_in_kernel(w_ref, o_ref, *, off, rank):
    x = w_ref[...]
    pad = jnp.zeros((x.shape[0], LANES - 2 * rank), F32)
    o_ref[0] = jnp.concatenate(
        [x[:, off[0]:off[4]], x[:, off[9]:off[11]],
         _pair_lanes(x[:, off[6]:off[7]], WIN_KV), _pair_lanes(x[:, off[11]:off[12]], GA_KV),
         x[:, off[7]:off[9]], x[:, off[12]:off[14]], x[:, off[4]:off[6]], pad], axis=1).astype(BF16)


def _prep_w_in(w_in, off, rank):
    depth, d, n_in = w_in.shape
    n_out = n_in - 2 * rank + LANES
    rb = _pick(d, (256, 128))
    kern = functools.partial(_w_in_kernel, off=tuple(off), rank=rank)
    return pl.pallas_call(
        kern,
        grid=(depth, d // rb),
        in_specs=[pl.BlockSpec((rb, n_in), lambda l, i: (l * (d // rb) + i, 0))],
        out_specs=pl.BlockSpec((1, rb, n_out), lambda l, i: (l, i, 0)),
        out_shape=jax.ShapeDtypeStruct((depth, d, n_out), BF16),
        compiler_params=_cparams(2),
        name="prep_w_in",
    )(w_in.reshape(depth * d, n_in))


def _inproj_kernel(x_ref, mod_ref, g_ref, w_ref, p_ref, h_scr, *, tm, tiles_per_batch, ctx_len):
    row0 = (pl.program_id(0) % tiles_per_batch) * tm
    _norm_modulate_rows(x_ref, h_scr, g_ref, mod_ref, 0, 1, row0, ctx_len)
    p_ref[...] = _dot(h_scr[...], w_ref[0])


def _inproj(xs, modl, gain, w_all, layer, *, seq_t, ctx_len):
    m, d = xs.shape
    n = w_all.shape[2]
    tm = _pick(seq_t, (576, 384, 128))
    tpb = seq_t // tm
    kern = functools.partial(_inproj_kernel, tm=tm, tiles_per_batch=tpb, ctx_len=ctx_len)
    return pl.pallas_call(
        kern,
        grid=(m // tm,),
        in_specs=[pl.BlockSpec((tm, d), lambda i: (i, 0)),
                  pl.BlockSpec((1, 16, d), lambda i: (i // tpb, 0, 0)),
                  _resident((1, d)),
                  pl.BlockSpec((1, d, n), lambda i: (layer, 0, 0), pipeline_mode=pl.Buffered(1))],
        out_specs=pl.BlockSpec((tm, n), lambda i: (i, 0)),
        out_shape=jax.ShapeDtypeStruct((m, n), F32),
        scratch_shapes=[pltpu.VMEM((tm, d), BF16)],
        compiler_params=_cparams(1),
        name="inproj",
    )(xs, modl, gain, w_all)


def _outproj_kernel(x_ref, ya_ref, yb_ref, yc_ref, yd_ref, mod_ref, g_ref, w_ref, o_ref, y_scr, *,
                    tm, tiles_per_batch, row_base, ctx_len, gw):
    i = pl.program_id(0)
    y_scr[:, 0 * gw:1 * gw] = ya_ref[...]
    y_scr[:, 1 * gw:2 * gw] = yb_ref[...]
    y_scr[:, 2 * gw:3 * gw] = yc_ref[...]
    y_scr[:, 3 * gw:4 * gw] = yd_ref[...]
    zn = _rms(_dot(y_scr[...], w_ref[0]), g_ref[...])
    row0 = row_base + (i % tiles_per_batch) * tm

    @pl.when(row0 >= ctx_len)
    def _():
        o_ref[...] = x_ref[...] + mod_ref[0, 10:11, :] * zn

    @pl.when(row0 < ctx_len)
    def _():
        is_ctx = row0 + lax.broadcasted_iota(jnp.int32, (tm, 1), 0) < ctx_len
        o_ref[...] = x_ref[...] + jnp.where(is_ctx, mod_ref[0, 2:3, :], mod_ref[0, 10:11, :]) * zn


def _outproj(xs, ys, modl, gain, w, layer, *, seq_t, ctx_len, latent_only):
    m, d = xs.shape
    gw = ys[0].shape[1]
    if latent_only:
        seq = seq_t - ctx_len
        tm = _pick(seq, (512, 256, 128))
        tpb = seq // tm
        n_tiles = (m // seq_t) * tpb
        align = math.gcd(seq_t, ctx_len, tm)
        row_start = lambda i: pl.multiple_of((i // tpb) * seq_t + ctx_len + (i % tpb) * tm, align)
        xspec = pl.BlockSpec((pl.Element(tm), pl.Element(d)), lambda i: (row_start(i), 0))
        yspec = pl.BlockSpec((pl.Element(tm), pl.Element(gw)), lambda i: (row_start(i), 0))
        row_base = ctx_len
    else:
        tm = _pick(seq_t, (576, 384, 128))
        tpb = seq_t // tm
        n_tiles = m // tm
        xspec = pl.BlockSpec((tm, d), lambda i: (i, 0))
        yspec = pl.BlockSpec((tm, gw), lambda i: (i, 0))
        row_base = 0
    kern = functools.partial(_outproj_kernel, tm=tm, tiles_per_batch=tpb, row_base=row_base,
                             ctx_len=ctx_len, gw=gw)
    return pl.pallas_call(
        kern,
        grid=(n_tiles,),
        in_specs=[xspec, yspec, yspec, yspec, yspec,
                  pl.BlockSpec((1, 16, d), lambda i: (i // tpb, 0, 0)),
                  _resident((1, d)),
                  pl.BlockSpec((1, 4 * gw, d), lambda i: (layer, 0, 0), pipeline_mode=pl.Buffered(1))],
        out_specs=pl.BlockSpec((tm, d), lambda i: (i, 0)),
        out_shape=jax.ShapeDtypeStruct((n_tiles * tm, d), F32),
        scratch_shapes=[pltpu.VMEM((tm, 4 * gw), BF16)],
        compiler_params=_cparams(1),
        name="outproj",
    )(xs, *ys, modl, gain, w)


def _ffn_kernel(x_ref, mod_ref, g2_ref, g3_ref, wg_ref, wu_ref, wo_ref, o_ref, h_scr, acc_scr, *,
                tm, tiles_per_batch, row_base, ctx_len, nf):
    i = pl.program_id(0)
    j = pl.program_id(1)
    row0 = row_base + (i % tiles_per_batch) * tm

    @pl.when(j == 0)
    def _():
        _norm_modulate_rows(x_ref, h_scr, g2_ref, mod_ref, 3, 4, row0, ctx_len)
        acc_scr[...] = jnp.zeros_like(acc_scr)

    h = h_scr[...]
    act = (_silu(_dot(h, wg_ref[0])) * _dot(h, wu_ref[0])).astype(BF16)
    acc_scr[...] += _dot(act, wo_ref[0])

    @pl.when(j == nf - 1)
    def _():
        _gated_residual_rows(x_ref, acc_scr, o_ref, g3_ref, mod_ref, 5, row0, ctx_len)


def _ffn(xs, modl, g2, g3, w_in, w_out, layer, *, rows_per_batch, row_base, ctx_len):
    m, d = xs.shape
    dff = w_out.shape[1]
    tm = _pick(rows_per_batch, (768, 512, 384, 128))
    tf = _pick(dff, (512, 256, 128))
    nf = dff // tf
    tpb = rows_per_batch // tm
    kern = functools.partial(_ffn_kernel, tm=tm, tiles_per_batch=tpb, row_base=row_base,
                             ctx_len=ctx_len, nf=nf)
    return pl.pallas_call(
        kern,
        grid=(m // tm, nf),
        in_specs=[pl.BlockSpec((tm, d), lambda i, j: (i, 0)),
                  pl.BlockSpec((1, 16, d), lambda i, j: (i // tpb, 0, 0)),
                  _resident((1, d)), _resident((1, d)),
                  pl.BlockSpec((1, d, tf), lambda i, j: (layer, 0, j)),
                  pl.BlockSpec((1, d, tf), lambda i, j: (layer, 0, nf + j)),
                  pl.BlockSpec((1, tf, d), lambda i, j: (layer, j, 0))],
        out_specs=pl.BlockSpec((tm, d), lambda i, j: (i, 0)),
        out_shape=jax.ShapeDtypeStruct((m, d), F32),
        scratch_shapes=[pltpu.VMEM((tm, d), BF16), pltpu.VMEM((tm, d), F32)],
        compiler_params=_cparams(2),
        name="ffn",
    )(xs, modl, g2, g3, w_in, w_in, w_out)


def _cmlp_tile(u_ref, v_ref, lng_ref, lnb_ref, ws_ref, bsb_ref, gc_ref, o_ref):
    tc, gw = u_ref.shape[1], u_ref.shape[2]
    cw = gw // CM_GROUPS
    u = _gelu_tanh(u_ref[0])
    v = _gelu_tanh(v_ref[0])
    vc = v - jnp.mean(v, axis=-1, keepdims=True)
    vn = vc * lax.rsqrt(jnp.mean(vc * vc, axis=-1, keepdims=True) + EPS) * lng_ref[...] + lnb_ref[...]
    vb = vn.astype(BF16)
    rows = []
    for ch in range(tc // CM_CHUNK):
        cols = []
        for g in range(CM_GROUPS):
            blk = vb[ch * CM_CHUNK:(ch + 1) * CM_CHUNK, g * cw:(g + 1) * cw]
            cols.append(_dot(ws_ref[g], blk) + bsb_ref[:, g * cw:(g + 1) * cw])
        rows.append(jnp.concatenate(cols, axis=1))
    s = jnp.concatenate(rows, axis=0)
    o_ref[0] = _rms(u * s, gc_ref[...]).astype(BF16)


def _group_mean_sq(x, ones_bd):
    sq = x * x
    hi = sq.astype(BF16)
    lo = (sq - hi.astype(F32)).astype(BF16)
    return (_dot(hi, ones_bd) + _dot(lo, ones_bd)) * (1.0 / HEAD_DIM)


def _rope(x, cos, sin_signed):
    w = x.shape[1]
    lane = lax.broadcasted_iota(jnp.int32, x.shape, 1)
    first = (lane % (HEAD_DIM // 2)) < (HEAD_DIM // 4)
    partner = jnp.where(first, pltpu.roll(x, w - HEAD_DIM // 4, 1), pltpu.roll(x, HEAD_DIM // 4, 1))
    return x * cos + partner * sin_signed


def _attn_pairs(q, k_scr, v_scr, sink_ref, pairs, *, mode, latent_tile, i, tq, seq_t, ctx_len):
    lo = lax.broadcasted_iota(jnp.int32, (tq, LANES), 1) < HEAD_DIM
    lane1 = lax.broadcasted_iota(jnp.int32, (1, LANES), 1)
    row2 = lax.broadcasted_iota(jnp.int32, (2 * tq, 1), 0)

    key_sets = [(0, ctx_len, None)]
    if latent_tile and mode == "window":
        wk = tq + 2 * WINDOW
        start = pl.multiple_of(jnp.clip(i * tq - WINDOW, ctx_len, seq_t - wk), LANES)
        r = lax.broadcasted_iota(jnp.int32, (2 * tq, wk), 0)
        r = jnp.where(r >= tq, r - tq, r)
        c = lax.broadcasted_iota(jnp.int32, (2 * tq, wk), 1)
        dist = (i * tq + r) - (start + c)
        key_sets.append((start, wk, (jnp.abs(dist) <= WINDOW)))
    elif latent_tile:
        key_sets = [(0, seq_t, None)]

    outs = []
    for j in pairs:
        qp = q[:, j * LANES:(j + 1) * LANES]
        q2 = jnp.concatenate([jnp.where(lo, qp, 0.0), jnp.where(lo, 0.0, qp)], axis=0).astype(BF16)
        scores = []
        for k0, kn, valid in key_sets:
            s = _dot_nt(q2, k_scr[pl.ds(k0, kn), :])
            scores.append(s if valid is None else jnp.where(valid, s, NEG_BIG))
        mx = functools.reduce(jnp.maximum, [jnp.max(s, axis=1, keepdims=True) for s in scores])
        if mode == "window":
            sv = sink_ref[:, j * LANES:(j + 1) * LANES]
            s_lo = jnp.max(jnp.where(lane1 < HEAD_DIM, sv, NEG_BIG), axis=1, keepdims=True)
            s_hi = jnp.max(jnp.where(lane1 < HEAD_DIM, NEG_BIG, sv), axis=1, keepdims=True)
            sk = jnp.where(row2 < tq, s_lo, s_hi) * LOG2_E
            mx = jnp.maximum(mx, sk)
            den = jnp.exp2(sk - mx)
        else:
            den = 0.0
        o2 = 0.0
        for (k0, kn, _), s in zip(key_sets, scores):
            p = jnp.exp2(s - mx)
            den = den + jnp.sum(p, axis=1, keepdims=True)
            o2 = o2 + _dot(p.astype(BF16), v_scr[pl.ds(k0, kn), :])
        o2 = o2 / den
        outs.append(jnp.where(lo, o2[:tq], o2[tq:]))
    return outs


def _mixers_kernel(bq_ref, dq_ref, cu_ref, cv_ref, bkv_ref, dkv_ref, cq_ref, sq_ref, ck_ref, sk_ref,
                   qg_ref, kg_ref, onesq_ref, onesk_ref, sink_ref, gb_ref, gd_ref,
                   lng_ref, lnb_ref, ws_ref, bsb_ref, gc_ref,
                   yb_ref, yc_ref, yd_ref, kb_scr, vb_scr, kd_scr, vd_scr, *, tq, seq_t, ctx_len, need_ctx):
    i = pl.program_id(1)
    n_ctx_tiles = ctx_len // tq

    @pl.when(i == 0)
    def _():
        kv = bkv_ref[0]
        kb_scr[...] = _rope(kv[:, :LANES], ck_ref[...], sk_ref[...]).astype(BF16)
        vb_scr[...] = kv[:, LANES:].astype(BF16)
        kv = dkv_ref[0]
        k = kv[:, :LANES]
        k = k * lax.rsqrt(_group_mean_sq(k, onesk_ref[...]) + EPS) * kg_ref[...]
        kd_scr[...] = _rope(k, ck_ref[...], sk_ref[...]).astype(BF16)
        vd_scr[...] = kv[:, LANES:].astype(BF16)

    n_pairs = bq_ref.shape[2] // LANES
    half = n_pairs // 2

    def mix(latent_tile):
        qb = _rope(bq_ref[0], cq_ref[...], sq_ref[...]) * (ATTN_SCALE * LOG2_E)
        qd = dq_ref[0]
        qd = qd * lax.rsqrt(_group_mean_sq(qd, onesq_ref[...]) + EPS) * qg_ref[...]
        qd = _rope(qd, cq_ref[...], sq_ref[...]) * (ATTN_SCALE * LOG2_E)
        common = dict(latent_tile=latent_tile, i=i, tq=tq, seq_t=seq_t, ctx_len=ctx_len)
        win = functools.partial(_attn_pairs, qb, kb_scr, vb_scr, sink_ref, mode="window", **common)
        ob = win(range(half))
        od = _attn_pairs(qd, kd_scr, vd_scr, None, range(n_pairs), mode="global", **common)
        ob = ob + win(range(half, n_pairs))
        yb_ref[0] = _rms(jnp.concatenate(ob, axis=1), gb_ref[...]).astype(BF16)
        yd_ref[0] = _rms(jnp.concatenate(od, axis=1), gd_ref[...]).astype(BF16)
        _cmlp_tile(cu_ref, cv_ref, lng_ref, lnb_ref, ws_ref, bsb_ref, gc_ref, yc_ref)

    def no_ctx_output():
        for y_ref in (yb_ref, yc_ref, yd_ref):
            y_ref[...] = jnp.zeros_like(y_ref)

    pl.when(i < n_ctx_tiles)(functools.partial(mix, False) if need_ctx else no_ctx_output)
    pl.when(i >= n_ctx_tiles)(functools.partial(mix, True))


def _mixers(p, tabs, qg, kg, sink, gb, gd, lng, lnb, ws, bsb, gc, *, blocks, ctx_len, need_ctx):
    b, t, _ = p.shape
    gw = gb.shape[1]
    tq = _pick(t, (256, 128))
    assert ctx_len % tq == 0 and tq % CM_CHUNK == 0
    cq, sq, ck, sk, ones_q, ones_k = tabs
    kern = functools.partial(_mixers_kernel, tq=tq, seq_t=t, ctx_len=ctx_len, need_ctx=need_ctx)
    const = lambda a: _resident(a.shape)
    qspec = lambda blk: pl.BlockSpec((1, tq, gw), lambda bi, i: (bi, i, blk))
    kvspec = lambda blk: pl.BlockSpec((1, t, 2 * LANES), lambda bi, i: (bi, 0, blk))
    tabspec = pl.BlockSpec((tq, gw), lambda bi, i: (i, 0))
    out = jax.ShapeDtypeStruct((b, t, gw), BF16)
    ospec = pl.BlockSpec((1, tq, gw), lambda bi, i: (bi, i, 0))
    kvscr = pltpu.VMEM((t, LANES), BF16)
    return pl.pallas_call(
        kern,
        grid=(b, t // tq),
        in_specs=[qspec(blocks["b_q"]), qspec(blocks["d_q"]), qspec(blocks["c_u"]), qspec(blocks["c_v"]),
                  kvspec(blocks["b_kv"]), kvspec(blocks["d_kv"]),
                  tabspec, tabspec, const(ck), const(sk), const(qg), const(kg),
                  const(ones_q), const(ones_k), const(sink), const(gb), const(gd),
                  const(lng), const(lnb), const(ws), const(bsb), const(gc)],
        out_specs=[ospec, ospec, ospec],
        out_shape=[out, out, out],
        scratch_shapes=[kvscr, kvscr, kvscr, kvscr],
        compiler_params=_cparams(2),
        name="mixers_bcd",
    )(p, p, p, p, p, p, cq, sq, ck, sk, qg, kg, ones_q, ones_k, sink, gb, gd, lng, lnb, ws, bsb, gc)


GLA_LEVELS = int(math.log2(GLA_CHUNK))
GLA_ROWS = GLA_LEVELS * GLA_CHUNK
GLA_STEPS_PER_ITER = 4


def _gla_constants(dk):
    n = GLA_CHUNK
    csum = np.zeros((2, GLA_ROWS, n), np.float32)
    qmask = np.zeros((2, GLA_LEVELS + 1, n, 1), np.float32)
    kmask = np.zeros((2, GLA_LEVELS + 1, n, 1), np.float32)
    smask = np.zeros((2, GLA_LEVELS + 1, n, n), np.float32)
    for d in range(2):
        tau = np.arange(n) if d == 0 else n - 1 - np.arange(n)
        qmask[d, 0] = 1.0
        kmask[d, 0] = 1.0
        smask[d, 0] = np.eye(n)
        for lv in range(1, GLA_LEVELS + 1):
            h = 1 << (lv - 1)
            blk = tau // (2 * h)
            upper = (tau % (2 * h)) >= h
            same = blk[:, None] == blk[None, :]
            both_up = upper[:, None] & upper[None, :]
            both_lo = (~upper[:, None]) & (~upper[None, :])
            c = np.where(upper[:, None],
                         same & both_up & (tau[None, :] <= tau[:, None]),
                         same & both_lo & (tau[None, :] > tau[:, None]))
            csum[d, (lv - 1) * n:lv * n] = c
            qmask[d, lv, :, 0] = upper
            kmask[d, lv, :, 0] = ~upper
            smask[d, lv] = same & upper[:, None] & (~upper[None, :])
    hk = GLA_HEADS * dk
    qmask = np.broadcast_to(qmask, (2, GLA_LEVELS + 1, n, hk)).copy()
    kmask = np.broadcast_to(kmask, (2, GLA_LEVELS + 1, n, hk)).copy()
    smask = np.tile(smask, (1, 1, 1, GLA_HEADS))
    return csum, qmask, kmask, smask


def _gla_kernel(pa_ref, lr_ref, gw_ref, gb_ref, gain_ref, cs_ref, qm_ref, km_ref, sm_ref,
                o_ref, la_scr, oacc_scr, st_scr, *, seq_t, ctx_len, dk, dv):
    hk = GLA_HEADS * dk
    hv = GLA_HEADS * dv
    n = GLA_CHUNK
    nc = seq_t // n
    nc_ctx = ctx_len // n

    z = _dot(lr_ref[0].astype(BF16), gw_ref[...]) + gb_ref[...]
    la_scr[...] = (jnp.minimum(z, 0.0) - jnp.log1p(jnp.exp(-jnp.abs(z)))) * (LOG2_E / GLA_TAU)
    oacc_scr[...] = jnp.zeros_like(oacc_scr)
    st_scr[...] = jnp.zeros_like(st_scr)

    lane_head = lax.broadcasted_iota(jnp.int32, (n, hk), 1) // dk
    head_lanes = [lane_head == h for h in range(GLA_HEADS)]

    def per_head_rows(x):
        return jnp.concatenate([jnp.where(m, x, 0.0) for m in head_lanes], axis=0).astype(BF16)

    steps = GLA_STEPS_PER_ITER if nc % GLA_STEPS_PER_ITER == 0 else 2
    assert nc % steps == 0
    chains = [(d, u) for u in range(steps) for d in (0, 1)]

    def scan_iter(it, carry):
        r0 = []
        for d, u in chains:
            s = it * steps + u
            c = s if d == 0 else jnp.where(s < nc_ctx, nc_ctx - 1 - s, nc - 1 - (s - nc_ctx))
            r0.append(pl.multiple_of(c * n, n))
        q = [pa_ref[0, pl.ds(r, n), 0:hk] * (dk ** -0.5) for r in r0]
        k = [pa_ref[0, pl.ds(r, n), hk:2 * hk] for r in r0]
        v = [pa_ref[0, pl.ds(r, n), 2 * hk:2 * hk + hv] for r in r0]
        ex2 = []
        a_first = []
        for (d, _), r in zip(chains, r0):
            a = la_scr[pl.ds(r, n), d * hk:(d + 1) * hk]
            a_hi = a.astype(BF16)
            a_lo = (a - a_hi.astype(F32)).astype(BF16)
            ex2.append(_dot(cs_ref[d], jnp.concatenate([a_hi, a_lo], axis=1)))
            a_first.append(a[0:1] if d == 0 else a[n - 1:n])
        ex = [x[:, :hk] + x[:, hk:] for x in ex2]
        e = [jnp.exp2(x) for x in ex]

        e_pre, e_suf, e_last = [], [], []
        for ci, (d, _) in enumerate(chains):
            top = ex[ci][(GLA_LEVELS - 1) * n:GLA_LEVELS * n]
            first, final = (0, n - 1) if d == 0 else (n - 1, 0)
            s_lo = top[first:first + 1] + a_first[ci]
            s_up = top[final:final + 1]
            signed = (2.0 * qm_ref[d, GLA_LEVELS] - 1.0) * top
            e_pre.append(jnp.exp2(s_lo + signed))
            e_suf.append(jnp.exp2(s_up - signed))
            e_last.append(jnp.exp2(s_lo + s_up))

        scores = [jnp.zeros((n, GLA_HEADS * n), F32) for _ in chains]
        for lv in range(GLA_LEVELS + 1):
            for ci, (d, _) in enumerate(chains):
                if lv == 0:
                    ql, kl = q[ci], k[ci]
                else:
                    el = e[ci][(lv - 1) * n:lv * n]
                    ql = q[ci] * el * qm_ref[d, lv]
                    kl = k[ci] * el * km_ref[d, lv]
                scores[ci] = scores[ci] + _dot_nt(ql.astype(BF16), per_head_rows(kl)) * sm_ref[d, lv]

        kt = []
        for ci in range(len(chains)):
            kh = k[ci] * e_suf[ci]
            kt.append(jnp.concatenate([kh, jnp.broadcast_to(e_last[ci], (n, hk))], axis=0).T)
        dstate = []
        o_intra = []
        for ci in range(len(chains)):
            vpad = jnp.concatenate([v[ci], jnp.zeros_like(v[ci])], axis=0).astype(BF16)
            full = _dot(kt[ci].astype(BF16), vpad)
            dstate.append(jnp.concatenate(
                [full[h * dk:(h + 1) * dk, h * dv:(h + 1) * dv] for h in range(GLA_HEADS)], axis=0))
            v_rows = jnp.concatenate([v[ci][:, h * dv:(h + 1) * dv] for h in range(GLA_HEADS)],
                                     axis=0).astype(BF16)
            o_intra.append(_dot(per_head_rows(scores[ci]), v_rows))

        st = [st_scr[0], st_scr[1]]
        for ci, (d, _) in enumerate(chains):
            qh = q[ci] * e_pre[ci]
            o_rows = o_intra[ci] + _dot(per_head_rows(qh), st[d].astype(BF16))
            oacc_scr[pl.ds(r0[ci], n), :] += jnp.concatenate(
                [o_rows[h * n:(h + 1) * n, :] for h in range(GLA_HEADS)], axis=1)
            st[d] = st[d] * kt[ci][:, n:n + 1] + dstate[ci]
        st_scr[0] = st[0]
        st_scr[1] = st[1]
        return carry

    lax.fori_loop(0, nc // steps, scan_iter, 0)

    tfin = _pick(seq_t, (256, 128))
    for t0 in range(0, seq_t, tfin):
        o = oacc_scr[t0:t0 + tfin, :]
        parts = [_rms(o[:, h * dv:(h + 1) * dv], gain_ref[:, h * dv:(h + 1) * dv])
                 for h in range(GLA_HEADS)]
        gate = pa_ref[0, t0:t0 + tfin, 2 * hk + hv:2 * hk + 2 * hv]
        o_ref[0, t0:t0 + tfin, :] = (jnp.concatenate(parts, axis=1) * _silu(gate)).astype(BF16)


def _gla(p, gw, gb, gain, consts, *, ctx_len, dk, dv):
    b, t, n = p.shape
    hk = GLA_HEADS * dk
    hv = GLA_HEADS * dv
    pa_w = 2 * hk + 2 * hv
    lr_blk = n // LANES - 1
    cs, qm, km, sm = consts
    kern = functools.partial(_gla_kernel, seq_t=t, ctx_len=ctx_len, dk=dk, dv=dv)
    const = lambda a: _resident(a.shape)
    return pl.pallas_call(
        kern,
        grid=(b,),
        in_specs=[pl.BlockSpec((1, t, pa_w), lambda bi: (bi, 0, 0)),
                  pl.BlockSpec((1, t, LANES), lambda bi: (bi, 0, lr_blk)),
                  const(gw), const(gb), const(gain), const(cs), const(qm), const(km), const(sm)],
        out_specs=pl.BlockSpec((1, t, hv), lambda bi: (bi, 0, 0)),
        out_shape=jax.ShapeDtypeStruct((b, t, hv), BF16),
        scratch_shapes=[pltpu.VMEM((t, 2 * hk), F32), pltpu.VMEM((t, hv), F32),
                        pltpu.VMEM((2, hk, dv), F32)],
        compiler_params=_cparams(1),
        name="gla",
    )(p, p, gw, gb, gain, cs, qm, km, sm)


def _rope_tables(length, ctx_len, n_heads):
    rows = length // GRID_W
    row = jnp.repeat(jnp.arange(rows), GRID_W).astype(F32)
    col = jnp.tile(jnp.arange(GRID_W), rows).astype(F32)
    quarter = HEAD_DIM // 4
    inv = ROPE_THETA ** (-jnp.arange(quarter, dtype=F32) / quarter)
    ar, ac = row[:, None] * inv, col[:, None] * inv
    ang = jnp.concatenate([ar, ar, ac, ac], axis=-1)
    cos, sin = jnp.cos(ang), jnp.sin(ang)
    first = (jnp.arange(HEAD_DIM) % (HEAD_DIM // 2)) < quarter
    sin_signed = jnp.where(first[None, :], -sin, sin)
    cos = jnp.concatenate([jnp.ones((ctx_len, HEAD_DIM), F32), cos], axis=0)
    sin_signed = jnp.concatenate([jnp.zeros((ctx_len, HEAD_DIM), F32), sin_signed], axis=0)
    return jnp.tile(cos, (1, n_heads)), jnp.tile(sin_signed, (1, n_heads))


def _pair_heads(a, axis, n_kv):
    axis = axis % a.ndim
    g = a.shape[axis] // (n_kv * HEAD_DIM)
    shape = a.shape[:axis] + (n_kv, g, HEAD_DIM) + a.shape[axis + 1:]
    return jnp.swapaxes(a.reshape(shape), axis, axis + 1).reshape(a.shape)


def _block_diag_mask(rows, cols, rb, cb):
    r = np.arange(rows)[:, None] // rb
    c = np.arange(cols)[None, :] // cb
    return (r == c).astype(np.float32)


def kernel(x, c, ctx, c_ctx, w_mod, b_mod, norm_g, w_in, gla_gate_up, gla_gate_b, win_sink, cm_ln_g,
           cm_ln_b, cm_ws, cm_bs, qk_g, mix_g, w_out, w_ffn_in, w_ffn_out):
    bsz, seq, d = x.shape
    ctx_len = ctx.shape[1]
    seq_t = ctx_len + seq
    depth = w_mod.shape[0]
    gw = d // 4
    rank = gla_gate_up.shape[2]
    dk = gla_gate_up.shape[3] // GLA_HEADS
    dv = gw // GLA_HEADS
    n_heads = gw // HEAD_DIM
    kvw = WIN_KV * HEAD_DIM
    hk = GLA_HEADS * dk
    assert kvw == LANES and GA_KV == WIN_KV and gw % LANES == 0 and 2 * rank <= LANES
    assert dk == GLA_CHUNK and 2 * hk == gw
    assert ctx_len % CM_CHUNK == 0 and seq % CM_CHUNK == 0 and seq % GRID_W == 0
    assert ctx_len % ROW_CHUNK == 0

    sizes = [hk, hk, gw, gw, rank, rank, gw, kvw, kvw, gw, gw, gw, kvw, kvw]
    off = [int(v) for v in np.concatenate([[0], np.cumsum(sizes)])]
    blocks = {"c_u": 3, "c_v": 4, "b_q": 5, "d_q": 6,
              "b_kv": (7 * gw) // (2 * LANES), "d_kv": (7 * gw) // (2 * LANES) + 1}

    cos_q, sin_q = _rope_tables(seq, ctx_len, n_heads)
    cos_k, sin_k = cos_q[:, :LANES], sin_q[:, :LANES]
    ones_q = jnp.asarray(_block_diag_mask(gw, gw, HEAD_DIM, HEAD_DIM), BF16)
    ones_k = ones_q[:LANES, :LANES]
    tabs = (cos_q, sin_q, cos_k, sin_k, ones_q, ones_k)

    csum, qmask, kmask, smask = _gla_constants(dk)
    gla_consts = (jnp.asarray(csum, BF16), jnp.asarray(qmask), jnp.asarray(kmask), jnp.asarray(smask))

    rows = 16
    cond = jnp.concatenate([c, c_ctx[None, :], jnp.zeros((rows - bsz - 1, d), F32)], axis=0)
    mod_all = _modulation(cond, w_mod, b_mod)
    mod_lat = mod_all[:, :bsz].reshape(depth, bsz, 6, d)
    mod_ctx = jnp.broadcast_to(mod_all[:, bsz].reshape(depth, 1, 6, d), (depth, bsz, 6, d))
    pad2 = jnp.zeros((depth, bsz, 2, d), F32)
    mods = jnp.concatenate([mod_ctx, pad2, mod_lat, pad2], axis=2)

    xs = jnp.concatenate([ctx, x], axis=1).reshape(bsz * seq_t, d)

    w_proj = _prep_w_in(w_in, off, rank)
    zeros_k = jnp.zeros((depth, rank, hk), F32)
    gate_w = jnp.concatenate(
        [jnp.concatenate([gla_gate_up[:, 0], zeros_k], axis=2),
         jnp.concatenate([zeros_k, gla_gate_up[:, 1]], axis=2),
         jnp.zeros((depth, LANES - 2 * rank, 2 * hk), F32)], axis=1).astype(BF16)
    gate_b = gla_gate_b.reshape(depth, 1, 2 * hk)
    g_a, g_c = mix_g[:, None, 0:gw], mix_g[:, None, 2 * gw:3 * gw]
    g_b = _pair_heads(mix_g[:, None, gw:2 * gw], 2, WIN_KV)
    g_d = _pair_heads(mix_g[:, None, 3 * gw:], 2, GA_KV)
    w_o = jnp.concatenate([w_out[:, 0:gw], _pair_heads(w_out[:, gw:2 * gw], 1, WIN_KV),
                           w_out[:, 2 * gw:3 * gw], _pair_heads(w_out[:, 3 * gw:], 1, GA_KV)],
                          axis=1).astype(BF16)
    sink = _pair_heads(jnp.repeat(win_sink, HEAD_DIM, axis=1)[:, None, :], 2, WIN_KV)
    q_gain = jnp.tile(qk_g[:, 0:1], (1, 1, n_heads))
    k_gain = jnp.tile(qk_g[:, 1:2], (1, 1, WIN_KV))
    bsb = jnp.repeat(jnp.swapaxes(cm_bs, 1, 2), gw // CM_GROUPS, axis=2)
    ws = cm_ws.astype(BF16)
    w_up, w_down = w_ffn_in.astype(BF16), w_ffn_out.astype(BF16)
    ln_g, ln_b = cm_ln_g[:, None, :], cm_ln_b[:, None, :]

    for l in range(depth):
        modl = mods[l]
        p = _inproj(xs, modl, norm_g[l, 0:1], w_proj, l, seq_t=seq_t, ctx_len=ctx_len)
        p3 = p.reshape(bsz, seq_t, -1)
        last = l == depth - 1
        y_a = _gla(p3, gate_w[l], gate_b[l], g_a[l], gla_consts, ctx_len=ctx_len, dk=dk, dv=dv)
        y_b, y_c, y_d = _mixers(p3, tabs, q_gain[l], k_gain[l], sink[l], g_b[l], g_d[l], ln_g[l], ln_b[l],
                                ws[l], bsb[l], g_c[l], blocks=blocks, ctx_len=ctx_len, need_ctx=not last)
        ys = [y.reshape(bsz * seq_t, gw) for y in (y_a, y_b, y_c, y_d)]
        xs = _outproj(xs, ys, modl, norm_g[l, 1:2], w_o, l, seq_t=seq_t, ctx_len=ctx_len,
                      latent_only=last)
        xs = _ffn(xs, modl, norm_g[l, 2:3], norm_g[l, 3:4], w_up, w_down, l,
                  rows_per_batch=seq if last else seq_t, row_base=ctx_len if last else 0,
                  ctx_len=ctx_len)

    return xs.reshape(bsz, seq, d)
```

```python
import functools
import math

import numpy as np
import jax
import jax.numpy as jnp
from jax import lax
from jax.experimental import pallas as pl
from jax.experimental.pallas import tpu as pltpu

F32 = jnp.float32
BF16 = jnp.bfloat16

EPS = 1e-6
HEAD_DIM = 64
ATTN_SCALE = HEAD_DIM ** -0.5
LOG2_E = math.log2(math.e)
ROPE_THETA = 10000.0
GRID_W = 64
GLA_HEADS = 4
GLA_TAU = 16.0
GLA_CHUNK = 64
WIN_KV = 2
WINDOW = 128
CM_GROUPS = 4
CM_CHUNK = 128
GA_KV = 2

LANES = 128
NEG_BIG = -1e30
V7X_VMEM_LIMIT = 56 * 1024 * 1024
ROW_CHUNK = 64


def _cparams(n_axes, vmem=V7X_VMEM_LIMIT, flags=None):
    return pltpu.CompilerParams(dimension_semantics=("arbitrary",) * n_axes,
                                vmem_limit_bytes=vmem, flags=flags)


def _pick(total, candidates):
    for cand in candidates:
        if total % cand == 0:
            return cand
    raise ValueError(f"no tile in {candidates} divides {total}")


def _resident(shape):
    return pl.BlockSpec(shape, lambda *_: (0,) * len(shape), pipeline_mode=pl.Buffered(1))


def _dot(a, b):
    return jnp.dot(a, b, preferred_element_type=F32)


def _dot_nt(a, b):
    return lax.dot_general(a, b, (((1,), (1,)), ((), ())), preferred_element_type=F32)


def _rms(x, gain):
    return x * lax.rsqrt(jnp.mean(x * x, axis=-1, keepdims=True) + EPS) * gain


def _silu(x):
    return x * jax.nn.sigmoid(x)


def _gelu_tanh(x):
    return 0.5 * x * (1.0 + jnp.tanh(math.sqrt(2.0 / math.pi) * (x + 0.044715 * (x * x * x))))


def _mod_vec(mod_ref, k, chunk_is_ctx):
    return jnp.where(chunk_is_ctx, mod_ref[0, k:k + 1, :], mod_ref[0, 8 + k:9 + k, :])


def _row_loop(n_rows, step):
    n_chunks = n_rows // ROW_CHUNK
    lax.fori_loop(0, n_chunks, step, 0, unroll=3 if n_chunks % 3 == 0 else 2)


def _norm_modulate_rows(x_ref, h_ref, g_ref, mod_ref, k_shift, k_scale, row0, ctx_len):
    def step(c, carry):
        r = pl.multiple_of(c * ROW_CHUNK, ROW_CHUNK)
        is_ctx = row0 + r < ctx_len
        gain = g_ref[...] * (1.0 + _mod_vec(mod_ref, k_scale, is_ctx))
        x = x_ref[pl.ds(r, ROW_CHUNK), :]
        inv = lax.rsqrt(jnp.mean(x * x, axis=-1, keepdims=True) + EPS)
        h_ref[pl.ds(r, ROW_CHUNK), :] = (x * inv * gain + _mod_vec(mod_ref, k_shift, is_ctx)).astype(BF16)
        return carry

    _row_loop(x_ref.shape[0], step)


def _gated_residual_rows(x_ref, z_ref, o_ref, g_ref, mod_ref, k_gate, row0, ctx_len):
    def step(c, carry):
        r = pl.multiple_of(c * ROW_CHUNK, ROW_CHUNK)
        gain = g_ref[...] * _mod_vec(mod_ref, k_gate, row0 + r < ctx_len)
        z = z_ref[pl.ds(r, ROW_CHUNK), :]
        inv = lax.rsqrt(jnp.mean(z * z, axis=-1, keepdims=True) + EPS)
        o_ref[pl.ds(r, ROW_CHUNK), :] = x_ref[pl.ds(r, ROW_CHUNK), :] + z * inv * gain
        return carry

    _row_loop(x_ref.shape[0], step)


def _mod_kernel(c_ref, w_ref, b_ref, o_ref):
    s = _silu(c_ref[...]).astype(BF16)
    o_ref[0] = _dot(s, w_ref[0].astype(BF16)) + b_ref[0]


def _modulation(cond, w_mod, b_mod):
    depth, d, n = w_mod.shape
    rows = cond.shape[0]
    tn = _pick(n, (1024, 512, 256, 128))
    return pl.pallas_call(
        _mod_kernel,
        grid=(depth, n // tn),
        in_specs=[pl.BlockSpec((rows, d), lambda l, j: (0, 0)),
                  pl.BlockSpec((1, d, tn), lambda l, j: (l, 0, j)),
                  pl.BlockSpec((1, 1, tn), lambda l, j: (l, 0, j))],
        out_specs=pl.BlockSpec((1, rows, tn), lambda l, j: (l, 0, j)),
        out_shape=jax.ShapeDtypeStruct((depth, rows, n), F32),
        compiler_params=_cparams(2),
        name="modulation",
    )(cond, w_mod, b_mod.reshape(depth, 1, n))


def _pair_lanes(y, n_kv):
    g = y.shape[1] // (n_kv * HEAD_DIM)
    cols = [y[:, (kv * g + j) * HEAD_DIM:(kv * g + j + 1) * HEAD_DIM] for j in range(g) for kv in range(n_kv)]
    return jnp.concatenate(cols, axis=1)


def _w_in_kernel(w_ref, o_ref, *, off, rank):
    x = w_ref[...]
    pad = jnp.zeros((x.shape[0], LANES - 2 * rank), F32)
    o_ref[0] = jnp.concatenate(
        [x[:, off[0]:off[4]], x[:, off[9]:off[11]],
         _pair_lanes(x[:, off[6]:off[7]], WIN_KV), _pair_lanes(x[:, off[11]:off[12]], GA_KV),
         x[:, off[7]:off[9]], x[:, off[12]:off[14]], x[:, off[4]:off[6]], pad], axis=1).astype(BF16)


def _prep_w_in(w_in, off, rank):
    depth, d, n_in = w_in.shape
    n_out = n_in - 2 * rank + LANES
    rb = _pick(d, (256, 128))
    kern = functools.partial(_w_in_kernel, off=tuple(off), rank=rank)
    return pl.pallas_call(
        kern,
        grid=(depth, d // rb),
        in_specs=[pl.BlockSpec((rb, n_in), lambda l, i: (l * (d // rb) + i, 0))],
        out_specs=pl.BlockSpec((1, rb, n_out), lambda l, i: (l, i, 0)),
        out_shape=jax.ShapeDtypeStruct((depth, d, n_out), BF16),
        compiler_params=_cparams(2),
        name="prep_w_in",
    )(w_in.reshape(depth * d, n_in))


def _inproj_kernel(x_ref, mod_ref, g_ref, w_ref, p_ref, h_scr, *, tm, tiles_per_batch, ctx_len):
    row0 = (pl.program_id(0) % tiles_per_batch) * tm
    _norm_modulate_rows(x_ref, h_scr, g_ref, mod_ref, 0, 1, row0, ctx_len)
    p_ref[...] = _dot(h_scr[...], w_ref[0])


def _inproj(xs, modl, gain, w_all, layer, *, seq_t, ctx_len):
    m, d = xs.shape
    n = w_all.shape[2]
    tm = _pick(seq_t, (576, 384, 128))
    tpb = seq_t // tm
    kern = functools.partial(_inproj_kernel, tm=tm, tiles_per_batch=tpb, ctx_len=ctx_len)
    return pl.pallas_call(
        kern,
        grid=(m // tm,),
        in_specs=[pl.BlockSpec((tm, d), lambda i: (i, 0)),
                  pl.BlockSpec((1, 16, d), lambda i: (i // tpb, 0, 0)),
                  _resident((1, d)),
                  pl.BlockSpec((1, d, n), lambda i: (layer, 0, 0), pipeline_mode=pl.Buffered(1))],
        out_specs=pl.BlockSpec((tm, n), lambda i: (i, 0)),
        out_shape=jax.ShapeDtypeStruct((m, n), F32),
        scratch_shapes=[pltpu.VMEM((tm, d), BF16)],
        compiler_params=_cparams(1),
        name="inproj",
    )(xs, modl, gain, w_all)


def _outproj_kernel(x_ref, ya_ref, yb_ref, yc_ref, yd_ref, mod_ref, g_ref, w_ref, o_ref, y_scr, *,
                    tm, tiles_per_batch, row_base, ctx_len, gw):
    i = pl.program_id(0)
    y_scr[:, 0 * gw:1 * gw] = ya_ref[...]
    y_scr[:, 1 * gw:2 * gw] = yb_ref[...]
    y_scr[:, 2 * gw:3 * gw] = yc_ref[...]
    y_scr[:, 3 * gw:4 * gw] = yd_ref[...]
    zn = _rms(_dot(y_scr[...], w_ref[0]), g_ref[...])
    row0 = row_base + (i % tiles_per_batch) * tm

    @pl.when(row0 >= ctx_len)
    def _():
        o_ref[...] = x_ref[...] + mod_ref[0, 10:11, :] * zn

    @pl.when(row0 < ctx_len)
    def _():
        is_ctx = row0 + lax.broadcasted_iota(jnp.int32, (tm, 1), 0) < ctx_len
        o_ref[...] = x_ref[...] + jnp.where(is_ctx, mod_ref[0, 2:3, :], mod_ref[0, 10:11, :]) * zn


def _outproj(xs, ys, modl, gain, w, layer, *, seq_t, ctx_len, latent_only):
    m, d = xs.shape
    gw = ys[0].shape[1]
    if latent_only:
        seq = seq_t - ctx_len
        tm = _pick(seq, (512, 256, 128))
        tpb = seq // tm
        n_tiles = (m // seq_t) * tpb
        align = math.gcd(seq_t, ctx_len, tm)
        row_start = lambda i: pl.multiple_of((i // tpb) * seq_t + ctx_len + (i % tpb) * tm, align)
        xspec = pl.BlockSpec((pl.Element(tm), pl.Element(d)), lambda i: (row_start(i), 0))
        yspec = pl.BlockSpec((pl.Element(tm), pl.Element(gw)), lambda i: (row_start(i), 0))
        row_base = ctx_len
    else:
        tm = _pick(seq_t, (576, 384, 128))
        tpb = seq_t // tm
        n_tiles = m // tm
        xspec = pl.BlockSpec((tm, d), lambda i: (i, 0))
        yspec = pl.BlockSpec((tm, gw), lambda i: (i, 0))
        row_base = 0
    kern = functools.partial(_outproj_kernel, tm=tm, tiles_per_batch=tpb, row_base=row_base,
                             ctx_len=ctx_len, gw=gw)
    return pl.pallas_call(
        kern,
        grid=(n_tiles,),
        in_specs=[xspec, yspec, yspec, yspec, yspec,
                  pl.BlockSpec((1, 16, d), lambda i: (i // tpb, 0, 0)),
                  _resident((1, d)),
                  pl.BlockSpec((1, 4 * gw, d), lambda i: (layer, 0, 0), pipeline_mode=pl.Buffered(1))],
        out_specs=pl.BlockSpec((tm, d), lambda i: (i, 0)),
        out_shape=jax.ShapeDtypeStruct((n_tiles * tm, d), F32),
        scratch_shapes=[pltpu.VMEM((tm, 4 * gw), BF16)],
        compiler_params=_cparams(1),
        name="outproj",
    )(xs, *ys, modl, gain, w)


def _ffn_kernel(x_ref, mod_ref, g2_ref, g3_ref, wg_ref, wu_ref, wo_ref, o_ref, h_scr, acc_scr, *,
                tm, tiles_per_batch, row_base, ctx_len, nf):
    i = pl.program_id(0)
    j = pl.program_id(1)
    row0 = row_base + (i % tiles_per_batch) * tm

    def hidden_tile():
        h = h_scr[...]
        act = (_silu(_dot(h, wg_ref[0])) * _dot(h, wu_ref[0])).astype(BF16)
        return _dot(act, wo_ref[0])

    @pl.when(j == 0)
    def _():
        _norm_modulate_rows(x_ref, h_scr, g2_ref, mod_ref, 3, 4, row0, ctx_len)
        acc_scr[...] = hidden_tile()

    @pl.when(j > 0)
    def _():
        acc_scr[...] += hidden_tile()

    @pl.when(j == nf - 1)
    def _():
        _gated_residual_rows(x_ref, acc_scr, o_ref, g3_ref, mod_ref, 5, row0, ctx_len)


def _ffn(xs, modl, g2, g3, w_in, w_out, layer, *, rows_per_batch, row_base, ctx_len):
    m, d = xs.shape
    dff = w_out.shape[1]
    tm = _pick(rows_per_batch, (768, 512, 384, 128))
    tf = _pick(dff, (512, 256, 128))
    nf = dff // tf
    tpb = rows_per_batch // tm
    kern = functools.partial(_ffn_kernel, tm=tm, tiles_per_batch=tpb, row_base=row_base,
                             ctx_len=ctx_len, nf=nf)
    return pl.pallas_call(
        kern,
        grid=(m // tm, nf),
        in_specs=[pl.BlockSpec((tm, d), lambda i, j: (i, 0)),
                  pl.BlockSpec((1, 16, d), lambda i, j: (i // tpb, 0, 0)),
                  _resident((1, d)), _resident((1, d)),
                  pl.BlockSpec((1, d, tf), lambda i, j: (layer, 0, j)),
                  pl.BlockSpec((1, d, tf), lambda i, j: (layer, 0, nf + j)),
                  pl.BlockSpec((1, tf, d), lambda i, j: (layer, j, 0))],
        out_specs=pl.BlockSpec((tm, d), lambda i, j: (i, 0)),
        out_shape=jax.ShapeDtypeStruct((m, d), F32),
        scratch_shapes=[pltpu.VMEM((tm, d), BF16), pltpu.VMEM((tm, d), F32)],
        compiler_params=_cparams(2),
        name="ffn",
    )(xs, modl, g2, g3, w_in, w_in, w_out)


def _cmlp_tile(u_ref, v_ref, lng_ref, lnb_ref, ws_ref, bsb_ref, gc_ref, o_ref):
    tc, gw = u_ref.shape[1], u_ref.shape[2]
    cw = gw // CM_GROUPS
    u = _gelu_tanh(u_ref[0])
    v = _gelu_tanh(v_ref[0])
    vc = v - jnp.mean(v, axis=-1, keepdims=True)
    vn = vc * lax.rsqrt(jnp.mean(vc * vc, axis=-1, keepdims=True) + EPS) * lng_ref[...] + lnb_ref[...]
    vb = vn.astype(BF16)
    rows = []
    for ch in range(tc // CM_CHUNK):
        cols = []
        for g in range(CM_GROUPS):
            blk = vb[ch * CM_CHUNK:(ch + 1) * CM_CHUNK, g * cw:(g + 1) * cw]
            cols.append(_dot(ws_ref[g], blk) + bsb_ref[:, g * cw:(g + 1) * cw])
        rows.append(jnp.concatenate(cols, axis=1))
    s = jnp.concatenate(rows, axis=0)
    o_ref[0] = _rms(u * s, gc_ref[...]).astype(BF16)


def _group_mean_sq(x, ones_bd):
    sq = x * x
    hi = sq.astype(BF16)
    lo = (sq - hi.astype(F32)).astype(BF16)
    return (_dot(hi, ones_bd) + _dot(lo, ones_bd)) * (1.0 / HEAD_DIM)


def _rope(x, cos, sin_signed):
    w = x.shape[1]
    lane = lax.broadcasted_iota(jnp.int32, x.shape, 1)
    first = (lane % (HEAD_DIM // 2)) < (HEAD_DIM // 4)
    partner = jnp.where(first, pltpu.roll(x, w - HEAD_DIM // 4, 1), pltpu.roll(x, HEAD_DIM // 4, 1))
    return x * cos + partner * sin_signed


def _attn_pairs(q, k_scr, v_scr, sink_ref, pairs, *, mode, latent_tile, i, tq, seq_t, ctx_len):
    lo = lax.broadcasted_iota(jnp.int32, (tq, LANES), 1) < HEAD_DIM
    lane1 = lax.broadcasted_iota(jnp.int32, (1, LANES), 1)
    row2 = lax.broadcasted_iota(jnp.int32, (2 * tq, 1), 0)

    key_sets = [(0, ctx_len, None)]
    if latent_tile and mode == "window":
        wk = tq + 2 * WINDOW
        start = pl.multiple_of(jnp.clip(i * tq - WINDOW, ctx_len, seq_t - wk), LANES)
        r = lax.broadcasted_iota(jnp.int32, (2 * tq, wk), 0)
        r = jnp.where(r >= tq, r - tq, r)
        c = lax.broadcasted_iota(jnp.int32, (2 * tq, wk), 1)
        dist = (i * tq + r) - (start + c)
        key_sets.append((start, wk, (jnp.abs(dist) <= WINDOW)))
    elif latent_tile:
        key_sets = [(0, seq_t, None)]

    outs = []
    for j in pairs:
        qp = q[:, j * LANES:(j + 1) * LANES]
        q2 = jnp.concatenate([jnp.where(lo, qp, 0.0), jnp.where(lo, 0.0, qp)], axis=0).astype(BF16)
        scores = []
        for k0, kn, valid in key_sets:
            s = _dot_nt(q2, k_scr[pl.ds(k0, kn), :])
            scores.append(s if valid is None else jnp.where(valid, s, NEG_BIG))
        mx = functools.reduce(jnp.maximum, [jnp.max(s, axis=1, keepdims=True) for s in scores])
        if mode == "window":
            sv = sink_ref[:, j * LANES:(j + 1) * LANES]
            s_lo = jnp.max(jnp.where(lane1 < HEAD_DIM, sv, NEG_BIG), axis=1, keepdims=True)
            s_hi = jnp.max(jnp.where(lane1 < HEAD_DIM, NEG_BIG, sv), axis=1, keepdims=True)
            sk = jnp.where(row2 < tq, s_lo, s_hi) * LOG2_E
            mx = jnp.maximum(mx, sk)
            den = jnp.exp2(sk - mx)
        else:
            den = 0.0
        o2 = 0.0
        for (k0, kn, _), s in zip(key_sets, scores):
            p = jnp.exp2(s - mx)
            den = den + jnp.sum(p, axis=1, keepdims=True)
            o2 = o2 + _dot(p.astype(BF16), v_scr[pl.ds(k0, kn), :])
        o2 = o2 / den
        outs.append(jnp.where(lo, o2[:tq], o2[tq:]))
    return outs


def _mixers_kernel(bq_ref, dq_ref, cu_ref, cv_ref, bkv_ref, dkv_ref, cq_ref, sq_ref, ck_ref, sk_ref,
                   qg_ref, kg_ref, onesq_ref, onesk_ref, sink_ref, gb_ref, gd_ref,
                   lng_ref, lnb_ref, ws_ref, bsb_ref, gc_ref,
                   yb_ref, yc_ref, yd_ref, kb_scr, vb_scr, kd_scr, vd_scr, *, tq, seq_t, ctx_len, need_ctx):
    i = pl.program_id(1)
    n_ctx_tiles = ctx_len // tq

    @pl.when(i == 0)
    def _():
        kv = bkv_ref[0]
        kb_scr[...] = _rope(kv[:, :LANES], ck_ref[...], sk_ref[...]).astype(BF16)
        vb_scr[...] = kv[:, LANES:].astype(BF16)
        kv = dkv_ref[0]
        k = kv[:, :LANES]
        k = k * lax.rsqrt(_group_mean_sq(k, onesk_ref[...]) + EPS) * kg_ref[...]
        kd_scr[...] = _rope(k, ck_ref[...], sk_ref[...]).astype(BF16)
        vd_scr[...] = kv[:, LANES:].astype(BF16)

    n_pairs = bq_ref.shape[2] // LANES
    half = n_pairs // 2

    def mix(latent_tile):
        qb = _rope(bq_ref[0], cq_ref[...], sq_ref[...]) * (ATTN_SCALE * LOG2_E)
        qd = dq_ref[0]
        qd = qd * lax.rsqrt(_group_mean_sq(qd, onesq_ref[...]) + EPS) * qg_ref[...]
        qd = _rope(qd, cq_ref[...], sq_ref[...]) * (ATTN_SCALE * LOG2_E)
        common = dict(latent_tile=latent_tile, i=i, tq=tq, seq_t=seq_t, ctx_len=ctx_len)
        win = functools.partial(_attn_pairs, qb, kb_scr, vb_scr, sink_ref, mode="window", **common)
        ob = win(range(half))
        od = _attn_pairs(qd, kd_scr, vd_scr, None, range(n_pairs), mode="global", **common)
        ob = ob + win(range(half, n_pairs))
        yb_ref[0] = _rms(jnp.concatenate(ob, axis=1), gb_ref[...]).astype(BF16)
        yd_ref[0] = _rms(jnp.concatenate(od, axis=1), gd_ref[...]).astype(BF16)
        _cmlp_tile(cu_ref, cv_ref, lng_ref, lnb_ref, ws_ref, bsb_ref, gc_ref, yc_ref)

    def no_ctx_output():
        for y_ref in (yb_ref, yc_ref, yd_ref):
            y_ref[...] = jnp.zeros_like(y_ref)

    pl.when(i < n_ctx_tiles)(functools.partial(mix, False) if need_ctx else no_ctx_output)
    pl.when(i >= n_ctx_tiles)(functools.partial(mix, True))


def _mixers(p, tabs, qg, kg, sink, gb, gd, lng, lnb, ws, bsb, gc, *, blocks, ctx_len, need_ctx):
    b, t, _ = p.shape
    gw = gb.shape[1]
    tq = _pick(t, (256, 128))
    assert ctx_len % tq == 0 and tq % CM_CHUNK == 0
    cq, sq, ck, sk, ones_q, ones_k = tabs
    kern = functools.partial(_mixers_kernel, tq=tq, seq_t=t, ctx_len=ctx_len, need_ctx=need_ctx)
    const = lambda a: _resident(a.shape)
    qspec = lambda blk: pl.BlockSpec((1, tq, gw), lambda bi, i: (bi, i, blk))
    kvspec = lambda blk: pl.BlockSpec((1, t, 2 * LANES), lambda bi, i: (bi, 0, blk))
    tabspec = pl.BlockSpec((tq, gw), lambda bi, i: (i, 0))
    out = jax.ShapeDtypeStruct((b, t, gw), BF16)
    ospec = pl.BlockSpec((1, tq, gw), lambda bi, i: (bi, i, 0))
    kvscr = pltpu.VMEM((t, LANES), BF16)
    return pl.pallas_call(
        kern,
        grid=(b, t // tq),
        in_specs=[qspec(blocks["b_q"]), qspec(blocks["d_q"]), qspec(blocks["c_u"]), qspec(blocks["c_v"]),
                  kvspec(blocks["b_kv"]), kvspec(blocks["d_kv"]),
                  tabspec, tabspec, const(ck), const(sk), const(qg), const(kg),
                  const(ones_q), const(ones_k), const(sink), const(gb), const(gd),
                  const(lng), const(lnb), const(ws), const(bsb), const(gc)],
        out_specs=[ospec, ospec, ospec],
        out_shape=[out, out, out],
        scratch_shapes=[kvscr, kvscr, kvscr, kvscr],
        compiler_params=_cparams(2),
        name="mixers_bcd",
    )(p, p, p, p, p, p, cq, sq, ck, sk, qg, kg, ones_q, ones_k, sink, gb, gd, lng, lnb, ws, bsb, gc)


GLA_LEVELS = int(math.log2(GLA_CHUNK))
GLA_ROWS = GLA_LEVELS * GLA_CHUNK
GLA_STEPS_PER_ITER = 4


def _gla_constants(dk):
    n = GLA_CHUNK
    csum = np.zeros((2, GLA_ROWS, n), np.float32)
    qmask = np.zeros((2, GLA_LEVELS + 1, n, 1), np.float32)
    kmask = np.zeros((2, GLA_LEVELS + 1, n, 1), np.float32)
    smask = np.zeros((2, GLA_LEVELS + 1, n, n), np.float32)
    for d in range(2):
        tau = np.arange(n) if d == 0 else n - 1 - np.arange(n)
        qmask[d, 0] = 1.0
        kmask[d, 0] = 1.0
        smask[d, 0] = np.eye(n)
        for lv in range(1, GLA_LEVELS + 1):
            h = 1 << (lv - 1)
            blk = tau // (2 * h)
            upper = (tau % (2 * h)) >= h
            same = blk[:, None] == blk[None, :]
            both_up = upper[:, None] & upper[None, :]
            both_lo = (~upper[:, None]) & (~upper[None, :])
            c = np.where(upper[:, None],
                         same & both_up & (tau[None, :] <= tau[:, None]),
                         same & both_lo & (tau[None, :] > tau[:, None]))
            csum[d, (lv - 1) * n:lv * n] = c
            qmask[d, lv, :, 0] = upper
            kmask[d, lv, :, 0] = ~upper
            smask[d, lv] = same & upper[:, None] & (~upper[None, :])
    hk = GLA_HEADS * dk
    qmask = np.broadcast_to(qmask, (2, GLA_LEVELS + 1, n, hk)).copy()
    kmask = np.broadcast_to(kmask, (2, GLA_LEVELS + 1, n, hk)).copy()
    smask = np.tile(smask, (1, 1, 1, GLA_HEADS))
    return csum, qmask, kmask, smask


def _gla_kernel(pa_ref, lr_ref, gw_ref, gb_ref, gain_ref, cs_ref, qm_ref, km_ref, sm_ref,
                o_ref, la_scr, oacc_scr, st_scr, *, seq_t, ctx_len, dk, dv):
    hk = GLA_HEADS * dk
    hv = GLA_HEADS * dv
    n = GLA_CHUNK
    nc = seq_t // n
    nc_ctx = ctx_len // n

    z = _dot(lr_ref[0].astype(BF16), gw_ref[...]) + gb_ref[...]
    la_scr[...] = (jnp.minimum(z, 0.0) - jnp.log1p(jnp.exp(-jnp.abs(z)))) * (LOG2_E / GLA_TAU)
    oacc_scr[...] = jnp.zeros_like(oacc_scr)
    st_scr[...] = jnp.zeros_like(st_scr)

    lane_head = lax.broadcasted_iota(jnp.int32, (n, hk), 1) // dk
    head_lanes = [lane_head == h for h in range(GLA_HEADS)]

    def per_head_rows(x):
        return jnp.concatenate([jnp.where(m, x, 0.0) for m in head_lanes], axis=0).astype(BF16)

    steps = GLA_STEPS_PER_ITER if nc % GLA_STEPS_PER_ITER == 0 else 2
    assert nc % steps == 0
    chains = [(d, u) for u in range(steps) for d in (0, 1)]

    def scan_iter(it, carry):
        r0 = []
        for d, u in chains:
            s = it * steps + u
            c = s if d == 0 else jnp.where(s < nc_ctx, nc_ctx - 1 - s, nc - 1 - (s - nc_ctx))
            r0.append(pl.multiple_of(c * n, n))
        q = [pa_ref[0, pl.ds(r, n), 0:hk] * (dk ** -0.5) for r in r0]
        k = [pa_ref[0, pl.ds(r, n), hk:2 * hk] for r in r0]
        v = [pa_ref[0, pl.ds(r, n), 2 * hk:2 * hk + hv] for r in r0]
        ex2 = []
        a_first = []
        for (d, _), r in zip(chains, r0):
            a = la_scr[pl.ds(r, n), d * hk:(d + 1) * hk]
            a_hi = a.astype(BF16)
            a_lo = (a - a_hi.astype(F32)).astype(BF16)
            ex2.append(_dot(cs_ref[d], jnp.concatenate([a_hi, a_lo], axis=1)))
            a_first.append(a[0:1] if d == 0 else a[n - 1:n])
        ex = [x[:, :hk] + x[:, hk:] for x in ex2]
        e = [jnp.exp2(x) for x in ex]

        e_pre, e_suf, e_last = [], [], []
        for ci, (d, _) in enumerate(chains):
            top = ex[ci][(GLA_LEVELS - 1) * n:GLA_LEVELS * n]
            first, final = (0, n - 1) if d == 0 else (n - 1, 0)
            s_lo = top[first:first + 1] + a_first[ci]
            s_up = top[final:final + 1]
            signed = (2.0 * qm_ref[d, GLA_LEVELS] - 1.0) * top
            e_pre.append(jnp.exp2(s_lo + signed))
            e_suf.append(jnp.exp2(s_up - signed))
            e_last.append(jnp.exp2(s_lo + s_up))

        scores = [jnp.zeros((n, GLA_HEADS * n), F32) for _ in chains]
        for lv in range(GLA_LEVELS + 1):
            for ci, (d, _) in enumerate(chains):
                if lv == 0:
                    ql, kl = q[ci], k[ci]
                else:
                    el = e[ci][(lv - 1) * n:lv * n]
                    ql = q[ci] * el * qm_ref[d, lv]
                    kl = k[ci] * el * km_ref[d, lv]
                scores[ci] = scores[ci] + _dot_nt(ql.astype(BF16), per_head_rows(kl)) * sm_ref[d, lv]

        kt = []
        for ci in range(len(chains)):
            kh = k[ci] * e_suf[ci]
            kt.append(jnp.concatenate([kh, jnp.broadcast_to(e_last[ci], (n, hk))], axis=0).T)
        dstate = []
        o_intra = []
        for ci in range(len(chains)):
            vpad = jnp.concatenate([v[ci], jnp.zeros_like(v[ci])], axis=0).astype(BF16)
            full = _dot(kt[ci].astype(BF16), vpad)
            dstate.append(jnp.concatenate(
                [full[h * dk:(h + 1) * dk, h * dv:(h + 1) * dv] for h in range(GLA_HEADS)], axis=0))
            v_rows = jnp.concatenate([v[ci][:, h * dv:(h + 1) * dv] for h in range(GLA_HEADS)],
                                     axis=0).astype(BF16)
            o_intra.append(_dot(per_head_rows(scores[ci]), v_rows))

        st = [st_scr[0], st_scr[1]]
        for ci, (d, _) in enumerate(chains):
            qh = q[ci] * e_pre[ci]
            o_rows = o_intra[ci] + _dot(per_head_rows(qh), st[d].astype(BF16))
            oacc_scr[pl.ds(r0[ci], n), :] += jnp.concatenate(
                [o_rows[h * n:(h + 1) * n, :] for h in range(GLA_HEADS)], axis=1)
            st[d] = st[d] * kt[ci][:, n:n + 1] + dstate[ci]
        st_scr[0] = st[0]
        st_scr[1] = st[1]
        return carry

    lax.fori_loop(0, nc // steps, scan_iter, 0)

    tfin = _pick(seq_t, (256, 128))
    for t0 in range(0, seq_t, tfin):
        o = oacc_scr[t0:t0 + tfin, :]
        parts = [_rms(o[:, h * dv:(h + 1) * dv], gain_ref[:, h * dv:(h + 1) * dv])
                 for h in range(GLA_HEADS)]
        gate = pa_ref[0, t0:t0 + tfin, 2 * hk + hv:2 * hk + 2 * hv]
        o_ref[0, t0:t0 + tfin, :] = (jnp.concatenate(parts, axis=1) * _silu(gate)).astype(BF16)


def _gla(p, gw, gb, gain, consts, *, ctx_len, dk, dv):
    b, t, n = p.shape
    hk = GLA_HEADS * dk
    hv = GLA_HEADS * dv
    pa_w = 2 * hk + 2 * hv
    lr_blk = n // LANES - 1
    cs, qm, km, sm = consts
    kern = functools.partial(_gla_kernel, seq_t=t, ctx_len=ctx_len, dk=dk, dv=dv)
    const = lambda a: _resident(a.shape)
    return pl.pallas_call(
        kern,
        grid=(b,),
        in_specs=[pl.BlockSpec((1, t, pa_w), lambda bi: (bi, 0, 0)),
                  pl.BlockSpec((1, t, LANES), lambda bi: (bi, 0, lr_blk)),
                  const(gw), const(gb), const(gain), const(cs), const(qm), const(km), const(sm)],
        out_specs=pl.BlockSpec((1, t, hv), lambda bi: (bi, 0, 0)),
        out_shape=jax.ShapeDtypeStruct((b, t, hv), BF16),
        scratch_shapes=[pltpu.VMEM((t, 2 * hk), F32), pltpu.VMEM((t, hv), F32),
                        pltpu.VMEM((2, hk, dv), F32)],
        compiler_params=_cparams(1),
        name="gla",
    )(p, p, gw, gb, gain, cs, qm, km, sm)


def _rope_tables(length, ctx_len, n_heads):
    rows = length // GRID_W
    row = jnp.repeat(jnp.arange(rows), GRID_W).astype(F32)
    col = jnp.tile(jnp.arange(GRID_W), rows).astype(F32)
    quarter = HEAD_DIM // 4
    inv = ROPE_THETA ** (-jnp.arange(quarter, dtype=F32) / quarter)
    ar, ac = row[:, None] * inv, col[:, None] * inv
    ang = jnp.concatenate([ar, ar, ac, ac], axis=-1)
    cos, sin = jnp.cos(ang), jnp.sin(ang)
    first = (jnp.arange(HEAD_DIM) % (HEAD_DIM // 2)) < quarter
    sin_signed = jnp.where(first[None, :], -sin, sin)
    cos = jnp.concatenate([jnp.ones((ctx_len, HEAD_DIM), F32), cos], axis=0)
    sin_signed = jnp.concatenate([jnp.zeros((ctx_len, HEAD_DIM), F32), sin_signed], axis=0)
    return jnp.tile(cos, (1, n_heads)), jnp.tile(sin_signed, (1, n_heads))


def _pair_heads(a, axis, n_kv):
    axis = axis % a.ndim
    g = a.shape[axis] // (n_kv * HEAD_DIM)
    shape = a.shape[:axis] + (n_kv, g, HEAD_DIM) + a.shape[axis + 1:]
    return jnp.swapaxes(a.reshape(shape), axis, axis + 1).reshape(a.shape)


def _block_diag_mask(rows, cols, rb, cb):
    r = np.arange(rows)[:, None] // rb
    c = np.arange(cols)[None, :] // cb
    return (r == c).astype(np.float32)


def kernel(x, c, ctx, c_ctx, w_mod, b_mod, norm_g, w_in, gla_gate_up, gla_gate_b, win_sink, cm_ln_g,
           cm_ln_b, cm_ws, cm_bs, qk_g, mix_g, w_out, w_ffn_in, w_ffn_out):
    bsz, seq, d = x.shape
    ctx_len = ctx.shape[1]
    seq_t = ctx_len + seq
    depth = w_mod.shape[0]
    gw = d // 4
    rank = gla_gate_up.shape[2]
    dk = gla_gate_up.shape[3] // GLA_HEADS
    dv = gw // GLA_HEADS
    n_heads = gw // HEAD_DIM
    kvw = WIN_KV * HEAD_DIM
    hk = GLA_HEADS * dk
    assert kvw == LANES and GA_KV == WIN_KV and gw % LANES == 0 and 2 * rank <= LANES
    assert dk == GLA_CHUNK and 2 * hk == gw
    assert ctx_len % CM_CHUNK == 0 and seq % CM_CHUNK == 0 and seq % GRID_W == 0
    assert ctx_len % ROW_CHUNK == 0

    sizes = [hk, hk, gw, gw, rank, rank, gw, kvw, kvw, gw, gw, gw, kvw, kvw]
    off = [int(v) for v in np.concatenate([[0], np.cumsum(sizes)])]
    blocks = {"c_u": 3, "c_v": 4, "b_q": 5, "d_q": 6,
              "b_kv": (7 * gw) // (2 * LANES), "d_kv": (7 * gw) // (2 * LANES) + 1}

    cos_q, sin_q = _rope_tables(seq, ctx_len, n_heads)
    cos_k, sin_k = cos_q[:, :LANES], sin_q[:, :LANES]
    ones_q = jnp.asarray(_block_diag_mask(gw, gw, HEAD_DIM, HEAD_DIM), BF16)
    ones_k = ones_q[:LANES, :LANES]
    tabs = (cos_q, sin_q, cos_k, sin_k, ones_q, ones_k)

    csum, qmask, kmask, smask = _gla_constants(dk)
    gla_consts = (jnp.asarray(csum, BF16), jnp.asarray(qmask), jnp.asarray(kmask), jnp.asarray(smask))

    rows = 16
    cond = jnp.concatenate([c, c_ctx[None, :], jnp.zeros((rows - bsz - 1, d), F32)], axis=0)
    mod_all = _modulation(cond, w_mod, b_mod)
    mod_lat = mod_all[:, :bsz].reshape(depth, bsz, 6, d)
    mod_ctx = jnp.broadcast_to(mod_all[:, bsz].reshape(depth, 1, 6, d), (depth, bsz, 6, d))
    pad2 = jnp.zeros((depth, bsz, 2, d), F32)
    mods = jnp.concatenate([mod_ctx, pad2, mod_lat, pad2], axis=2)

    xs = jnp.concatenate([ctx, x], axis=1).reshape(bsz * seq_t, d)

    w_proj = _prep_w_in(w_in, off, rank)
    zeros_k = jnp.zeros((depth, rank, hk), F32)
    gate_w = jnp.concatenate(
        [jnp.concatenate([gla_gate_up[:, 0], zeros_k], axis=2),
         jnp.concatenate([zeros_k, gla_gate_up[:, 1]], axis=2),
         jnp.zeros((depth, LANES - 2 * rank, 2 * hk), F32)], axis=1).astype(BF16)
    gate_b = gla_gate_b.reshape(depth, 1, 2 * hk)
    g_a, g_c = mix_g[:, None, 0:gw], mix_g[:, None, 2 * gw:3 * gw]
    g_b = _pair_heads(mix_g[:, None, gw:2 * gw], 2, WIN_KV)
    g_d = _pair_heads(mix_g[:, None, 3 * gw:], 2, GA_KV)
    w_o = jnp.concatenate([w_out[:, 0:gw], _pair_heads(w_out[:, gw:2 * gw], 1, WIN_KV),
                           w_out[:, 2 * gw:3 * gw], _pair_heads(w_out[:, 3 * gw:], 1, GA_KV)],
                          axis=1).astype(BF16)
    sink = _pair_heads(jnp.repeat(win_sink, HEAD_DIM, axis=1)[:, None, :], 2, WIN_KV)
    q_gain = jnp.tile(qk_g[:, 0:1], (1, 1, n_heads))
    k_gain = jnp.tile(qk_g[:, 1:2], (1, 1, WIN_KV))
    bsb = jnp.repeat(jnp.swapaxes(cm_bs, 1, 2), gw // CM_GROUPS, axis=2)
    ws = cm_ws.astype(BF16)
    w_up, w_down = w_ffn_in.astype(BF16), w_ffn_out.astype(BF16)
    ln_g, ln_b = cm_ln_g[:, None, :], cm_ln_b[:, None, :]

    for l in range(depth):
        modl = mods[l]
        p = _inproj(xs, modl, norm_g[l, 0:1], w_proj, l, seq_t=seq_t, ctx_len=ctx_len)
        p3 = p.reshape(bsz, seq_t, -1)
        last = l == depth - 1
        y_a = _gla(p3, gate_w[l], gate_b[l], g_a[l], gla_consts, ctx_len=ctx_len, dk=dk, dv=dv)
        y_b, y_c, y_d = _mixers(p3, tabs, q_gain[l], k_gain[l], sink[l], g_b[l], g_d[l], ln_g[l], ln_b[l],
                                ws[l], bsb[l], g_c[l], blocks=blocks, ctx_len=ctx_len, need_ctx=not last)
        ys = [y.reshape(bsz * seq_t, gw) for y in (y_a, y_b, y_c, y_d)]
        xs = _outproj(xs, ys, modl, norm_g[l, 1:2], w_o, l, seq_t=seq_t, ctx_len=ctx_len,
                      latent_only=last)
        xs = _ffn(xs, modl, norm_g[l, 2:3], norm_g[l, 3:4], w_up, w_down, l,
                  rows_per_batch=seq if last else seq_t, row_base=ctx_len if last else 0,
                  ctx_len=ctx_len)

    return xs.reshape(bsz, seq, d)
```
